```python
import math
import jax, jax.numpy as jnp
from jax import lax
import numpy as np

D_MODEL = 2048
BATCH = 2
SEQ = 8192
DEPTH = 1
DEC_BATCH = 32
DEC_SEQ = 8
PAST_LEN = 16384
PAGE_SIZE = 128

RET_HEADS = 8
RET_DK = 128
RET_DV = 128
RET_WIDTH = RET_HEADS * RET_DV
RET_CHUNK = 128
ATT_GROUPS = ((128, 1), (512, 4), (2048, 16))
N_GROUPS = 3
ATT_HG = 4
ATT_HD = 128
ATT_GW = ATT_HG * ATT_HD
ATT_SPAN = 128
ATT_BLK = 128
ALIBI_MAX = 8.0
D_FF = 5632
CONV_W = 3
EPS = 1e-6
NEG = -1e30
IN_COLS = 4 * RET_WIDTH + 3 * N_GROUPS * ATT_GW + 2 * D_MODEL

kernel_name = 'retention_dilated_swa_convffn_step'


def _rmsnorm(x, g):
    xf = x.astype(jnp.float32)
    xf = xf * lax.rsqrt(jnp.mean(xf * xf, axis=-1, keepdims=True) + EPS)
    return xf.astype(x.dtype) * g


def _split_offsets():
    sizes = [RET_WIDTH] * 4 + [ATT_GW] * (3 * N_GROUPS) + [D_MODEL] * 2
    offs, acc = [], 0
    for s in sizes[:-1]:
        acc += s
        offs.append(acc)
    return offs


def _log_gamma():
    return jnp.log(1.0 - jnp.exp2(-5.0 - jnp.arange(RET_HEADS, dtype=jnp.float32)))


def _alibi_slopes():
    n = N_GROUPS * ATT_HG
    h = jnp.arange(1, n + 1, dtype=jnp.float32)
    return jnp.exp2(-ALIBI_MAX * h / n).reshape(N_GROUPS, ATT_HG)


def _retention_chunk(state, q, k, v):
    L = q.shape[1]
    lg = _log_gamma()
    pos = jnp.arange(L, dtype=jnp.float32)
    diff = pos[:, None] - pos[None, :]
    dmask = jnp.where(diff >= 0, jnp.exp(jnp.maximum(diff, 0.0)[None] * lg[:, None, None]), 0.0)
    scores = jnp.einsum('bihd,bjhd->bhij', q, k) * dmask
    inner = jnp.einsum('bhij,bjhe->bihe', scores, v)
    q_decay = jnp.exp((pos[:, None] + 1.0) * lg[None])
    cross = jnp.einsum('bihd,bhde->bihe', q, state) * q_decay[None, :, :, None]
    k_decay = jnp.exp((L - 1.0 - pos)[:, None] * lg[None])
    new_state = (state * jnp.exp(L * lg)[None, :, None, None]
                 + jnp.einsum('bjhd,bjhe->bhde', k * k_decay[None, :, :, None], v))
    return new_state, inner + cross


def _retention_prompt(q, k, v):
    B, S, H, _ = q.shape
    nc = S // RET_CHUNK
    def to_chunks(t):
        return t.reshape(B, nc, RET_CHUNK, H, t.shape[-1]).swapaxes(0, 1)
    state0 = jnp.zeros((B, H, RET_DK, RET_DV), jnp.float32)
    def step(st, qkv):
        return _retention_chunk(st, qkv[0], qkv[1], qkv[2])
    st, out = lax.scan(step, state0, (to_chunks(q), to_chunks(k), to_chunks(v)))
    return st, out.swapaxes(0, 1).reshape(B, S, H, RET_DV)


def _retention_branch(rq, rk, rv, rg, gn_ret, w_ret_o, state):
    B, L, _ = rq.shape
    q = rq.reshape(B, L, RET_HEADS, RET_DK).astype(jnp.float32)
    k = rk.reshape(B, L, RET_HEADS, RET_DK).astype(jnp.float32) * (RET_DK ** -0.5)
    v = rv.reshape(B, L, RET_HEADS, RET_DV).astype(jnp.float32)
    if state is None:
        new_state, y = _retention_prompt(q, k, v)
    else:
        new_state, y = _retention_chunk(state.astype(jnp.float32), q, k, v)
    mu = jnp.mean(y, axis=-1, keepdims=True)
    yc = y - mu
    yn = yc * lax.rsqrt(jnp.mean(yc * yc, axis=-1, keepdims=True) + EPS)
    yn = yn.reshape(B, L, RET_WIDTH).astype(rq.dtype) * gn_ret
    return (jax.nn.silu(rg) * yn) @ w_ret_o, new_state


def _dilated_prompt(q, k, v, dil, slopes):
    B, S, H, hd = q.shape
    L = S // dil
    nb = -(-L // ATT_BLK)
    Lp = nb * ATT_BLK
    N = B * dil
    def sub(t):
        t = t.reshape(B, L, dil, H, hd).swapaxes(1, 2).reshape(N, L, H, hd)
        return jnp.pad(t, ((0, 0), (0, Lp - L), (0, 0), (0, 0)))
    def band(t):
        t = jnp.pad(t, ((0, 0), (ATT_BLK, 0), (0, 0), (0, 0))).reshape(N, nb + 1, ATT_BLK, H, hd)
        return jnp.concatenate([t[:, :-1], t[:, 1:]], axis=2)
    qb = sub(q).reshape(N, nb, ATT_BLK, H, hd)
    kb = band(sub(k))
    vb = band(sub(v))
    r = jnp.arange(ATT_BLK)[:, None]
    c = jnp.arange(2 * ATT_BLK)[None, :]
    dist = ATT_BLK + r - c
    key_idx = jnp.arange(nb)[:, None, None] * ATT_BLK - ATT_BLK + c[None]
    valid = (dist >= 0)[None] & (dist <= ATT_SPAN)[None] & (key_idx >= 0)
    bias = -slopes[:, None, None] * (dil * dist).astype(jnp.float32)[None]
    s = jnp.einsum('nbqhd,nbkhd->nbhqk', qb, kb).astype(jnp.float32) * (hd ** -0.5) + bias[None, None]
    s = jnp.where(valid[None, :, None], s, NEG)
    lse = jax.nn.logsumexp(s, axis=-1)
    p = jnp.exp(s - lse[..., None])
    o = jnp.einsum('nbhqk,nbkhd->nbqhd', p.astype(v.dtype), vb)
    o = o.reshape(N, Lp, H, hd)[:, :L].reshape(B, dil, L, H, hd).swapaxes(1, 2).reshape(B, S, H, hd)
    lse = lse.transpose(0, 1, 3, 2).reshape(N, Lp, H)[:, :L]
    lse = lse.reshape(B, dil, L, H).swapaxes(1, 2).reshape(B, S, H)
    return o, lse


def _dilated_sample(q, k, v, buf, dil, slopes):
    B, T, H, hd = q.shape
    Wb = buf.shape[1]
    buf = buf.astype(k.dtype)
    kall = jnp.concatenate([buf[:, :, 0], k], axis=1)
    vall = jnp.concatenate([buf[:, :, 1], v], axis=1)
    j = jnp.arange(ATT_SPAN + 1)
    idx = Wb + jnp.arange(T)[:, None] - dil * j[None]
    valid = idx >= 0
    idx = jnp.maximum(idx, 0)
    kg = kall[:, idx]
    vg = vall[:, idx]
    bias = -slopes[:, None, None] * (dil * j).astype(jnp.float32)[None, None]
    s = jnp.einsum('bthd,btjhd->bhtj', q, kg).astype(jnp.float32) * (hd ** -0.5) + bias[None]
    s = jnp.where(valid[None, None], s, NEG)
    lse = jax.nn.logsumexp(s, axis=-1)
    p = jnp.exp(s - lse[..., None])
    o = jnp.einsum('bhtj,btjhd->bthd', p.astype(v.dtype), vg)
    new_buf = jnp.concatenate([buf, jnp.stack([k, v], axis=2)], axis=1)[:, T:]
    return o, lse.swapaxes(1, 2), new_buf


def _dilated_branch(att_cols, kv_bufs, w_att_o):
    B, L, _ = att_cols[0].shape
    slopes = _alibi_slopes()
    outs, lses, new_bufs = [], [], []
    for g, (win, dil) in enumerate(ATT_GROUPS):
        q, k, v = (t.reshape(B, L, ATT_HG, ATT_HD) for t in att_cols[3 * g:3 * g + 3])
        if kv_bufs is None:
            o, lse = _dilated_prompt(q, k, v, dil, slopes[g])
            nbuf = jnp.stack([k, v], axis=2)[:, -min(win, L):]
        else:
            o, lse, nbuf = _dilated_sample(q, k, v, kv_bufs[g], dil, slopes[g])
        outs.append(o)
        lses.append(lse)
        new_bufs.append(nbuf)
    wts = jax.nn.softmax(jnp.stack(lses, axis=0), axis=0)
    o = jnp.einsum('gblh,gblhd->blhd', wts.astype(outs[0].dtype), jnp.stack(outs, axis=0))
    return o.reshape(B, L, ATT_GW) @ w_att_o, new_bufs


def _conv_ffn(h, w_up, conv_w, conv_b, w_down, prev):
    L = h.shape[1]
    u = h @ w_up
    ext = jnp.concatenate([prev.astype(u.dtype), u], axis=1)
    c = conv_b + sum(ext[:, t:t + L] * conv_w[t] for t in range(CONV_W))
    a, b = jnp.split(c, 2, axis=-1)
    return (jax.nn.gelu(a, approximate=False) * b) @ w_down, ext[:, -(CONV_W - 1):]


def _layer(x, ret_state, kv_bufs, conv_state, g_mix, w_in, gn_ret, w_ret_o, w_att_o, w_o,
           g_ffn, w_up, conv_w, conv_b, w_down):
    h = _rmsnorm(x, g_mix)
    parts = jnp.split(h @ w_in, _split_offsets(), axis=-1)
    rq, rk, rv, rg = parts[0:4]
    att_cols = parts[4:4 + 3 * N_GROUPS]
    gate_ret, gate_att = parts[4 + 3 * N_GROUPS:]
    ret_out, new_ret = _retention_branch(rq, rk, rv, rg, gn_ret, w_ret_o, ret_state)
    att_out, new_bufs = _dilated_branch(att_cols, kv_bufs, w_att_o)
    merged = jax.nn.sigmoid(gate_ret) * ret_out + jax.nn.sigmoid(gate_att) * att_out
    x = x + merged @ w_o
    if conv_state is None:
        conv_state = jnp.zeros((x.shape[0], CONV_W - 1, 2 * D_FF), x.dtype)
    ffn_out, new_conv = _conv_ffn(_rmsnorm(x, g_ffn), w_up, conv_w, conv_b, w_down, conv_state)
    return x + ffn_out, new_ret, new_bufs, new_conv


def setup_inputs(seed: int = 0) -> dict:
    key = jax.random.key(seed)
    ks = jax.random.split(key, 20)
    f32 = jnp.float32
    def nrm(k, shape, scale=1.0):
        return jax.random.normal(k, shape, f32) * scale
    F2 = 2 * D_FF
    return {
        'x_prompt': nrm(ks[0], (BATCH, SEQ, D_MODEL)),
        'x_sample': nrm(ks[1], (DEC_BATCH, DEC_SEQ, D_MODEL)),
        'state_ret': nrm(ks[2], (DEPTH, DEC_BATCH, RET_HEADS, RET_DK, RET_DV), 0.5),
        'cache_kv_w128': nrm(ks[3], (DEPTH, DEC_BATCH, min(ATT_GROUPS[0][0], PAST_LEN), 2, ATT_HG, ATT_HD)),
        'cache_kv_w512': nrm(ks[4], (DEPTH, DEC_BATCH, min(ATT_GROUPS[1][0], PAST_LEN), 2, ATT_HG, ATT_HD)),
        'cache_kv_w2048': nrm(ks[5], (DEPTH, DEC_BATCH, min(ATT_GROUPS[2][0], PAST_LEN), 2, ATT_HG, ATT_HD)),
        'state_conv': nrm(ks[6], (DEPTH, DEC_BATCH, CONV_W - 1, F2)),
        'g_mix': 1.0 + nrm(ks[7], (DEPTH, D_MODEL), 0.01),
        'w_in': nrm(ks[8], (DEPTH, D_MODEL, IN_COLS), D_MODEL ** -0.5),
        'gn_ret': 1.0 + nrm(ks[9], (DEPTH, RET_WIDTH), 0.01),
        'w_ret_o': nrm(ks[10], (DEPTH, RET_WIDTH, D_MODEL), RET_WIDTH ** -0.5),
        'w_att_o': nrm(ks[11], (DEPTH, ATT_GW, D_MODEL), ATT_GW ** -0.5),
        'w_o': nrm(ks[12], (DEPTH, D_MODEL, D_MODEL), D_MODEL ** -0.5),
        'g_ffn': 1.0 + nrm(ks[13], (DEPTH, D_MODEL), 0.01),
        'w_up': nrm(ks[14], (DEPTH, D_MODEL, F2), D_MODEL ** -0.5),
        'conv_w': nrm(ks[15], (DEPTH, CONV_W, F2), CONV_W ** -0.5),
        'conv_b': nrm(ks[16], (DEPTH, F2), 0.01),
        'w_down': nrm(ks[17], (DEPTH, D_FF, D_MODEL), D_FF ** -0.5),
        'g_final': 1.0 + nrm(ks[18], (D_MODEL,), 0.01),
    }


def reference(x_prompt, x_sample, state_ret, cache_kv_w128, cache_kv_w512, cache_kv_w2048, state_conv,
              g_mix, w_in, gn_ret, w_ret_o, w_att_o, w_o, g_ffn, w_up, conv_w, conv_b, w_down, g_final):
    xp, xs = x_prompt, x_sample
    ret_p, ret_s, conv_p, conv_s = [], [], [], []
    kv_p = [[] for _ in range(N_GROUPS)]
    kv_s = [[] for _ in range(N_GROUPS)]
    for l in range(DEPTH):
        wts = (g_mix[l], w_in[l], gn_ret[l], w_ret_o[l], w_att_o[l], w_o[l],
               g_ffn[l], w_up[l], conv_w[l], conv_b[l], w_down[l])
        xp, r, bufs, cv = _layer(xp, None, None, None, *wts)
        ret_p.append(r)
        conv_p.append(cv)
        for g in range(N_GROUPS):
            kv_p[g].append(bufs[g])
        xs, r, bufs, cv = _layer(xs, state_ret[l], (cache_kv_w128[l], cache_kv_w512[l], cache_kv_w2048[l]),
                                 state_conv[l], *wts)
        ret_s.append(r)
        conv_s.append(cv)
        for g in range(N_GROUPS):
            kv_s[g].append(bufs[g])
    y_prompt = _rmsnorm(xp, g_final)
    y_sample = _rmsnorm(xs, g_final)
    return (y_prompt, y_sample,
            jnp.stack(ret_p), jnp.stack(ret_s),
            jnp.stack(kv_p[0]), jnp.stack(kv_s[0]),
            jnp.stack(kv_p[1]), jnp.stack(kv_s[1]),
            jnp.stack(kv_p[2]), jnp.stack(kv_s[2]),
            jnp.stack(conv_p), jnp.stack(conv_s))
```

```python
import functools
import math

import jax
import jax.numpy as jnp
from jax import lax
from jax.experimental import pallas as pl
from jax.experimental.pallas import tpu as pltpu

D_MODEL = 2048
RET_HEADS = 8
RET_DK = 128
RET_DV = 128
RET_WIDTH = RET_HEADS * RET_DV
RET_CHUNK = 128
ATT_GROUPS = ((128, 1), (512, 4), (2048, 16))
N_GROUPS = 3
ATT_HG = 4
ATT_HD = 128
ATT_GW = ATT_HG * ATT_HD
ATT_SPAN = 128
ATT_BLK = 128
ALIBI_MAX = 8.0
D_FF = 5632
CONV_W = 3
EPS = 1e-6
NEG = -1e30
IN_COLS = 4 * RET_WIDTH + 3 * N_GROUPS * ATT_GW + 2 * D_MODEL

GATE_COLS = 2 * D_MODEL
RET_COL0 = GATE_COLS
ATT_COL0 = GATE_COLS + 4 * RET_WIDTH
PROJ_TN = 512
ATT_TILE0 = ATT_COL0 // PROJ_TN

MIB = 1024 * 1024
BF16 = jnp.bfloat16
F32 = jnp.float32


def _params(semantics, vmem_mib):
    return pltpu.CompilerParams(dimension_semantics=semantics, vmem_limit_bytes=vmem_mib * MIB)


def _dot(a, b):
    return jnp.dot(a, b, preferred_element_type=F32)


def _dot_nt(a, b):
    return lax.dot_general(a, b, (((1,), (1,)), ((), ())), preferred_element_type=F32)


def _rmsnorm_rows(x, g):
    ms = jnp.mean(x * x, axis=-1, keepdims=True)
    return x * lax.rsqrt(ms + EPS) * g


def _kv_slot(n):
    t = ATT_TILE0
    return ((n >= t + 2).astype(jnp.int32) + (n >= t + 3) + (n >= t + 5) + (n >= t + 6) + (n >= t + 8))


def _inproj_kernel(x_ref, g_ref, w_ref, *rest, want_kv):
    if want_kv:
        proj_ref, kv_ref, h_scr = rest
    else:
        proj_ref, h_scr = rest
    n = pl.program_id(1)

    @pl.when(n == 0)
    def _():
        h_scr[...] = _rmsnorm_rows(x_ref[...], g_ref[...]).astype(BF16)

    acc = _dot(h_scr[...], w_ref[...])
    proj_ref[...] = acc.astype(proj_ref.dtype)
    if want_kv:
        rel = n - ATT_TILE0
        is_kv = (rel == 1) | (rel == 2) | (rel == 4) | (rel == 5) | (rel == 7) | (rel == 8)

        @pl.when(is_kv)
        def _():
            kv_ref[...] = acc


def _inproj(x2d, g, w_bf16, *, tm, out_dtype, want_kv):
    M = x2d.shape[0]
    N = w_bf16.shape[1]
    tn = PROJ_TN
    out_shape = [jax.ShapeDtypeStruct((M, N), out_dtype)]
    out_specs = [pl.BlockSpec((tm, tn), lambda m, n: (m, n))]
    if want_kv:
        out_shape.append(jax.ShapeDtypeStruct((M, 2 * N_GROUPS * ATT_GW), F32))
        out_specs.append(pl.BlockSpec((tm, tn), lambda m, n: (m, _kv_slot(n))))
    return pl.pallas_call(
        functools.partial(_inproj_kernel, want_kv=want_kv),
        grid=(M // tm, N // tn),
        in_specs=[
            pl.BlockSpec((tm, D_MODEL), lambda m, n: (m, 0)),
            pl.BlockSpec((1, D_MODEL), lambda m, n: (0, 0)),
            pl.BlockSpec((D_MODEL, tn), lambda m, n: (0, n)),
        ],
        out_specs=out_specs,
        out_shape=out_shape,
        scratch_shapes=[pltpu.VMEM((tm, D_MODEL), BF16)],
        compiler_params=_params(("arbitrary", "arbitrary"), 48),
        name="inproj",
    )(x2d, g.reshape(1, D_MODEL), w_bf16)


def _ret_tables(lq):
    lg = jnp.log(1.0 - jnp.exp2(-5.0 - jnp.arange(RET_HEADS, dtype=F32)))
    scale = RET_DK ** -0.5
    i = jnp.arange(lq, dtype=F32)
    j = jnp.arange(RET_CHUNK, dtype=F32)
    diff = i[:, None] - j[None, :]
    live = (diff >= 0) & (j[None, :] < lq)
    dm = jnp.where(live[None], jnp.exp(jnp.maximum(diff, 0.0)[None] * lg[:, None, None]), 0.0) * scale
    qd = jnp.exp((i[None, :] + 1.0) * lg[:, None])
    qd = jnp.broadcast_to(qd[:, :, None], (RET_HEADS, lq, RET_DV))
    kd = jnp.where(j[None, :] < lq, jnp.exp(jnp.maximum(lq - 1.0 - j, 0.0)[None, :] * lg[:, None]), 0.0) * scale
    kd = jnp.broadcast_to(kd[:, :, None], (RET_HEADS, RET_CHUNK, RET_DK))
    gl = jnp.exp(lq * lg)
    return dm.astype(F32), qd.astype(F32), kd.astype(F32), gl.astype(F32)


def _ret_head(q, k, v, st, dm, qd, kd, gl):
    s = _dot_nt(q, k) * dm
    inner = _dot(s.astype(BF16), v)
    cross = _dot(q, st.astype(BF16)) * qd
    kt = (k.astype(F32) * kd).T.astype(BF16)
    new_st = st * gl + _dot(kt, v)
    return inner + cross, new_st


def _groupnorm_gate(y, gn, rg):
    mu = jnp.mean(y, axis=-1, keepdims=True)
    yc = y - mu
    yn = yc * lax.rsqrt(jnp.mean(yc * yc, axis=-1, keepdims=True) + EPS)
    return (rg * jax.nn.sigmoid(rg)) * (yn * gn)


def _ret_prompt_kernel(gl_ref, q_ref, k_ref, v_ref, rg_ref, dm_ref, qd_ref, kd_ref, gn_ref, y_ref, st_ref):
    c = pl.program_id(1)

    @pl.when(c == 0)
    def _():
        st_ref[...] = jnp.zeros_like(st_ref)

    for h in range(RET_HEADS):
        sl = slice(h * RET_DK, (h + 1) * RET_DK)
        y, new_st = _ret_head(q_ref[:, sl], k_ref[:, sl], v_ref[:, sl], st_ref[h],
                              dm_ref[h], qd_ref[h], kd_ref[h], gl_ref[h])
        st_ref[h] = new_st
        y_ref[:, sl] = _groupnorm_gate(y, gn_ref[:, sl], rg_ref[:, sl].astype(F32)).astype(y_ref.dtype)


def _ret_prompt(proj3, gn_ret):
    B, S, _ = proj3.shape
    L = RET_CHUNK
    dm, qd, kd, gl = _ret_tables(L)
    cb = RET_COL0 // RET_WIDTH
    col = lambda k: pl.BlockSpec((None, L, RET_WIDTH), lambda b, c, k=k: (b, c, cb + k))
    tab = pl.BlockSpec((RET_HEADS, L, RET_DK), lambda b, c: (0, 0, 0))
    return pl.pallas_call(
        _ret_prompt_kernel,
        grid=(B, S // L),
        in_specs=[
            pl.BlockSpec(memory_space=pltpu.SMEM),
            col(0), col(1), col(2), col(3), tab, tab, tab,
            pl.BlockSpec((1, RET_WIDTH), lambda b, c: (0, 0)),
        ],
        out_specs=[
            pl.BlockSpec((None, L, RET_WIDTH), lambda b, c: (b, c, 0)),
            pl.BlockSpec((None, RET_HEADS, RET_DK, RET_DV), lambda b, c: (b, 0, 0, 0)),
        ],
        out_shape=[
            jax.ShapeDtypeStruct((B, S, RET_WIDTH), BF16),
            jax.ShapeDtypeStruct((B, RET_HEADS, RET_DK, RET_DV), F32),
        ],
        compiler_params=_params(("arbitrary", "arbitrary"), 32),
        name="ret_prompt",
    )(gl, proj3, proj3, proj3, proj3, dm, qd, kd, gn_ret.reshape(1, RET_WIDTH))


def _pad_rows(x, rows):
    x = x.astype(F32)
    return jnp.concatenate([x, jnp.zeros((rows - x.shape[0], x.shape[1]), F32)], axis=0)


def _ret_sample_kernel(gl_ref, q_ref, k_ref, v_ref, rg_ref, st_in_ref, dm_ref, qd_ref, kd_ref, gn_ref,
                       y_ref, st_ref):
    for h in range(RET_HEADS):
        sl = slice(h * RET_DK, (h + 1) * RET_DK)
        k = _pad_rows(k_ref[:, sl], RET_CHUNK).astype(BF16)
        v = _pad_rows(v_ref[:, sl], RET_CHUNK).astype(BF16)
        y, new_st = _ret_head(q_ref[:, sl].astype(BF16), k, v, st_in_ref[h],
                              dm_ref[h], qd_ref[h], kd_ref[h], gl_ref[h])
        st_ref[h] = new_st
        y_ref[:, sl] = _groupnorm_gate(y, gn_ref[:, sl], rg_ref[:, sl].astype(F32)).astype(y_ref.dtype)


def _ret_sample(proj_s, state, gn_ret, T):
    nb = state.shape[0]
    dm, qd, kd, gl = _ret_tables(T)
    cb = RET_COL0 // RET_WIDTH
    col = lambda k: pl.BlockSpec((T, RET_WIDTH), lambda b, k=k: (b, cb + k))
    st_spec = pl.BlockSpec((None, RET_HEADS, RET_DK, RET_DV), lambda b: (b, 0, 0, 0))
    return pl.pallas_call(
        _ret_sample_kernel,
        grid=(nb,),
        in_specs=[
            pl.BlockSpec(memory_space=pltpu.SMEM),
            col(0), col(1), col(2), col(3), st_spec,
            pl.BlockSpec((RET_HEADS, T, RET_CHUNK), lambda b: (0, 0, 0)),
            pl.BlockSpec((RET_HEADS, T, RET_DV), lambda b: (0, 0, 0)),
            pl.BlockSpec((RET_HEADS, RET_CHUNK, RET_DK), lambda b: (0, 0, 0)),
            pl.BlockSpec((1, RET_WIDTH), lambda b: (0, 0)),
        ],
        out_specs=[pl.BlockSpec((T, RET_WIDTH), lambda b: (b, 0)), st_spec],
        out_shape=[
            jax.ShapeDtypeStruct((nb * T, RET_WIDTH), F32),
            jax.ShapeDtypeStruct(state.shape, F32),
        ],
        compiler_params=_params(("arbitrary",), 32),
        name="ret_sample",
    )(gl, proj_s, proj_s, proj_s, proj_s, state, dm, qd, kd, gn_ret.reshape(1, RET_WIDTH))


def _alibi_slope(g, h):
    n = N_GROUPS * ATT_HG
    return 2.0 ** (-ALIBI_MAX * (g * ATT_HG + h + 1) / n)


def _softmax_pv(scores, values):
    m = scores[0].max(axis=-1, keepdims=True)
    for s in scores[1:]:
        m = jnp.maximum(m, s.max(axis=-1, keepdims=True))
    l = None
    acc = None
    for s, v in zip(scores, values):
        p = jnp.exp(s - m)
        ls = p.sum(axis=-1, keepdims=True)
        a = _dot(p.astype(BF16), v)
        l = ls if l is None else l + ls
        acc = a if acc is None else acc + a
    return acc / l, m + jnp.log(l)


def _att_prompt_kernel(q_ref, kp_ref, kc_ref, vp_ref, vc_ref, o_ref, lse_ref, *, group, dil):
    j = pl.program_id(2)
    blk = ATT_BLK
    r = lax.broadcasted_iota(jnp.int32, (blk, blk), 0)
    c = lax.broadcasted_iota(jnp.int32, (blk, blk), 1)
    dist_c = r - c
    dist_p = dist_c + blk
    valid_c = dist_c >= 0
    valid_p = (dist_p <= ATT_SPAN) & (j > 0)
    scale = ATT_HD ** -0.5
    for h in range(ATT_HG):
        sl = slice(h * ATT_HD, (h + 1) * ATT_HD)
        slope = _alibi_slope(group, h) * dil
        q = q_ref[:, sl]
        sp = _dot_nt(q, kp_ref[:, sl]) * scale - slope * dist_p.astype(F32)
        sc = _dot_nt(q, kc_ref[:, sl]) * scale - slope * dist_c.astype(F32)
        sp = jnp.where(valid_p, sp, NEG)
        sc = jnp.where(valid_c, sc, NEG)
        o, lse = _softmax_pv([sp, sc], [vp_ref[:, sl], vc_ref[:, sl]])
        o_ref[:, sl] = o.astype(o_ref.dtype)
        lse_ref[:, sl] = jnp.broadcast_to(lse, (blk, ATT_HD))


def _att_prompt(proj3, group):
    B, S, C = proj3.shape
    _, dil = ATT_GROUPS[group]
    L = S // dil
    nb = L // ATT_BLK
    view = proj3.reshape(B, L, dil * C)
    cpr = C // ATT_GW
    c0 = ATT_COL0 // ATT_GW + 3 * group

    def spec(which, prev):
        def imap(b, r, j):
            jj = jnp.maximum(j - 1, 0) if prev else j
            return (b, jj, r * cpr + c0 + which)
        return pl.BlockSpec((None, ATT_BLK, ATT_GW), imap)

    out_spec = pl.BlockSpec((None, ATT_BLK, ATT_GW), lambda b, r, j: (b, j, r))
    o, lse = pl.pallas_call(
        functools.partial(_att_prompt_kernel, group=group, dil=dil),
        grid=(B, dil, nb),
        in_specs=[spec(0, False), spec(1, True), spec(1, False), spec(2, True), spec(2, False)],
        out_specs=[out_spec, out_spec],
        out_shape=[
            jax.ShapeDtypeStruct((B, L, dil * ATT_GW), BF16),
            jax.ShapeDtypeStruct((B, L, dil * ATT_GW), F32),
        ],
        compiler_params=_params(("arbitrary", "arbitrary", "arbitrary"), 32),
        name=f"att_prompt_g{group}",
    )(view, view, view, view, view)
    return o.reshape(B * S, ATT_GW), lse.reshape(B * S, ATT_GW)


def _att_sample_kernel(q_ref, k_ref, v_ref, c_ref, o_ref, lse_ref, *, group, dil, wb, T):
    t = lax.broadcasted_iota(jnp.int32, (T, wb), 0)
    c = lax.broadcasted_iota(jnp.int32, (T, wb), 1)
    dist_c = wb + t - c
    valid_c = ((dist_c & (dil - 1)) == 0) & (dist_c <= ATT_SPAN * dil)
    tn = lax.broadcasted_iota(jnp.int32, (T, ATT_BLK), 0)
    cn = lax.broadcasted_iota(jnp.int32, (T, ATT_BLK), 1)
    dist_n = tn - cn
    valid_n = (dist_n >= 0) & ((dist_n & (dil - 1)) == 0) & (cn < T)
    scale = ATT_HD ** -0.5
    for h in range(ATT_HG):
        sl = slice(h * ATT_HD, (h + 1) * ATT_HD)
        slv = slice(ATT_GW + h * ATT_HD, ATT_GW + (h + 1) * ATT_HD)
        slope = _alibi_slope(group, h)
        q = q_ref[:, sl].astype(BF16)
        kc = c_ref[:, sl].astype(BF16)
        vc = c_ref[:, slv].astype(BF16)
        kn = _pad_rows(k_ref[:, sl], ATT_BLK).astype(BF16)
        vn = _pad_rows(v_ref[:, sl], ATT_BLK).astype(BF16)
        sc = _dot_nt(q, kc) * scale - slope * dist_c.astype(F32)
        sn = _dot_nt(q, kn) * scale - slope * dist_n.astype(F32)
        sc = jnp.where(valid_c, sc, NEG)
        sn = jnp.where(valid_n, sn, NEG)
        o, lse = _softmax_pv([sc, sn], [vc, vn])
        o_ref[:, sl] = o.astype(o_ref.dtype)
        lse_ref[:, sl] = jnp.broadcast_to(lse, (T, ATT_HD))


def _att_sample(proj_s, cache, group, T):
    nb, wb = cache.shape[0], cache.shape[1]
    _, dil = ATT_GROUPS[group]
    c0 = ATT_COL0 // ATT_GW + 3 * group
    col = lambda k: pl.BlockSpec((T, ATT_GW), lambda b, k=k: (b, c0 + k))
    out_spec = pl.BlockSpec((T, ATT_GW), lambda b: (b, 0))
    return pl.pallas_call(
        functools.partial(_att_sample_kernel, group=group, dil=dil, wb=wb, T=T),
        grid=(nb,),
        in_specs=[col(0), col(1), col(2), pl.BlockSpec((None, wb, 2 * ATT_GW), lambda b: (b, 0, 0))],
        out_specs=[out_spec, out_spec],
        out_shape=[
            jax.ShapeDtypeStruct((nb * T, ATT_GW), F32),
            jax.ShapeDtypeStruct((nb * T, ATT_GW), F32),
        ],
        compiler_params=_params(("arbitrary",), 40),
        name=f"att_sample_g{group}",
    )(proj_s, proj_s, proj_s, cache.reshape(nb, wb, 2 * ATT_GW))


SHIFT_DMAS = 8


def _cache_shift_kernel(cache_ref, new_ref, out_ref, sems, *, T, wb, col0, nsplit, per):
    copies = []
    for i in range(nsplit):
        bs = pl.ds(i * per, per)
        copies.append(pltpu.make_async_copy(
            cache_ref.at[bs, pl.ds(T, wb - T), :], out_ref.at[bs, pl.ds(0, wb - T), :], sems.at[0, i]))
        copies.append(pltpu.make_async_copy(
            new_ref.at[bs, :, pl.ds(col0, 2 * ATT_GW)], out_ref.at[bs, pl.ds(wb - T, T), :], sems.at[1, i]))
    for cp in copies:
        cp.start()
    for cp in copies:
        cp.wait()


def _cache_shift(cache, proj_s, group, T):
    nb, wb = cache.shape[0], cache.shape[1]
    assert wb > T
    nsplit = SHIFT_DMAS if nb % SHIFT_DMAS == 0 else 1
    col0 = ATT_COL0 + 3 * ATT_GW * group + ATT_GW
    out = pl.pallas_call(
        functools.partial(_cache_shift_kernel, T=T, wb=wb, col0=col0, nsplit=nsplit, per=nb // nsplit),
        in_specs=[pl.BlockSpec(memory_space=pl.ANY), pl.BlockSpec(memory_space=pl.ANY)],
        out_specs=pl.BlockSpec(memory_space=pl.ANY),
        out_shape=jax.ShapeDtypeStruct((nb, wb, 2 * ATT_GW), F32),
        scratch_shapes=[pltpu.SemaphoreType.DMA((2, nsplit))],
        name=f"cache_shift_g{group}",
    )(cache.reshape(nb, wb, 2 * ATT_GW), proj_s.reshape(nb, T, proj_s.shape[-1]))
    return out.reshape(cache.shape)


def _epilogue_kernel(x_ref, gr_ref, ga_ref, yret_ref, o0_ref, o1_ref, o2_ref, l0_ref, l1_ref, l2_ref,
                     wr_ref, wa_ref, wo_ref, gf_ref, x1_ref, h2_ref):
    l0, l1, l2 = l0_ref[...], l1_ref[...], l2_ref[...]
    m = jnp.maximum(jnp.maximum(l0, l1), l2)
    e0, e1, e2 = jnp.exp(l0 - m), jnp.exp(l1 - m), jnp.exp(l2 - m)
    inv = 1.0 / (e0 + e1 + e2)
    o = ((e0 * inv) * o0_ref[...].astype(F32) + (e1 * inv) * o1_ref[...].astype(F32)
         + (e2 * inv) * o2_ref[...].astype(F32))
    att = _dot(o.astype(BF16), wa_ref[...])
    ret = _dot(yret_ref[...].astype(BF16), wr_ref[...])
    merged = (jax.nn.sigmoid(gr_ref[...].astype(F32)) * ret + jax.nn.sigmoid(ga_ref[...].astype(F32)) * att)
    x1 = x_ref[...] + _dot(merged.astype(BF16), wo_ref[...])
    x1_ref[...] = x1
    h2_ref[...] = _rmsnorm_rows(x1, gf_ref[...]).astype(BF16)


def _epilogue(x2d, proj2, yret, os_, lses, w_ret_o, w_att_o, w_o, g_ffn, *, tm):
    M = x2d.shape[0]
    row = lambda w: pl.BlockSpec((tm, w), lambda m: (m, 0))
    const = lambda a: pl.BlockSpec(a.shape, lambda m: (0, 0), pipeline_mode=pl.Buffered(1))
    gf = g_ffn.reshape(1, D_MODEL)
    return pl.pallas_call(
        _epilogue_kernel,
        grid=(M // tm,),
        in_specs=[
            row(D_MODEL),
            pl.BlockSpec((tm, D_MODEL), lambda m: (m, 0)),
            pl.BlockSpec((tm, D_MODEL), lambda m: (m, 1)),
            row(RET_WIDTH), row(ATT_GW), row(ATT_GW), row(ATT_GW), row(ATT_GW), row(ATT_GW), row(ATT_GW),
            const(w_ret_o), const(w_att_o), const(w_o), const(gf),
        ],
        out_specs=[row(D_MODEL), row(D_MODEL)],
        out_shape=[jax.ShapeDtypeStruct((M, D_MODEL), F32), jax.ShapeDtypeStruct((M, D_MODEL), BF16)],
        compiler_params=_params(("arbitrary",), 56),
        name="epilogue",
    )(x2d, proj2, proj2, yret, *os_, *lses, w_ret_o, w_att_o, w_o, gf)


def _up_kernel(h_ref, w_ref, u_ref, tail_ref):
    acc = _dot(h_ref[...], w_ref[...])
    u_ref[...] = acc.astype(u_ref.dtype)
    tail_ref[...] = acc[acc.shape[0] - 8:, :]


def _up_proj(h2, w_up, *, tm):
    M = h2.shape[0]
    N = w_up.shape[1]
    tn = 1024
    return pl.pallas_call(
        _up_kernel,
        grid=(M // tm, N // tn),
        in_specs=[
            pl.BlockSpec((tm, D_MODEL), lambda m, n: (m, 0)),
            pl.BlockSpec((D_MODEL, tn), lambda m, n: (0, n)),
        ],
        out_specs=[
            pl.BlockSpec((tm, tn), lambda m, n: (m, n)),
            pl.BlockSpec((None, 8, tn), lambda m, n: (m, 0, n)),
        ],
        out_shape=[jax.ShapeDtypeStruct((M, N), BF16), jax.ShapeDtypeStruct((M // tm, 8, N), F32)],
        compiler_params=_params(("arbitrary", "arbitrary"), 48),
        name="up_proj",
    )(h2, w_up)


def _gelu(a):
    return 0.5 * a * (1.0 + lax.erf(a * math.sqrt(0.5)))


def _conv3(u, r1, r2, w_ref, b_ref):
    return b_ref[...] + (r2 * w_ref[0:1, :] + r1 * w_ref[1:2, :] + u * w_ref[2:3, :])


def _shifted_rows(u, prev8):
    ext = jnp.concatenate([prev8, u], axis=0)
    r1 = pltpu.roll(ext, 1, axis=0)[8:]
    r2 = pltpu.roll(ext, 2, axis=0)[8:]
    return r1, r2


def _down_prompt_kernel(ua_ref, ub_ref, pa_ref, pb_ref, cwa_ref, cwb_ref, cba_ref, cbb_ref, wd_ref,
                        x1_ref, gfin_ref, y_ref, *, tiles_per_seq, final_norm):
    m = pl.program_id(0)
    f = pl.program_id(1)
    seq_start = (m % tiles_per_seq) == 0

    def conv(u_ref, p_ref, w_ref, b_ref):
        u = u_ref[...].astype(F32)
        prev = jnp.where(seq_start, 0.0, p_ref[...].astype(F32))
        r1, r2 = _shifted_rows(u, prev)
        return _conv3(u, r1, r2, w_ref, b_ref)

    a = conv(ua_ref, pa_ref, cwa_ref, cba_ref)
    b = conv(ub_ref, pb_ref, cwb_ref, cbb_ref)
    part = _dot((_gelu(a) * b).astype(BF16), wd_ref[...])

    @pl.when(f == 0)
    def _():
        y_ref[...] = x1_ref[...] + part

    @pl.when(f > 0)
    def _():
        y_ref[...] += part

    if final_norm:
        @pl.when(f == pl.num_programs(1) - 1)
        def _():
            y_ref[...] = _rmsnorm_rows(y_ref[...], gfin_ref[...])


def _down_prompt(u, x1, conv_w, conv_b, w_down, g_final, *, tm, tiles_per_seq, final_norm):
    M = u.shape[0]
    tf = 512
    nf = D_FF // tf
    t8 = tm // 8
    cb = conv_b.reshape(1, 2 * D_FF)
    gfin = g_final.reshape(1, D_MODEL)
    prev = lambda off: pl.BlockSpec((8, tf), lambda m, f: (jnp.maximum(m * t8 - 1, 0), f + off))
    return pl.pallas_call(
        functools.partial(_down_prompt_kernel, tiles_per_seq=tiles_per_seq, final_norm=final_norm),
        grid=(M // tm, nf),
        in_specs=[
            pl.BlockSpec((tm, tf), lambda m, f: (m, f)),
            pl.BlockSpec((tm, tf), lambda m, f: (m, f + nf)),
            prev(0), prev(nf),
            pl.BlockSpec((CONV_W, tf), lambda m, f: (0, f)),
            pl.BlockSpec((CONV_W, tf), lambda m, f: (0, f + nf)),
            pl.BlockSpec((1, tf), lambda m, f: (0, f)),
            pl.BlockSpec((1, tf), lambda m, f: (0, f + nf)),
            pl.BlockSpec((tf, D_MODEL), lambda m, f: (f, 0)),
            pl.BlockSpec((tm, D_MODEL), lambda m, f: (m, 0)),
            pl.BlockSpec((1, D_MODEL), lambda m, f: (0, 0)),
        ],
        out_specs=pl.BlockSpec((tm, D_MODEL), lambda m, f: (m, 0)),
        out_shape=jax.ShapeDtypeStruct((M, D_MODEL), F32),
        compiler_params=_params(("arbitrary", "arbitrary"), 48),
        name="down_prompt",
    )(u, u, u, u, conv_w, conv_w, cb, cb, w_down, x1, gfin)


def _ffn_sample_kernel(h_ref, wua_ref, wub_ref, h1a_ref, h1b_ref, h2a_ref, h2b_ref, cwa_ref, cwb_ref,
                       cba_ref, cbb_ref, wd_ref, x1_ref, gfin_ref, y_ref, ua_ref, ub_ref, *, T, final_norm):
    f = pl.program_id(0)
    h = h_ref[...]
    t = lax.broadcasted_iota(jnp.int32, ua_ref.shape, 0) & (T - 1)

    def conv(wu_ref, hal1_ref, hal2_ref, w_ref, b_ref, u_out_ref):
        u = _dot(h, wu_ref[...])
        u_out_ref[...] = u
        r1 = jnp.where(t >= 1, pltpu.roll(u, 1, axis=0), hal1_ref[...])
        r2 = jnp.where(t >= 2, pltpu.roll(u, 2, axis=0), hal2_ref[...])
        return _conv3(u, r1, r2, w_ref, b_ref)

    a = conv(wua_ref, h1a_ref, h2a_ref, cwa_ref, cba_ref, ua_ref)
    b = conv(wub_ref, h1b_ref, h2b_ref, cwb_ref, cbb_ref, ub_ref)
    part = _dot((_gelu(a) * b).astype(BF16), wd_ref[...])

    @pl.when(f == 0)
    def _():
        y_ref[...] = x1_ref[...] + part

    @pl.when(f > 0)
    def _():
        y_ref[...] += part

    if final_norm:
        @pl.when(f == pl.num_programs(0) - 1)
        def _():
            y_ref[...] = _rmsnorm_rows(y_ref[...], gfin_ref[...])


def _ffn_sample(h2, x1, state_conv, w_up, conv_w, conv_b, w_down, g_final, T, final_norm):
    M = h2.shape[0]
    nb = M // T
    tf = 512
    nf = D_FF // tf
    F2 = 2 * D_FF
    sc = state_conv.astype(F32)
    hal1 = jnp.concatenate([sc[:, 1:2], jnp.zeros((nb, T - 1, F2), F32)], axis=1).reshape(M, F2)
    hal2 = jnp.concatenate([sc[:, 0:2], jnp.zeros((nb, T - 2, F2), F32)], axis=1).reshape(M, F2)
    cb = conv_b.reshape(1, F2)
    gfin = g_final.reshape(1, D_MODEL)
    full = lambda w: pl.BlockSpec((M, w), lambda f: (0, 0))
    ca = lambda r, w=tf: pl.BlockSpec((r, w), lambda f: (0, f))
    cbk = lambda r, w=tf: pl.BlockSpec((r, w), lambda f: (0, f + nf))
    y, ua, ub = pl.pallas_call(
        functools.partial(_ffn_sample_kernel, T=T, final_norm=final_norm),
        grid=(nf,),
        in_specs=[
            full(D_MODEL), ca(D_MODEL), cbk(D_MODEL), ca(M), cbk(M), ca(M), cbk(M),
            ca(CONV_W), cbk(CONV_W), ca(1), cbk(1),
            pl.BlockSpec((tf, D_MODEL), lambda f: (f, 0)),
            full(D_MODEL), pl.BlockSpec((1, D_MODEL), lambda f: (0, 0)),
        ],
        out_specs=[full(D_MODEL), ca(M), ca(M)],
        out_shape=[
            jax.ShapeDtypeStruct((M, D_MODEL), F32),
            jax.ShapeDtypeStruct((M, D_FF), F32),
            jax.ShapeDtypeStruct((M, D_FF), F32),
        ],
        compiler_params=_params(("arbitrary",), 48),
        name="ffn_sample",
    )(h2, w_up, w_up, hal1, hal1, hal2, hal2, conv_w, conv_w, cb, cb, w_down, x1, gfin)
    u = jnp.concatenate([ua, ub], axis=-1).reshape(nb, T, F2)
    return y, u[:, T - (CONV_W - 1):]


def _permute_w_in(w):
    split = 4 * RET_WIDTH + 3 * N_GROUPS * ATT_GW
    return jnp.concatenate([w[:, split:], w[:, :split]], axis=1).astype(BF16)


def kernel(x_prompt, x_sample, state_ret, cache_kv_w128, cache_kv_w512, cache_kv_w2048, state_conv, g_mix, w_in,
           gn_ret, w_ret_o, w_att_o, w_o, g_ffn, w_up, conv_w, conv_b, w_down, g_final):
    B, S, _ = x_prompt.shape
    NB, T, _ = x_sample.shape
    depth = w_in.shape[0]
    caches = (cache_kv_w128, cache_kv_w512, cache_kv_w2048)
    F2 = 2 * D_FF
    TM = 1024
    tiles_per_seq = S // TM

    xp = x_prompt.reshape(B * S, D_MODEL)
    xs = x_sample.reshape(NB * T, D_MODEL)
    ret_p, ret_s, conv_p, conv_s = [], [], [], []
    kv_p = [[] for _ in range(N_GROUPS)]
    kv_s = [[] for _ in range(N_GROUPS)]
    for l in range(depth):
        w_in_l = _permute_w_in(w_in[l])
        w_ret_o_l = w_ret_o[l].astype(BF16)
        w_att_o_l = w_att_o[l].astype(BF16)
        w_o_l = w_o[l].astype(BF16)
        w_up_l = w_up[l].astype(BF16)
        w_down_l = w_down[l].astype(BF16)

        proj, kvf = _inproj(xp, g_mix[l], w_in_l, tm=TM, out_dtype=BF16, want_kv=True)
        proj3 = proj.reshape(B, S, IN_COLS)
        yret, st = _ret_prompt(proj3, gn_ret[l])
        ret_p.append(st)
        os_, lses = [], []
        for g in range(N_GROUPS):
            o, lse = _att_prompt(proj3, g)
            os_.append(o)
            lses.append(lse)
        kvf3 = kvf.reshape(B, S, 2 * N_GROUPS * ATT_GW)
        for g, (win, _) in enumerate(ATT_GROUPS):
            w = min(win, S)
            kv_p[g].append(kvf3[:, S - w:, 2 * ATT_GW * g:2 * ATT_GW * (g + 1)].reshape(B, w, 2, ATT_HG, ATT_HD))
        x1, h2 = _epilogue(xp, proj, yret.reshape(B * S, RET_WIDTH), os_, lses,
                           w_ret_o_l, w_att_o_l, w_o_l, g_ffn[l], tm=256)
        u, utail = _up_proj(h2, w_up_l, tm=TM)
        conv_p.append(utail[tiles_per_seq - 1::tiles_per_seq, 8 - (CONV_W - 1):])
        xp = _down_prompt(u, x1, conv_w[l], conv_b[l], w_down_l, g_final,
                          tm=512, tiles_per_seq=S // 512, final_norm=l == depth - 1)

        proj_s, = _inproj(xs, g_mix[l], w_in_l, tm=NB * T, out_dtype=F32, want_kv=False)
        yret_s, st_s = _ret_sample(proj_s, state_ret[l], gn_ret[l], T)
        ret_s.append(st_s)
        os_, lses = [], []
        for g in range(N_GROUPS):
            o, lse = _att_sample(proj_s, caches[g][l], g, T)
            os_.append(o)
            lses.append(lse)
            kv_s[g].append(_cache_shift(caches[g][l], proj_s, g, T))
        x1s, h2s = _epilogue(xs, proj_s, yret_s, os_, lses, w_ret_o_l, w_att_o_l, w_o_l, g_ffn[l], tm=NB * T)
        xs, cv = _ffn_sample(h2s, x1s, state_conv[l], w_up_l, conv_w[l], conv_b[l], w_down_l,
                             g_final, T, final_norm=l == depth - 1)
        conv_s.append(cv)

    return (xp.reshape(B, S, D_MODEL), xs.reshape(NB, T, D_MODEL),
            jnp.stack(ret_p), jnp.stack(ret_s),
            jnp.stack(kv_p[0]), jnp.stack(kv_s[0]),
            jnp.stack(kv_p[1]), jnp.stack(kv_s[1]),
            jnp.stack(kv_p[2]), jnp.stack(kv_s[2]),
            jnp.stack(conv_p), jnp.stack(conv_s))
```

```python
import functools
import math

import jax
import jax.numpy as jnp
from jax import lax
from jax.experimental import pallas as pl
from jax.experimental.pallas import tpu as pltpu

D_MODEL = 2048
RET_HEADS = 8
RET_DK = 128
RET_DV = 128
RET_WIDTH = RET_HEADS * RET_DV
RET_CHUNK = 128
ATT_GROUPS = ((128, 1), (512, 4), (2048, 16))
N_GROUPS = 3
ATT_HG = 4
ATT_HD = 128
ATT_GW = ATT_HG * ATT_HD
ATT_SPAN = 128
ATT_BLK = 128
ALIBI_MAX = 8.0
D_FF = 5632
CONV_W = 3
EPS = 1e-6
NEG = -1e30
IN_COLS = 4 * RET_WIDTH + 3 * N_GROUPS * ATT_GW + 2 * D_MODEL

GATE_COLS = 2 * D_MODEL
RET_COL0 = GATE_COLS
ATT_COL0 = GATE_COLS + 4 * RET_WIDTH
PROJ_TN = 512
ATT_TILE0 = ATT_COL0 // PROJ_TN

LANES = 128
MIB = 1024 * 1024
BF16 = jnp.bfloat16
F32 = jnp.float32


def _params(semantics, vmem_mib):
    return pltpu.CompilerParams(dimension_semantics=semantics, vmem_limit_bytes=vmem_mib * MIB)


def _dot(a, b):
    return jnp.dot(a, b, preferred_element_type=F32)


def _dot_nt(a, b):
    return lax.dot_general(a, b, (((1,), (1,)), ((), ())), preferred_element_type=F32)


def _rmsnorm_rows(x, g):
    ms = jnp.mean(x * x, axis=-1, keepdims=True)
    return x * lax.rsqrt(ms + EPS) * g


def _kv_slot(n):
    t = ATT_TILE0
    return ((n >= t + 2).astype(jnp.int32) + (n >= t + 3) + (n >= t + 5) + (n >= t + 6) + (n >= t + 8))


def _inproj_sample_kernel(x_ref, g_ref, w_ref, proj_ref, h_scr):
    @pl.when(pl.program_id(1) == 0)
    def _():
        h_scr[...] = _rmsnorm_rows(x_ref[...], g_ref[...]).astype(BF16)

    proj_ref[...] = _dot(h_scr[...], w_ref[...])


def _inproj_prompt_kernel(x_ref, g_ref, w_ref, main_ref, a0_ref, a1_ref, a2_ref, kv_ref, h_scr, acc_scr):
    n = pl.program_id(1)

    @pl.when(n == 0)
    def _():
        h_scr[...] = _rmsnorm_rows(x_ref[...], g_ref[...]).astype(BF16)

    acc = _dot(h_scr[...], w_ref[...])

    @pl.when(n < ATT_TILE0)
    def _():
        main_ref[...] = acc.astype(BF16)

    for grp, a_ref in enumerate((a0_ref, a1_ref, a2_ref)):
        dil = ATT_GROUPS[grp][1]
        lo = ATT_TILE0 + 3 * grp

        @pl.when((n >= lo) & (n < lo + 3))
        def _(a_ref=a_ref, dil=dil):
            if dil == 1:
                a_ref[0] = acc.astype(BF16)
            else:
                rows = acc_scr.shape[1] // dil
                for cb in range(acc_scr.shape[0]):
                    lanes = slice(cb * LANES, (cb + 1) * LANES)
                    acc_scr[cb] = acc[:, lanes]
                    for r in range(dil):
                        a_ref[r, :, lanes] = acc_scr[cb, pl.ds(r, rows, stride=dil), :].astype(BF16)

    rel = n - ATT_TILE0
    is_kv = (rel == 1) | (rel == 2) | (rel == 4) | (rel == 5) | (rel == 7) | (rel == 8)

    @pl.when(is_kv)
    def _():
        kv_ref[...] = acc


def _inproj_sample(x2d, g, w_bf16):
    M = x2d.shape[0]
    N = w_bf16.shape[1]
    tn = PROJ_TN
    return pl.pallas_call(
        _inproj_sample_kernel,
        grid=(1, N // tn),
        in_specs=[
            pl.BlockSpec((M, D_MODEL), lambda m, n: (m, 0)),
            pl.BlockSpec((1, D_MODEL), lambda m, n: (0, 0)),
            pl.BlockSpec((D_MODEL, tn), lambda m, n: (0, n)),
        ],
        out_specs=pl.BlockSpec((M, tn), lambda m, n: (m, n)),
        out_shape=jax.ShapeDtypeStruct((M, N), F32),
        scratch_shapes=[pltpu.VMEM((M, D_MODEL), BF16)],
        compiler_params=_params(("arbitrary", "arbitrary"), 32),
        name="inproj_sample",
    )(x2d, g.reshape(1, D_MODEL), w_bf16)


def _inproj_prompt(x2d, g, w_bf16, *, B, S, tm):
    M = x2d.shape[0]
    N = w_bf16.shape[1]
    tn = PROJ_TN
    tps = S // tm
    out_shape = [jax.ShapeDtypeStruct((M, ATT_COL0), BF16)]
    out_specs = [pl.BlockSpec((tm, tn), lambda m, n: (m, jnp.minimum(n, ATT_TILE0 - 1)))]
    for grp, (_, dil) in enumerate(ATT_GROUPS):
        lo = ATT_TILE0 + 3 * grp
        out_shape.append(jax.ShapeDtypeStruct((B, dil, S // dil, 3 * ATT_GW), BF16))
        out_specs.append(pl.BlockSpec(
            (None, dil, tm // dil, tn), lambda m, n, lo=lo: (m // tps, 0, m % tps, jnp.clip(n - lo, 0, 2))))
    out_shape.append(jax.ShapeDtypeStruct((M, 2 * N_GROUPS * ATT_GW), F32))
    out_specs.append(pl.BlockSpec((tm, tn), lambda m, n: (m, _kv_slot(n))))
    return pl.pallas_call(
        _inproj_prompt_kernel,
        grid=(M // tm, N // tn),
        in_specs=[
            pl.BlockSpec((tm, D_MODEL), lambda m, n: (m, 0)),
            pl.BlockSpec((1, D_MODEL), lambda m, n: (0, 0)),
            pl.BlockSpec((D_MODEL, tn), lambda m, n: (0, n)),
        ],
        out_specs=out_specs,
        out_shape=out_shape,
        scratch_shapes=[pltpu.VMEM((tm, D_MODEL), BF16), pltpu.VMEM((tn // LANES, tm, LANES), F32)],
        compiler_params=_params(("arbitrary", "arbitrary"), 48),
        name="inproj_prompt",
    )(x2d, g.reshape(1, D_MODEL), w_bf16)


def _ret_tables(lq):
    lg = jnp.log(1.0 - jnp.exp2(-5.0 - jnp.arange(RET_HEADS, dtype=F32)))
    scale = RET_DK ** -0.5
    i = jnp.arange(lq, dtype=F32)
    j = jnp.arange(RET_CHUNK, dtype=F32)
    diff = i[:, None] - j[None, :]
    live = (diff >= 0) & (j[None, :] < lq)
    dm = jnp.where(live[None], jnp.exp(jnp.maximum(diff, 0.0)[None] * lg[:, None, None]), 0.0) * scale
    qd = jnp.exp((i[None, :] + 1.0) * lg[:, None])
    qd = jnp.broadcast_to(qd[:, :, None], (RET_HEADS, lq, RET_DV))
    kd = jnp.where(j[None, :] < lq, jnp.exp(jnp.maximum(lq - 1.0 - j, 0.0)[None, :] * lg[:, None]), 0.0) * scale
    kd = jnp.broadcast_to(kd[:, :, None], (RET_HEADS, RET_CHUNK, RET_DK))
    gl = jnp.exp(lq * lg)
    return dm.astype(F32), qd.astype(F32), kd.astype(F32), gl.astype(F32)


def _ret_head(q, k, v, st, dm, qd, kd, gl):
    s = _dot_nt(q, k) * dm
    inner = _dot(s.astype(BF16), v)
    cross = _dot(q, st.astype(BF16)) * qd
    kt = (k.astype(F32) * kd).T.astype(BF16)
    new_st = st * gl + _dot(kt, v)
    return inner + cross, new_st


def _groupnorm_gate(y, gn, rg):
    mu = jnp.mean(y, axis=-1, keepdims=True)
    yc = y - mu
    yn = yc * lax.rsqrt(jnp.mean(yc * yc, axis=-1, keepdims=True) + EPS)
    return (rg * jax.nn.sigmoid(rg)) * (yn * gn)


def _ret_prompt_kernel(gl_ref, q_ref, k_ref, v_ref, rg_ref, dm_ref, qd_ref, kd_ref, gn_ref, y_ref, st_ref):
    c = pl.program_id(1)

    @pl.when(c == 0)
    def _():
        st_ref[...] = jnp.zeros_like(st_ref)

    for h in range(RET_HEADS):
        sl = slice(h * RET_DK, (h + 1) * RET_DK)
        y, new_st = _ret_head(q_ref[:, sl], k_ref[:, sl], v_ref[:, sl], st_ref[h],
                              dm_ref[h], qd_ref[h], kd_ref[h], gl_ref[h])
        st_ref[h] = new_st
        y_ref[:, sl] = _groupnorm_gate(y, gn_ref[:, sl], rg_ref[:, sl].astype(F32)).astype(y_ref.dtype)


def _ret_prompt(proj3, gn_ret):
    B, S, _ = proj3.shape
    L = RET_CHUNK
    dm, qd, kd, gl = _ret_tables(L)
    cb = RET_COL0 // RET_WIDTH
    col = lambda k: pl.BlockSpec((None, L, RET_WIDTH), lambda b, c, k=k: (b, c, cb + k))
    tab = pl.BlockSpec((RET_HEADS, L, RET_DK), lambda b, c: (0, 0, 0))
    return pl.pallas_call(
        _ret_prompt_kernel,
        grid=(B, S // L),
        in_specs=[
            pl.BlockSpec(memory_space=pltpu.SMEM),
            col(0), col(1), col(2), col(3), tab, tab, tab,
            pl.BlockSpec((1, RET_WIDTH), lambda b, c: (0, 0)),
        ],
        out_specs=[
            pl.BlockSpec((None, L, RET_WIDTH), lambda b, c: (b, c, 0)),
            pl.BlockSpec((None, RET_HEADS, RET_DK, RET_DV), lambda b, c: (b, 0, 0, 0)),
        ],
        out_shape=[
            jax.ShapeDtypeStruct((B, S, RET_WIDTH), BF16),
            jax.ShapeDtypeStruct((B, RET_HEADS, RET_DK, RET_DV), F32),
        ],
        compiler_params=_params(("arbitrary", "arbitrary"), 32),
        name="ret_prompt",
    )(gl, proj3, proj3, proj3, proj3, dm, qd, kd, gn_ret.reshape(1, RET_WIDTH))


def _pad_rows(x, rows):
    x = x.astype(F32)
    return jnp.concatenate([x, jnp.zeros((rows - x.shape[0], x.shape[1]), F32)], axis=0)


def _ret_sample_kernel(gl_ref, q_ref, k_ref, v_ref, rg_ref, st_in_ref, dm_ref, qd_ref, kd_ref, gn_ref,
                       y_ref, st_ref):
    for h in range(RET_HEADS):
        sl = slice(h * RET_DK, (h + 1) * RET_DK)
        k = _pad_rows(k_ref[:, sl], RET_CHUNK).astype(BF16)
        v = _pad_rows(v_ref[:, sl], RET_CHUNK).astype(BF16)
        y, new_st = _ret_head(q_ref[:, sl].astype(BF16), k, v, st_in_ref[h],
                              dm_ref[h], qd_ref[h], kd_ref[h], gl_ref[h])
        st_ref[h] = new_st
        y_ref[:, sl] = _groupnorm_gate(y, gn_ref[:, sl], rg_ref[:, sl].astype(F32)).astype(y_ref.dtype)


def _ret_sample(proj_s, state, gn_ret, T):
    nb = state.shape[0]
    dm, qd, kd, gl = _ret_tables(T)
    cb = RET_COL0 // RET_WIDTH
    col = lambda k: pl.BlockSpec((T, RET_WIDTH), lambda b, k=k: (b, cb + k))
    st_spec = pl.BlockSpec((None, RET_HEADS, RET_DK, RET_DV), lambda b: (b, 0, 0, 0))
    return pl.pallas_call(
        _ret_sample_kernel,
        grid=(nb,),
        in_specs=[
            pl.BlockSpec(memory_space=pltpu.SMEM),
            col(0), col(1), col(2), col(3), st_spec,
            pl.BlockSpec((RET_HEADS, T, RET_CHUNK), lambda b: (0, 0, 0)),
            pl.BlockSpec((RET_HEADS, T, RET_DV), lambda b: (0, 0, 0)),
            pl.BlockSpec((RET_HEADS, RET_CHUNK, RET_DK), lambda b: (0, 0, 0)),
            pl.BlockSpec((1, RET_WIDTH), lambda b: (0, 0)),
        ],
        out_specs=[pl.BlockSpec((T, RET_WIDTH), lambda b: (b, 0)), st_spec],
        out_shape=[
            jax.ShapeDtypeStruct((nb * T, RET_WIDTH), F32),
            jax.ShapeDtypeStruct(state.shape, F32),
        ],
        compiler_params=_params(("arbitrary",), 32),
        name="ret_sample",
    )(gl, proj_s, proj_s, proj_s, proj_s, state, dm, qd, kd, gn_ret.reshape(1, RET_WIDTH))


def _alibi_slope(g, h):
    n = N_GROUPS * ATT_HG
    return 2.0 ** (-ALIBI_MAX * (g * ATT_HG + h + 1) / n)


def _softmax_pv(scores, values):
    m = scores[0].max(axis=-1, keepdims=True)
    for s in scores[1:]:
        m = jnp.maximum(m, s.max(axis=-1, keepdims=True))
    l = None
    acc = None
    for s, v in zip(scores, values):
        p = jnp.exp(s - m)
        ls = p.sum(axis=-1, keepdims=True)
        a = _dot(p.astype(BF16), v)
        l = ls if l is None else l + ls
        acc = a if acc is None else acc + a
    return acc / l, m + jnp.log(l)


ATT_SUB = 4


def _att_prompt_kernel(q_ref, kp_ref, kc_ref, vp_ref, vc_ref, o_ref, lse_ref, *, group, dil, nsub):
    j = pl.program_id(2)
    blk = ATT_BLK
    r = lax.broadcasted_iota(jnp.int32, (blk, blk), 0)
    c = lax.broadcasted_iota(jnp.int32, (blk, blk), 1)
    dist_c = r - c
    dist_p = dist_c + blk
    valid_c = dist_c >= 0
    in_span = dist_p <= ATT_SPAN
    scale = ATT_HD ** -0.5
    for i in range(nsub):
        rows = slice(i * blk, (i + 1) * blk)
        prev_rows = slice((i - 1) * blk, i * blk)
        valid_p = (in_span & (j > 0)) if i == 0 else in_span
        for h in range(ATT_HG):
            sl = slice(h * ATT_HD, (h + 1) * ATT_HD)
            slope = _alibi_slope(group, h) * dil
            q = q_ref[rows, sl]
            kp, vp = (kp_ref[:, sl], vp_ref[:, sl]) if i == 0 else (kc_ref[prev_rows, sl], vc_ref[prev_rows, sl])
            sp = _dot_nt(q, kp) * scale - slope * dist_p.astype(F32)
            sc = _dot_nt(q, kc_ref[rows, sl]) * scale - slope * dist_c.astype(F32)
            sp = jnp.where(valid_p, sp, NEG)
            sc = jnp.where(valid_c, sc, NEG)
            o, lse = _softmax_pv([sp, sc], [vp, vc_ref[rows, sl]])
            o_ref[rows, sl] = o.astype(o_ref.dtype)
            lse_ref[rows, sl] = jnp.broadcast_to(lse, (blk, ATT_HD))


def _att_prompt(qkv, group):
    B, dil, L, _ = qkv.shape
    nsub = math.gcd(ATT_SUB, L // ATT_BLK)
    rows = nsub * ATT_BLK
    cur = lambda which: pl.BlockSpec((None, None, rows, ATT_GW), lambda b, r, j: (b, r, j, which))
    prev = lambda which: pl.BlockSpec(
        (None, None, ATT_BLK, ATT_GW), lambda b, r, j: (b, r, jnp.maximum(j * nsub - 1, 0), which))
    out_spec = pl.BlockSpec((None, None, rows, ATT_GW), lambda b, r, j: (b, r, j, 0))
    return pl.pallas_call(
        functools.partial(_att_prompt_kernel, group=group, dil=dil, nsub=nsub),
        grid=(B, dil, L // rows),
        in_specs=[cur(0), prev(1), cur(1), prev(2), cur(2)],
        out_specs=[out_spec, out_spec],
        out_shape=[
            jax.ShapeDtypeStruct((B, dil, L, ATT_GW), BF16),
            jax.ShapeDtypeStruct((B, dil, L, ATT_GW), F32),
        ],
        compiler_params=_params(("arbitrary", "arbitrary", "arbitrary"), 32),
        name=f"att_prompt_g{group}",
    )(qkv, qkv, qkv, qkv, qkv)


def _att_sample_kernel(q_ref, k_ref, v_ref, c_ref, o_ref, lse_ref, nc_ref, *, group, dil, wb, T):
    slots = 2 * ATT_HG
    nc_ref[pl.ds(0, (wb - T) * slots), :] = c_ref[pl.ds(T * slots, (wb - T) * slots), :]
    t = lax.broadcasted_iota(jnp.int32, (T, wb), 0)
    c = lax.broadcasted_iota(jnp.int32, (T, wb), 1)
    dist_c = wb + t - c
    valid_c = ((dist_c & (dil - 1)) == 0) & (dist_c <= ATT_SPAN * dil)
    tn = lax.broadcasted_iota(jnp.int32, (T, ATT_BLK), 0)
    cn = lax.broadcasted_iota(jnp.int32, (T, ATT_BLK), 1)
    dist_n = tn - cn
    valid_n = (dist_n >= 0) & ((dist_n & (dil - 1)) == 0) & (cn < T)
    scale = ATT_HD ** -0.5
    for h in range(ATT_HG):
        sl = slice(h * ATT_HD, (h + 1) * ATT_HD)
        slope = _alibi_slope(group, h)
        q = q_ref[:, sl].astype(BF16)
        kc = c_ref[pl.ds(h, wb, stride=slots), :].astype(BF16)
        vc = c_ref[pl.ds(ATT_HG + h, wb, stride=slots), :].astype(BF16)
        nc_ref[pl.ds((wb - T) * slots + h, T, stride=slots), :] = k_ref[:, sl]
        nc_ref[pl.ds((wb - T) * slots + ATT_HG + h, T, stride=slots), :] = v_ref[:, sl]
        kn = _pad_rows(k_ref[:, sl], ATT_BLK).astype(BF16)
        vn = _pad_rows(v_ref[:, sl], ATT_BLK).astype(BF16)
        sc = _dot_nt(q, kc) * scale - slope * dist_c.astype(F32)
        sn = _dot_nt(q, kn) * scale - slope * dist_n.astype(F32)
        sc = jnp.where(valid_c, sc, NEG)
        sn = jnp.where(valid_n, sn, NEG)
        o, lse = _softmax_pv([sc, sn], [vc, vn])
        o_ref[:, sl] = o.astype(o_ref.dtype)
        lse_ref[:, sl] = jnp.broadcast_to(lse, (T, ATT_HD))


def _att_sample(proj_s, cache, group, T):
    nb, wb = cache.shape[0], cache.shape[1]
    assert wb > T
    _, dil = ATT_GROUPS[group]
    c0 = ATT_COL0 // ATT_GW + 3 * group
    col = lambda k: pl.BlockSpec((T, ATT_GW), lambda b, k=k: (b, c0 + k))
    out_spec = pl.BlockSpec((T, ATT_GW), lambda b: (b, 0))
    cache_rows = cache.reshape(nb, wb * 2 * ATT_HG, ATT_HD)
    cache_spec = pl.BlockSpec((None, wb * 2 * ATT_HG, ATT_HD), lambda b: (b, 0, 0))
    o, lse, new_cache = pl.pallas_call(
        functools.partial(_att_sample_kernel, group=group, dil=dil, wb=wb, T=T),
        grid=(nb,),
        in_specs=[col(0), col(1), col(2), cache_spec],
        out_specs=[out_spec, out_spec, cache_spec],
        out_shape=[
            jax.ShapeDtypeStruct((nb * T, ATT_GW), F32),
            jax.ShapeDtypeStruct((nb * T, ATT_GW), F32),
            jax.ShapeDtypeStruct(cache_rows.shape, F32),
        ],
        compiler_params=_params(("arbitrary",), 48),
        name=f"att_sample_g{group}",
    )(proj_s, proj_s, proj_s, cache_rows)
    return o, lse, new_cache.reshape(cache.shape)


def _epilogue_kernel(x_ref, gr_ref, ga_ref, yret_ref, o0_ref, o1_ref, o2_ref, l0_ref, l1_ref, l2_ref,
                     wr_ref, wa_ref, wo_ref, gf_ref, x1_ref, h2_ref, *scratch, by_residue):
    def position_order(ref, scr):
        dil = ref.shape[0]
        if dil == 1:
            return ref[0].astype(F32)
        for r in range(dil):
            slab = ref[r].astype(F32)
            for cb in range(scr.shape[0]):
                scr[cb, pl.ds(r, ref.shape[1], stride=dil), :] = slab[:, cb * LANES:(cb + 1) * LANES]
        return jnp.concatenate([scr[cb] for cb in range(scr.shape[0])], axis=-1)

    if by_residue:
        o0, o1, o2 = (position_order(ref, scr) for ref, scr in zip((o0_ref, o1_ref, o2_ref), scratch[0:3]))
        l0, l1, l2 = (position_order(ref, scr) for ref, scr in zip((l0_ref, l1_ref, l2_ref), scratch[3:6]))
    else:
        o0, o1, o2 = o0_ref[...], o1_ref[...], o2_ref[...]
        l0, l1, l2 = l0_ref[...], l1_ref[...], l2_ref[...]
    m = jnp.maximum(jnp.maximum(l0, l1), l2)
    e0, e1, e2 = jnp.exp(l0 - m), jnp.exp(l1 - m), jnp.exp(l2 - m)
    inv = 1.0 / (e0 + e1 + e2)
    o = (e0 * inv) * o0 + (e1 * inv) * o1 + (e2 * inv) * o2
    att = _dot(o.astype(BF16), wa_ref[...])
    ret = _dot(yret_ref[...].astype(BF16), wr_ref[...])
    merged = (jax.nn.sigmoid(gr_ref[...].astype(F32)) * ret + jax.nn.sigmoid(ga_ref[...].astype(F32)) * att)
    x1 = x_ref[...] + _dot(merged.astype(BF16), wo_ref[...])
    x1_ref[...] = x1
    h2_ref[...] = _rmsnorm_rows(x1, gf_ref[...]).astype(BF16)


def _epilogue(x2d, proj2, yret, os_, lses, w_ret_o, w_att_o, w_o, g_ffn, *, tm, seq_len=None):
    M = x2d.shape[0]
    row = lambda w: pl.BlockSpec((tm, w), lambda m: (m, 0))
    const = lambda a: pl.BlockSpec(a.shape, lambda m: (0, 0), pipeline_mode=pl.Buffered(1))
    gf = g_ffn.reshape(1, D_MODEL)
    by_residue = seq_len is not None
    scratch = []
    if by_residue:
        tps = seq_len // tm
        att_specs = [pl.BlockSpec((None, a.shape[1], tm // a.shape[1], ATT_GW), lambda m: (m // tps, 0, m % tps, 0))
                     for a in (*os_, *lses)]
        scratch = [pltpu.VMEM((ATT_GW // LANES, tm, LANES), F32)] * 6
    else:
        att_specs = [row(ATT_GW)] * 6
    return pl.pallas_call(
        functools.partial(_epilogue_kernel, by_residue=by_residue),
        grid=(M // tm,),
        in_specs=[
            row(D_MODEL),
            pl.BlockSpec((tm, D_MODEL), lambda m: (m, 0)),
            pl.BlockSpec((tm, D_MODEL), lambda m: (m, 1)),
            row(RET_WIDTH), *att_specs,
            const(w_ret_o), const(w_att_o), const(w_o), const(gf),
        ],
        out_specs=[row(D_MODEL), row(D_MODEL)],
        out_shape=[jax.ShapeDtypeStruct((M, D_MODEL), F32), jax.ShapeDtypeStruct((M, D_MODEL), BF16)],
        scratch_shapes=scratch,
        compiler_params=_params(("arbitrary",), 56),
        name="epilogue",
    )(x2d, proj2, proj2, yret, *os_, *lses, w_ret_o, w_att_o, w_o, gf)


def _up_kernel(h_ref, w_ref, u_ref, tail_ref):
    acc = _dot(h_ref[...], w_ref[...])
    u_ref[...] = acc.astype(u_ref.dtype)
    tail_ref[...] = acc[acc.shape[0] - 8:, :]


def _up_proj(h2, w_up, *, tm):
    M = h2.shape[0]
    N = w_up.shape[1]
    tn = 1024
    return pl.pallas_call(
        _up_kernel,
        grid=(M // tm, N // tn),
        in_specs=[
            pl.BlockSpec((tm, D_MODEL), lambda m, n: (m, 0)),
            pl.BlockSpec((D_MODEL, tn), lambda m, n: (0, n)),
        ],
        out_specs=[
            pl.BlockSpec((tm, tn), lambda m, n: (m, n)),
            pl.BlockSpec((None, 8, tn), lambda m, n: (m, 0, n)),
        ],
        out_shape=[jax.ShapeDtypeStruct((M, N), BF16), jax.ShapeDtypeStruct((M // tm, 8, N), F32)],
        compiler_params=_params(("arbitrary", "arbitrary"), 48),
        name="up_proj",
    )(h2, w_up)


def _gelu(a):
    return 0.5 * a * (1.0 + lax.erf(a * math.sqrt(0.5)))


def _conv3(u, r1, r2, w_ref, b_ref):
    return b_ref[...] + (r2 * w_ref[0:1, :] + r1 * w_ref[1:2, :] + u * w_ref[2:3, :])


def _shifted_rows(u, prev8):
    ext = jnp.concatenate([prev8, u], axis=0)
    r1 = pltpu.roll(ext, 1, axis=0)[8:]
    r2 = pltpu.roll(ext, 2, axis=0)[8:]
    return r1, r2


def _down_prompt_kernel(ua_ref, ub_ref, pa_ref, pb_ref, cwa_ref, cwb_ref, cba_ref, cbb_ref, wd_ref,
                        x1_ref, gfin_ref, y_ref, *, tiles_per_seq, final_norm):
    m = pl.program_id(0)
    f = pl.program_id(1)
    seq_start = (m % tiles_per_seq) == 0

    def conv(u_ref, p_ref, w_ref, b_ref):
        u = u_ref[...].astype(F32)
        prev = jnp.where(seq_start, 0.0, p_ref[...].astype(F32))
        r1, r2 = _shifted_rows(u, prev)
        return _conv3(u, r1, r2, w_ref, b_ref)

    a = conv(ua_ref, pa_ref, cwa_ref, cba_ref)
    b = conv(ub_ref, pb_ref, cwb_ref, cbb_ref)
    part = _dot((_gelu(a) * b).astype(BF16), wd_ref[...])

    @pl.when(f == 0)
    def _():
        y_ref[...] = x1_ref[...] + part

    @pl.when(f > 0)
    def _():
        y_ref[...] += part

    if final_norm:
        @pl.when(f == pl.num_programs(1) - 1)
        def _():
            y_ref[...] = _rmsnorm_rows(y_ref[...], gfin_ref[...])


def _down_prompt(u, x1, conv_w, conv_b, w_down, g_final, *, tm, tiles_per_seq, final_norm):
    M = u.shape[0]
    tf = 512
    nf = D_FF // tf
    t8 = tm // 8
    cb = conv_b.reshape(1, 2 * D_FF)
    gfin = g_final.reshape(1, D_MODEL)
    prev = lambda off: pl.BlockSpec((8, tf), lambda m, f: (jnp.maximum(m * t8 - 1, 0), f + off))
    return pl.pallas_call(
        functools.partial(_down_prompt_kernel, tiles_per_seq=tiles_per_seq, final_norm=final_norm),
        grid=(M // tm, nf),
        in_specs=[
            pl.BlockSpec((tm, tf), lambda m, f: (m, f)),
            pl.BlockSpec((tm, tf), lambda m, f: (m, f + nf)),
            prev(0), prev(nf),
            pl.BlockSpec((CONV_W, tf), lambda m, f: (0, f)),
            pl.BlockSpec((CONV_W, tf), lambda m, f: (0, f + nf)),
            pl.BlockSpec((1, tf), lambda m, f: (0, f)),
            pl.BlockSpec((1, tf), lambda m, f: (0, f + nf)),
            pl.BlockSpec((tf, D_MODEL), lambda m, f: (f, 0)),
            pl.BlockSpec((tm, D_MODEL), lambda m, f: (m, 0)),
            pl.BlockSpec((1, D_MODEL), lambda m, f: (0, 0)),
        ],
        out_specs=pl.BlockSpec((tm, D_MODEL), lambda m, f: (m, 0)),
        out_shape=jax.ShapeDtypeStruct((M, D_MODEL), F32),
        compiler_params=_params(("arbitrary", "arbitrary"), 48),
        name="down_prompt",
    )(u, u, u, u, conv_w, conv_w, cb, cb, w_down, x1, gfin)


def _ffn_sample_kernel(h_ref, wua_ref, wub_ref, h1a_ref, h1b_ref, h2a_ref, h2b_ref, cwa_ref, cwb_ref,
                       cba_ref, cbb_ref, wd_ref, x1_ref, gfin_ref, y_ref, ua_ref, ub_ref, *, T, final_norm):
    f = pl.program_id(0)
    h = h_ref[...]
    t = lax.broadcasted_iota(jnp.int32, ua_ref.shape, 0) & (T - 1)

    def conv(wu_ref, hal1_ref, hal2_ref, w_ref, b_ref, u_out_ref):
        u = _dot(h, wu_ref[...])
        u_out_ref[...] = u
        r1 = jnp.where(t >= 1, pltpu.roll(u, 1, axis=0), hal1_ref[...])
        r2 = jnp.where(t >= 2, pltpu.roll(u, 2, axis=0), hal2_ref[...])
        return _conv3(u, r1, r2, w_ref, b_ref)

    a = conv(wua_ref, h1a_ref, h2a_ref, cwa_ref, cba_ref, ua_ref)
    b = conv(wub_ref, h1b_ref, h2b_ref, cwb_ref, cbb_ref, ub_ref)
    part = _dot((_gelu(a) * b).astype(BF16), wd_ref[...])

    @pl.when(f == 0)
    def _():
        y_ref[...] = x1_ref[...] + part

    @pl.when(f > 0)
    def _():
        y_ref[...] += part

    if final_norm:
        @pl.when(f == pl.num_programs(0) - 1)
        def _():
            y_ref[...] = _rmsnorm_rows(y_ref[...], gfin_ref[...])


def _ffn_sample(h2, x1, state_conv, w_up, conv_w, conv_b, w_down, g_final, T, final_norm):
    M = h2.shape[0]
    nb = M // T
    tf = 512
    nf = D_FF // tf
    F2 = 2 * D_FF
    sc = state_conv.astype(F32)
    hal1 = jnp.concatenate([sc[:, 1:2], jnp.zeros((nb, T - 1, F2), F32)], axis=1).reshape(M, F2)
    hal2 = jnp.concatenate([sc[:, 0:2], jnp.zeros((nb, T - 2, F2), F32)], axis=1).reshape(M, F2)
    cb = conv_b.reshape(1, F2)
    gfin = g_final.reshape(1, D_MODEL)
    full = lambda w: pl.BlockSpec((M, w), lambda f: (0, 0))
    ca = lambda r, w=tf: pl.BlockSpec((r, w), lambda f: (0, f))
    cbk = lambda r, w=tf: pl.BlockSpec((r, w), lambda f: (0, f + nf))
    y, ua, ub = pl.pallas_call(
        functools.partial(_ffn_sample_kernel, T=T, final_norm=final_norm),
        grid=(nf,),
        in_specs=[
            full(D_MODEL), ca(D_MODEL), cbk(D_MODEL), ca(M), cbk(M), ca(M), cbk(M),
            ca(CONV_W), cbk(CONV_W), ca(1), cbk(1),
            pl.BlockSpec((tf, D_MODEL), lambda f: (f, 0)),
            full(D_MODEL), pl.BlockSpec((1, D_MODEL), lambda f: (0, 0)),
        ],
        out_specs=[full(D_MODEL), ca(M), ca(M)],
        out_shape=[
            jax.ShapeDtypeStruct((M, D_MODEL), F32),
            jax.ShapeDtypeStruct((M, D_FF), F32),
            jax.ShapeDtypeStruct((M, D_FF), F32),
        ],
        compiler_params=_params(("arbitrary",), 48),
        name="ffn_sample",
    )(h2, w_up, w_up, hal1, hal1, hal2, hal2, conv_w, conv_w, cb, cb, w_down, x1, gfin)
    u = jnp.concatenate([ua, ub], axis=-1).reshape(nb, T, F2)
    return y, u[:, T - (CONV_W - 1):]


def _permute_w_in(w):
    split = 4 * RET_WIDTH + 3 * N_GROUPS * ATT_GW
    return jnp.concatenate([w[:, split:], w[:, :split]], axis=1).astype(BF16)


def kernel(x_prompt, x_sample, state_ret, cache_kv_w128, cache_kv_w512, cache_kv_w2048, state_conv, g_mix, w_in,
           gn_ret, w_ret_o, w_att_o, w_o, g_ffn, w_up, conv_w, conv_b, w_down, g_final):
    B, S, _ = x_prompt.shape
    NB, T, _ = x_sample.shape
    depth = w_in.shape[0]
    caches = (cache_kv_w128, cache_kv_w512, cache_kv_w2048)
    F2 = 2 * D_FF
    TM = 1024
    tiles_per_seq = S // TM

    xp = x_prompt.reshape(B * S, D_MODEL)
    xs = x_sample.reshape(NB * T, D_MODEL)
    ret_p, ret_s, conv_p, conv_s = [], [], [], []
    kv_p = [[] for _ in range(N_GROUPS)]
    kv_s = [[] for _ in range(N_GROUPS)]
    for l in range(depth):
        w_in_l = _permute_w_in(w_in[l])
        w_ret_o_l = w_ret_o[l].astype(BF16)
        w_att_o_l = w_att_o[l].astype(BF16)
        w_o_l = w_o[l].astype(BF16)
        w_up_l = w_up[l].astype(BF16)
        w_down_l = w_down[l].astype(BF16)

        proj, qkv0, qkv1, qkv2, kvf = _inproj_prompt(xp, g_mix[l], w_in_l, B=B, S=S, tm=TM)
        yret, st = _ret_prompt(proj.reshape(B, S, ATT_COL0), gn_ret[l])
        ret_p.append(st)
        os_, lses = [], []
        for g, qkv in enumerate((qkv0, qkv1, qkv2)):
            o, lse = _att_prompt(qkv, g)
            os_.append(o)
            lses.append(lse)
        kvf3 = kvf.reshape(B, S, 2 * N_GROUPS * ATT_GW)
        for g, (win, _) in enumerate(ATT_GROUPS):
            w = min(win, S)
            kv_p[g].append(kvf3[:, S - w:, 2 * ATT_GW * g:2 * ATT_GW * (g + 1)].reshape(B, w, 2, ATT_HG, ATT_HD))
        x1, h2 = _epilogue(xp, proj, yret.reshape(B * S, RET_WIDTH), os_, lses,
                           w_ret_o_l, w_att_o_l, w_o_l, g_ffn[l], tm=256, seq_len=S)
        u, utail = _up_proj(h2, w_up_l, tm=TM)
        conv_p.append(utail[tiles_per_seq - 1::tiles_per_seq, 8 - (CONV_W - 1):])
        xp = _down_prompt(u, x1, conv_w[l], conv_b[l], w_down_l, g_final,
                          tm=512, tiles_per_seq=S // 512, final_norm=l == depth - 1)

        proj_s = _inproj_sample(xs, g_mix[l], w_in_l)
        yret_s, st_s = _ret_sample(proj_s, state_ret[l], gn_ret[l], T)
        ret_s.append(st_s)
        os_, lses = [], []
        for g in range(N_GROUPS):
            o, lse, new_cache = _att_sample(proj_s, caches[g][l], g, T)
            os_.append(o)
            lses.append(lse)
            kv_s[g].append(new_cache)
        x1s, h2s = _epilogue(xs, proj_s, yret_s, os_, lses, w_ret_o_l, w_att_o_l, w_o_l, g_ffn[l], tm=NB * T)
        xs, cv = _ffn_sample(h2s, x1s, state_conv[l], w_up_l, conv_w[l], conv_b[l], w_down_l,
                             g_final, T, final_norm=l == depth - 1)
        conv_s.append(cv)

    return (xp.reshape(B, S, D_MODEL), xs.reshape(NB, T, D_MODEL),
            jnp.stack(ret_p), jnp.stack(ret_s),
            jnp.stack(kv_p[0]), jnp.stack(kv_s[0]),
            jnp.stack(kv_p[1]), jnp.stack(kv_s[1]),
            jnp.stack(kv_p[2]), jnp.stack(kv_s[2]),
            jnp.stack(conv_p), jnp.stack(conv_s))
```

```python
import functools
import math

import jax
import jax.numpy as jnp
from jax import lax
from jax.experimental import pallas as pl
from jax.experimental.pallas import tpu as pltpu

D_MODEL = 2048
RET_HEADS = 8
RET_DK = 128
RET_DV = 128
RET_WIDTH = RET_HEADS * RET_DV
RET_CHUNK = 128
ATT_GROUPS = ((128, 1), (512, 4), (2048, 16))
N_GROUPS = 3
ATT_HG = 4
ATT_HD = 128
ATT_GW = ATT_HG * ATT_HD
ATT_SPAN = 128
ATT_BLK = 128
ALIBI_MAX = 8.0
D_FF = 5632
CONV_W = 3
EPS = 1e-6
NEG = -1e30
IN_COLS = 4 * RET_WIDTH + 3 * N_GROUPS * ATT_GW + 2 * D_MODEL

GATE_COLS = 2 * D_MODEL
RET_COL0 = GATE_COLS
MAIN_COLS = GATE_COLS + 4 * RET_WIDTH
ATT_COLS = 3 * N_GROUPS * ATT_GW
CAST_TN = 512

LANES = 128
MIB = 1024 * 1024
BF16 = jnp.bfloat16
F32 = jnp.float32


def _params(semantics, vmem_mib):
    return pltpu.CompilerParams(dimension_semantics=semantics, vmem_limit_bytes=vmem_mib * MIB)


def _dot(a, b):
    return jnp.dot(a, b, preferred_element_type=F32)


def _dot_nt(a, b):
    return lax.dot_general(a, b, (((1,), (1,)), ((), ())), preferred_element_type=F32)


def _rmsnorm_rows(x, g):
    ms = jnp.mean(x * x, axis=-1, keepdims=True)
    return x * lax.rsqrt(ms + EPS) * g


def _cast_cols_kernel(w_ref, o_ref):
    o_ref[...] = w_ref[...].astype(o_ref.dtype)


def _cast_cols(w, src_tile, n_tiles, tn, name):
    K = w.shape[0]
    return pl.pallas_call(
        _cast_cols_kernel,
        grid=(n_tiles,),
        in_specs=[pl.BlockSpec((K, tn), lambda n: (0, src_tile(n)))],
        out_specs=pl.BlockSpec((K, tn), lambda n: (0, n)),
        out_shape=jax.ShapeDtypeStruct((K, n_tiles * tn), BF16),
        compiler_params=_params(("arbitrary",), 32),
        name=name,
    )(w)


def _inproj_sample_kernel(x_ref, g_ref, w_ref, proj_ref, h_scr):
    @pl.when(pl.program_id(0) == 0)
    def _():
        h_scr[...] = _rmsnorm_rows(x_ref[...], g_ref[...]).astype(BF16)

    proj_ref[...] = _dot(h_scr[...], w_ref[...])


def _inproj_sample(x2d, g, w_bf16, name):
    M = x2d.shape[0]
    N = w_bf16.shape[1]
    tn = 512
    return pl.pallas_call(
        _inproj_sample_kernel,
        grid=(N // tn,),
        in_specs=[
            pl.BlockSpec((M, D_MODEL), lambda n: (0, 0)),
            pl.BlockSpec((1, D_MODEL), lambda n: (0, 0)),
            pl.BlockSpec((D_MODEL, tn), lambda n: (0, n)),
        ],
        out_specs=pl.BlockSpec((M, tn), lambda n: (0, n)),
        out_shape=jax.ShapeDtypeStruct((M, N), F32),
        scratch_shapes=[pltpu.VMEM((M, D_MODEL), BF16)],
        compiler_params=_params(("arbitrary",), 32),
        name=name,
    )(x2d, g.reshape(1, D_MODEL), w_bf16)


def _inproj_main_kernel(x_ref, g_ref, w_ref, main_ref, h_ref):
    @pl.when(pl.program_id(1) == 0)
    def _():
        h_ref[...] = _rmsnorm_rows(x_ref[...], g_ref[...]).astype(BF16)

    main_ref[...] = _dot(h_ref[...], w_ref[...]).astype(BF16)


def _inproj_main(x2d, g, w_main, *, tm):
    M = x2d.shape[0]
    N = w_main.shape[1]
    tn = 1024
    return pl.pallas_call(
        _inproj_main_kernel,
        grid=(M // tm, N // tn),
        in_specs=[
            pl.BlockSpec((tm, D_MODEL), lambda m, n: (m, 0)),
            pl.BlockSpec((1, D_MODEL), lambda m, n: (0, 0)),
            pl.BlockSpec((D_MODEL, tn), lambda m, n: (0, n)),
        ],
        out_specs=[pl.BlockSpec((tm, tn), lambda m, n: (m, n)), pl.BlockSpec((tm, D_MODEL), lambda m, n: (m, 0))],
        out_shape=[jax.ShapeDtypeStruct((M, N), BF16), jax.ShapeDtypeStruct((M, D_MODEL), BF16)],
        compiler_params=_params(("arbitrary", "arbitrary"), 48),
        name="inproj_main",
    )(x2d, g.reshape(1, D_MODEL), w_main)


def _inproj_att_kernel(h_ref, w_ref, a0_ref, a1_ref, a2_ref, kv_ref, acc_scr):
    n = pl.program_id(1)
    acc = _dot(h_ref[...], w_ref[...])
    kv_ref[...] = acc[:, ATT_GW:]

    for grp, a_ref in enumerate((a0_ref, a1_ref, a2_ref)):
        dil = ATT_GROUPS[grp][1]

        @pl.when(n == grp)
        def _(a_ref=a_ref, dil=dil):
            if dil == 1:
                a_ref[0] = acc.astype(BF16)
            else:
                rows = acc_scr.shape[1] // dil
                for cb in range(acc_scr.shape[0]):
                    lanes = slice(cb * LANES, (cb + 1) * LANES)
                    acc_scr[cb] = acc[:, lanes]
                    for r in range(dil):
                        a_ref[r, :, lanes] = acc_scr[cb, pl.ds(r, rows, stride=dil), :].astype(BF16)


def _inproj_att(h, w_att, *, B, S, tm):
    M = h.shape[0]
    tn = 3 * ATT_GW
    tps = S // tm
    out_shape, out_specs = [], []
    for _, dil in ATT_GROUPS:
        out_shape.append(jax.ShapeDtypeStruct((B, dil, S // dil, tn), BF16))
        out_specs.append(pl.BlockSpec((None, dil, tm // dil, tn), lambda m, n: (m // tps, 0, m % tps, 0)))
    out_shape.append(jax.ShapeDtypeStruct((M, N_GROUPS * 2 * ATT_GW), F32))
    out_specs.append(pl.BlockSpec((tm, 2 * ATT_GW), lambda m, n: (m, n)))
    return pl.pallas_call(
        _inproj_att_kernel,
        grid=(M // tm, N_GROUPS),
        in_specs=[
            pl.BlockSpec((tm, D_MODEL), lambda m, n: (m, 0)),
            pl.BlockSpec((D_MODEL, tn), lambda m, n: (0, n)),
        ],
        out_specs=out_specs,
        out_shape=out_shape,
        scratch_shapes=[pltpu.VMEM((tn // LANES, tm, LANES), F32)],
        compiler_params=_params(("arbitrary", "arbitrary"), 48),
        name="inproj_att",
    )(h, w_att)


def _ret_tables(lq):
    lg = jnp.log(1.0 - jnp.exp2(-5.0 - jnp.arange(RET_HEADS, dtype=F32)))
    scale = RET_DK ** -0.5
    i = jnp.arange(lq, dtype=F32)
    j = jnp.arange(RET_CHUNK, dtype=F32)
    diff = i[:, None] - j[None, :]
    live = (diff >= 0) & (j[None, :] < lq)
    dm = jnp.where(live[None], jnp.exp(jnp.maximum(diff, 0.0)[None] * lg[:, None, None]), 0.0) * scale
    qd = jnp.exp((i[None, :] + 1.0) * lg[:, None])
    qd = jnp.broadcast_to(qd[:, :, None], (RET_HEADS, lq, RET_DV))
    kd = jnp.where(j[None, :] < lq, jnp.exp(jnp.maximum(lq - 1.0 - j, 0.0)[None, :] * lg[:, None]), 0.0) * scale
    kd = jnp.broadcast_to(kd[:, :, None], (RET_HEADS, RET_CHUNK, RET_DK))
    gl = jnp.exp(lq * lg)
    return dm.astype(F32), qd.astype(F32), kd.astype(F32), gl.astype(F32)


def _ret_head(q, k, v, st, dm, qd, kd, gl):
    s = _dot_nt(q, k) * dm
    inner = _dot(s.astype(BF16), v)
    cross = _dot(q, st.astype(BF16)) * qd
    kt = (k.astype(F32) * kd).T.astype(BF16)
    new_st = st * gl + _dot(kt, v)
    return inner + cross, new_st


def _groupnorm_gate(y, gn, rg):
    mu = jnp.mean(y, axis=-1, keepdims=True)
    yc = y - mu
    yn = yc * lax.rsqrt(jnp.mean(yc * yc, axis=-1, keepdims=True) + EPS)
    return (rg * jax.nn.sigmoid(rg)) * (yn * gn)


def _ret_prompt_kernel(gl_ref, q_ref, k_ref, v_ref, rg_ref, dm_ref, qd_ref, kd_ref, gn_ref, y_ref, st_ref):
    c = pl.program_id(1)

    @pl.when(c == 0)
    def _():
        st_ref[...] = jnp.zeros_like(st_ref)

    for h in range(RET_HEADS):
        sl = slice(h * RET_DK, (h + 1) * RET_DK)
        y, new_st = _ret_head(q_ref[:, sl], k_ref[:, sl], v_ref[:, sl], st_ref[h],
                              dm_ref[h], qd_ref[h], kd_ref[h], gl_ref[h])
        st_ref[h] = new_st
        y_ref[:, sl] = _groupnorm_gate(y, gn_ref[:, sl], rg_ref[:, sl].astype(F32)).astype(y_ref.dtype)


def _ret_prompt(proj3, gn_ret):
    B, S, _ = proj3.shape
    L = RET_CHUNK
    dm, qd, kd, gl = _ret_tables(L)
    cb = RET_COL0 // RET_WIDTH
    col = lambda k: pl.BlockSpec((None, L, RET_WIDTH), lambda b, c, k=k: (b, c, cb + k))
    tab = pl.BlockSpec((RET_HEADS, L, RET_DK), lambda b, c: (0, 0, 0))
    return pl.pallas_call(
        _ret_prompt_kernel,
        grid=(B, S // L),
        in_specs=[
            pl.BlockSpec(memory_space=pltpu.SMEM),
            col(0), col(1), col(2), col(3), tab, tab, tab,
            pl.BlockSpec((1, RET_WIDTH), lambda b, c: (0, 0)),
        ],
        out_specs=[
            pl.BlockSpec((None, L, RET_WIDTH), lambda b, c: (b, c, 0)),
            pl.BlockSpec((None, RET_HEADS, RET_DK, RET_DV), lambda b, c: (b, 0, 0, 0)),
        ],
        out_shape=[
            jax.ShapeDtypeStruct((B, S, RET_WIDTH), BF16),
            jax.ShapeDtypeStruct((B, RET_HEADS, RET_DK, RET_DV), F32),
        ],
        compiler_params=_params(("arbitrary", "arbitrary"), 32),
        name="ret_prompt",
    )(gl, proj3, proj3, proj3, proj3, dm, qd, kd, gn_ret.reshape(1, RET_WIDTH))


def _pad_rows(x, rows):
    x = x.astype(F32)
    return jnp.concatenate([x, jnp.zeros((rows - x.shape[0], x.shape[1]), F32)], axis=0)


def _ret_sample_kernel(gl_ref, q_ref, k_ref, v_ref, rg_ref, st_in_ref, dm_ref, qd_ref, kd_ref, gn_ref,
                       y_ref, st_ref):
    for h in range(RET_HEADS):
        sl = slice(h * RET_DK, (h + 1) * RET_DK)
        k = _pad_rows(k_ref[:, sl], RET_CHUNK).astype(BF16)
        v = _pad_rows(v_ref[:, sl], RET_CHUNK).astype(BF16)
        y, new_st = _ret_head(q_ref[:, sl].astype(BF16), k, v, st_in_ref[h],
                              dm_ref[h], qd_ref[h], kd_ref[h], gl_ref[h])
        st_ref[h] = new_st
        y_ref[:, sl] = _groupnorm_gate(y, gn_ref[:, sl], rg_ref[:, sl].astype(F32)).astype(y_ref.dtype)


def _ret_sample(proj_s, state, gn_ret, T):
    nb = state.shape[0]
    dm, qd, kd, gl = _ret_tables(T)
    cb = RET_COL0 // RET_WIDTH
    col = lambda k: pl.BlockSpec((T, RET_WIDTH), lambda b, k=k: (b, cb + k))
    st_spec = pl.BlockSpec((None, RET_HEADS, RET_DK, RET_DV), lambda b: (b, 0, 0, 0))
    return pl.pallas_call(
        _ret_sample_kernel,
        grid=(nb,),
        in_specs=[
            pl.BlockSpec(memory_space=pltpu.SMEM),
            col(0), col(1), col(2), col(3), st_spec,
            pl.BlockSpec((RET_HEADS, T, RET_CHUNK), lambda b: (0, 0, 0)),
            pl.BlockSpec((RET_HEADS, T, RET_DV), lambda b: (0, 0, 0)),
            pl.BlockSpec((RET_HEADS, RET_CHUNK, RET_DK), lambda b: (0, 0, 0)),
            pl.BlockSpec((1, RET_WIDTH), lambda b: (0, 0)),
        ],
        out_specs=[pl.BlockSpec((T, RET_WIDTH), lambda b: (b, 0)), st_spec],
        out_shape=[
            jax.ShapeDtypeStruct((nb * T, RET_WIDTH), F32),
            jax.ShapeDtypeStruct(state.shape, F32),
        ],
        compiler_params=_params(("arbitrary",), 32),
        name="ret_sample",
    )(gl, proj_s, proj_s, proj_s, proj_s, state, dm, qd, kd, gn_ret.reshape(1, RET_WIDTH))


def _alibi_slope(g, h):
    n = N_GROUPS * ATT_HG
    return 2.0 ** (-ALIBI_MAX * (g * ATT_HG + h + 1) / n)


def _softmax_pv(scores, values):
    m = scores[0].max(axis=-1, keepdims=True)
    for s in scores[1:]:
        m = jnp.maximum(m, s.max(axis=-1, keepdims=True))
    l = None
    acc = None
    for s, v in zip(scores, values):
        p = jnp.exp(s - m)
        ls = p.sum(axis=-1, keepdims=True)
        a = _dot(p.astype(BF16), v)
        l = ls if l is None else l + ls
        acc = a if acc is None else acc + a
    return acc / l, m + jnp.log(l)


ATT_SUB = 4


def _att_prompt_kernel(q_ref, kp_ref, kc_ref, vp_ref, vc_ref, o_ref, lse_ref, *, group, dil, nsub):
    j = pl.program_id(2)
    blk = ATT_BLK
    r = lax.broadcasted_iota(jnp.int32, (blk, blk), 0)
    c = lax.broadcasted_iota(jnp.int32, (blk, blk), 1)
    dist_c = r - c
    dist_p = dist_c + blk
    valid_c = dist_c >= 0
    in_span = dist_p <= ATT_SPAN
    scale = ATT_HD ** -0.5
    for i in range(nsub):
        rows = slice(i * blk, (i + 1) * blk)
        prev_rows = slice((i - 1) * blk, i * blk)
        valid_p = (in_span & (j > 0)) if i == 0 else in_span
        for h in range(ATT_HG):
            sl = slice(h * ATT_HD, (h + 1) * ATT_HD)
            slope = _alibi_slope(group, h) * dil
            q = q_ref[rows, sl]
            kp, vp = (kp_ref[:, sl], vp_ref[:, sl]) if i == 0 else (kc_ref[prev_rows, sl], vc_ref[prev_rows, sl])
            sp = _dot_nt(q, kp) * scale - slope * dist_p.astype(F32)
            sc = _dot_nt(q, kc_ref[rows, sl]) * scale - slope * dist_c.astype(F32)
            sp = jnp.where(valid_p, sp, NEG)
            sc = jnp.where(valid_c, sc, NEG)
            o, lse = _softmax_pv([sp, sc], [vp, vc_ref[rows, sl]])
            o_ref[rows, sl] = o.astype(o_ref.dtype)
            lse_ref[rows, sl] = jnp.broadcast_to(lse, (blk, ATT_HD))


def _att_prompt(qkv, group):
    B, dil, L, _ = qkv.shape
    nsub = math.gcd(ATT_SUB, L // ATT_BLK)
    rows = nsub * ATT_BLK
    cur = lambda which: pl.BlockSpec((None, None, rows, ATT_GW), lambda b, r, j: (b, r, j, which))
    prev = lambda which: pl.BlockSpec(
        (None, None, ATT_BLK, ATT_GW), lambda b, r, j: (b, r, jnp.maximum(j * nsub - 1, 0), which))
    out_spec = pl.BlockSpec((None, None, rows, ATT_GW), lambda b, r, j: (b, r, j, 0))
    return pl.pallas_call(
        functools.partial(_att_prompt_kernel, group=group, dil=dil, nsub=nsub),
        grid=(B, dil, L // rows),
        in_specs=[cur(0), prev(1), cur(1), prev(2), cur(2)],
        out_specs=[out_spec, out_spec],
        out_shape=[
            jax.ShapeDtypeStruct((B, dil, L, ATT_GW), BF16),
            jax.ShapeDtypeStruct((B, dil, L, ATT_GW), F32),
        ],
        compiler_params=_params(("arbitrary", "arbitrary", "arbitrary"), 32),
        name=f"att_prompt_g{group}",
    )(qkv, qkv, qkv, qkv, qkv)


def _att_sample_kernel(q_ref, k_ref, v_ref, c_ref, o_ref, lse_ref, nc_ref, *, group, dil, wb, T):
    slots = 2 * ATT_HG
    nc_ref[pl.ds(0, (wb - T) * slots), :] = c_ref[pl.ds(T * slots, (wb - T) * slots), :]
    t = lax.broadcasted_iota(jnp.int32, (T, wb), 0)
    c = lax.broadcasted_iota(jnp.int32, (T, wb), 1)
    dist_c = wb + t - c
    valid_c = ((dist_c & (dil - 1)) == 0) & (dist_c <= ATT_SPAN * dil)
    tn = lax.broadcasted_iota(jnp.int32, (T, ATT_BLK), 0)
    cn = lax.broadcasted_iota(jnp.int32, (T, ATT_BLK), 1)
    dist_n = tn - cn
    valid_n = (dist_n >= 0) & ((dist_n & (dil - 1)) == 0) & (cn < T)
    scale = ATT_HD ** -0.5
    for h in range(ATT_HG):
        sl = slice(h * ATT_HD, (h + 1) * ATT_HD)
        slope = _alibi_slope(group, h)
        q = q_ref[:, sl].astype(BF16)
        kc = c_ref[pl.ds(h, wb, stride=slots), :].astype(BF16)
        vc = c_ref[pl.ds(ATT_HG + h, wb, stride=slots), :].astype(BF16)
        nc_ref[pl.ds((wb - T) * slots + h, T, stride=slots), :] = k_ref[:, sl]
        nc_ref[pl.ds((wb - T) * slots + ATT_HG + h, T, stride=slots), :] = v_ref[:, sl]
        kn = _pad_rows(k_ref[:, sl], ATT_BLK).astype(BF16)
        vn = _pad_rows(v_ref[:, sl], ATT_BLK).astype(BF16)
        sc = _dot_nt(q, kc) * scale - slope * dist_c.astype(F32)
        sn = _dot_nt(q, kn) * scale - slope * dist_n.astype(F32)
        sc = jnp.where(valid_c, sc, NEG)
        sn = jnp.where(valid_n, sn, NEG)
        o, lse = _softmax_pv([sc, sn], [vc, vn])
        o_ref[:, sl] = o.astype(o_ref.dtype)
        lse_ref[:, sl] = jnp.broadcast_to(lse, (T, ATT_HD))


def _att_sample(proj_s, cache, group, T):
    nb, wb = cache.shape[0], cache.shape[1]
    assert wb > T
    _, dil = ATT_GROUPS[group]
    c0 = 3 * group
    col = lambda k: pl.BlockSpec((T, ATT_GW), lambda b, k=k: (b, c0 + k))
    out_spec = pl.BlockSpec((T, ATT_GW), lambda b: (b, 0))
    cache_rows = cache.reshape(nb, wb * 2 * ATT_HG, ATT_HD)
    cache_spec = pl.BlockSpec((None, wb * 2 * ATT_HG, ATT_HD), lambda b: (b, 0, 0))
    o, lse, new_cache = pl.pallas_call(
        functools.partial(_att_sample_kernel, group=group, dil=dil, wb=wb, T=T),
        grid=(nb,),
        in_specs=[col(0), col(1), col(2), cache_spec],
        out_specs=[out_spec, out_spec, cache_spec],
        out_shape=[
            jax.ShapeDtypeStruct((nb * T, ATT_GW), F32),
            jax.ShapeDtypeStruct((nb * T, ATT_GW), F32),
            jax.ShapeDtypeStruct(cache_rows.shape, F32),
        ],
        compiler_params=_params(("arbitrary",), 48),
        name=f"att_sample_g{group}",
    )(proj_s, proj_s, proj_s, cache_rows)
    return o, lse, new_cache.reshape(cache.shape)


def _epilogue_kernel(x_ref, gr_ref, ga_ref, yret_ref, o0_ref, o1_ref, o2_ref, l0_ref, l1_ref, l2_ref,
                     wr_ref, wa_ref, wo_ref, gf_ref, x1_ref, h2_ref, *scratch, by_residue):
    def position_order(ref, scr):
        dil = ref.shape[0]
        if dil == 1:
            return ref[0].astype(F32)
        for r in range(dil):
            slab = ref[r].astype(F32)
            for cb in range(scr.shape[0]):
                scr[cb, pl.ds(r, ref.shape[1], stride=dil), :] = slab[:, cb * LANES:(cb + 1) * LANES]
        return jnp.concatenate([scr[cb] for cb in range(scr.shape[0])], axis=-1)

    if by_residue:
        o0, o1, o2 = (position_order(ref, scr) for ref, scr in zip((o0_ref, o1_ref, o2_ref), scratch[0:3]))
        l0, l1, l2 = (position_order(ref, scr) for ref, scr in zip((l0_ref, l1_ref, l2_ref), scratch[3:6]))
    else:
        o0, o1, o2 = o0_ref[...], o1_ref[...], o2_ref[...]
        l0, l1, l2 = l0_ref[...], l1_ref[...], l2_ref[...]
    m = jnp.maximum(jnp.maximum(l0, l1), l2)
    e0, e1, e2 = jnp.exp(l0 - m), jnp.exp(l1 - m), jnp.exp(l2 - m)
    inv = 1.0 / (e0 + e1 + e2)
    o = (e0 * inv) * o0 + (e1 * inv) * o1 + (e2 * inv) * o2
    att = _dot(o.astype(BF16), wa_ref[...])
    ret = _dot(yret_ref[...].astype(BF16), wr_ref[...])
    merged = (jax.nn.sigmoid(gr_ref[...].astype(F32)) * ret + jax.nn.sigmoid(ga_ref[...].astype(F32)) * att)
    x1 = x_ref[...] + _dot(merged.astype(BF16), wo_ref[...])
    x1_ref[...] = x1
    h2_ref[...] = _rmsnorm_rows(x1, gf_ref[...]).astype(BF16)


def _epilogue(x2d, proj2, yret, os_, lses, w_ret_o, w_att_o, w_o, g_ffn, *, tm, seq_len=None):
    M = x2d.shape[0]
    row = lambda w: pl.BlockSpec((tm, w), lambda m: (m, 0))
    const = lambda a: pl.BlockSpec(a.shape, lambda m: (0, 0), pipeline_mode=pl.Buffered(1))
    gf = g_ffn.reshape(1, D_MODEL)
    by_residue = seq_len is not None
    scratch = []
    if by_residue:
        tps = seq_len // tm
        att_specs = [pl.BlockSpec((None, a.shape[1], tm // a.shape[1], ATT_GW), lambda m: (m // tps, 0, m % tps, 0))
                     for a in (*os_, *lses)]
        scratch = [pltpu.VMEM((ATT_GW // LANES, tm, LANES), F32)] * 6
    else:
        att_specs = [row(ATT_GW)] * 6
    return pl.pallas_call(
        functools.partial(_epilogue_kernel, by_residue=by_residue),
        grid=(M // tm,),
        in_specs=[
            row(D_MODEL),
            pl.BlockSpec((tm, D_MODEL), lambda m: (m, 0)),
            pl.BlockSpec((tm, D_MODEL), lambda m: (m, 1)),
            row(RET_WIDTH), *att_specs,
            const(w_ret_o), const(w_att_o), const(w_o), const(gf),
        ],
        out_specs=[row(D_MODEL), row(D_MODEL)],
        out_shape=[jax.ShapeDtypeStruct((M, D_MODEL), F32), jax.ShapeDtypeStruct((M, D_MODEL), BF16)],
        scratch_shapes=scratch,
        compiler_params=_params(("arbitrary",), 56),
        name="epilogue",
    )(x2d, proj2, proj2, yret, *os_, *lses, w_ret_o, w_att_o, w_o, gf)


def _gelu(a):
    return 0.5 * a * (1.0 + lax.erf(a * math.sqrt(0.5)))


def _conv3(u, r1, r2, w_ref, b_ref):
    return b_ref[...] + (r2 * w_ref[0:1, :] + r1 * w_ref[1:2, :] + u * w_ref[2:3, :])


UP_CHUNK = 256


def _up_gate_kernel(h_ref, w_ref, cwa_ref, cwb_ref, cba_ref, cbb_ref, g_ref, ta_ref, tb_ref, carry_scr,
                    *, tiles_per_seq):
    m = pl.program_id(0)
    f = pl.program_id(1)
    tm = h_ref.shape[0]

    @pl.when(m % tiles_per_seq == 0)
    def _():
        carry_scr[f] = jnp.zeros(carry_scr.shape[1:], F32)

    lhs = h_ref[...]
    nchunk = g_ref.shape[1] // UP_CHUNK
    us = [_dot(lhs, w_ref[:, 2 * c * UP_CHUNK:2 * (c + 1) * UP_CHUNK]) for c in range(nchunk)]
    for c, u2 in enumerate(us):
        cols = slice(c * UP_CHUNK, (c + 1) * UP_CHUNK)

        def conv(u, half, cw_ref, cb_ref, tail_ref):
            prev = carry_scr[f, half, :, cols]
            carry_scr[f, half, :, cols] = u[tm - 8:]
            tail_ref[:, cols] = u[tm - 8:]
            ext = jnp.concatenate([prev, u], axis=0)
            r1 = pltpu.roll(ext, 1, axis=0)[8:]
            r2 = pltpu.roll(ext, 2, axis=0)[8:]
            return cb_ref[:, cols] + (r2 * cw_ref[0:1, cols] + r1 * cw_ref[1:2, cols] + u * cw_ref[2:3, cols])

        a = conv(u2[:, :UP_CHUNK], 0, cwa_ref, cba_ref, ta_ref)
        b = conv(u2[:, UP_CHUNK:], 1, cwb_ref, cbb_ref, tb_ref)
        g_ref[:, cols] = (_gelu(a) * b).astype(BF16)


def _up_gate(h2, w_up_ab, conv_w, conv_b, *, tm, tiles_per_seq):
    M = h2.shape[0]
    tf = 512
    nf = D_FF // tf
    cb = conv_b.reshape(1, 2 * D_FF)
    return pl.pallas_call(
        functools.partial(_up_gate_kernel, tiles_per_seq=tiles_per_seq),
        grid=(M // tm, nf),
        in_specs=[
            pl.BlockSpec((tm, D_MODEL), lambda m, f: (m, 0)),
            pl.BlockSpec((D_MODEL, 2 * tf), lambda m, f: (0, f)),
            pl.BlockSpec((CONV_W, tf), lambda m, f: (0, f)),
            pl.BlockSpec((CONV_W, tf), lambda m, f: (0, f + nf)),
            pl.BlockSpec((1, tf), lambda m, f: (0, f)),
            pl.BlockSpec((1, tf), lambda m, f: (0, f + nf)),
        ],
        out_specs=[
            pl.BlockSpec((tm, tf), lambda m, f: (m, f)),
            pl.BlockSpec((None, 8, tf), lambda m, f: (m, 0, f)),
            pl.BlockSpec((None, 8, tf), lambda m, f: (m, 0, f)),
        ],
        out_shape=[
            jax.ShapeDtypeStruct((M, D_FF), BF16),
            jax.ShapeDtypeStruct((M // tm, 8, D_FF), F32),
            jax.ShapeDtypeStruct((M // tm, 8, D_FF), F32),
        ],
        scratch_shapes=[pltpu.VMEM((nf, 2, 8, tf), F32)],
        compiler_params=_params(("arbitrary", "arbitrary"), 48),
        name="up_gate",
    )(h2, w_up_ab, conv_w, conv_w, cb, cb)


def _down_kernel(g_ref, wd_ref, x1_ref, gfin_ref, y_ref, *, final_norm):
    y = x1_ref[...] + _dot(g_ref[...], wd_ref[...])
    y_ref[...] = _rmsnorm_rows(y, gfin_ref[...]) if final_norm else y


def _down(g, x1, w_down, g_final, *, tm, final_norm):
    M = g.shape[0]
    return pl.pallas_call(
        functools.partial(_down_kernel, final_norm=final_norm),
        grid=(M // tm,),
        in_specs=[
            pl.BlockSpec((tm, D_FF), lambda m: (m, 0)),
            pl.BlockSpec((D_FF, D_MODEL), lambda m: (0, 0), pipeline_mode=pl.Buffered(1)),
            pl.BlockSpec((tm, D_MODEL), lambda m: (m, 0)),
            pl.BlockSpec((1, D_MODEL), lambda m: (0, 0)),
        ],
        out_specs=pl.BlockSpec((tm, D_MODEL), lambda m: (m, 0)),
        out_shape=jax.ShapeDtypeStruct((M, D_MODEL), F32),
        compiler_params=_params(("arbitrary",), 48),
        name="down",
    )(g, w_down, x1, g_final.reshape(1, D_MODEL))


def _ffn_sample_kernel(h_ref, wu_ref, h1a_ref, h1b_ref, h2a_ref, h2b_ref, cwa_ref, cwb_ref,
                       cba_ref, cbb_ref, wd_ref, x1_ref, gfin_ref, y_ref, ua_ref, ub_ref, *, T, final_norm):
    f = pl.program_id(0)
    t = lax.broadcasted_iota(jnp.int32, ua_ref.shape, 0) & (T - 1)
    u2 = _dot(h_ref[...], wu_ref[...])
    nchunk = ua_ref.shape[1] // UP_CHUNK
    half = lambda k: jnp.concatenate(
        [u2[:, (2 * c + k) * UP_CHUNK:(2 * c + k + 1) * UP_CHUNK] for c in range(nchunk)], axis=-1)

    def conv(u, hal1_ref, hal2_ref, w_ref, b_ref, u_out_ref):
        u_out_ref[...] = u
        r1 = jnp.where(t >= 1, pltpu.roll(u, 1, axis=0), hal1_ref[...])
        r2 = jnp.where(t >= 2, pltpu.roll(u, 2, axis=0), hal2_ref[...])
        return _conv3(u, r1, r2, w_ref, b_ref)

    a = conv(half(0), h1a_ref, h2a_ref, cwa_ref, cba_ref, ua_ref)
    b = conv(half(1), h1b_ref, h2b_ref, cwb_ref, cbb_ref, ub_ref)
    part = _dot((_gelu(a) * b).astype(BF16), wd_ref[...])

    @pl.when(f == 0)
    def _():
        y_ref[...] = x1_ref[...] + part

    @pl.when(f > 0)
    def _():
        y_ref[...] += part

    if final_norm:
        @pl.when(f == pl.num_programs(0) - 1)
        def _():
            y_ref[...] = _rmsnorm_rows(y_ref[...], gfin_ref[...])


def _ffn_sample(h2, x1, state_conv, w_up_ab, conv_w, conv_b, w_down, g_final, T, final_norm):
    M = h2.shape[0]
    nb = M // T
    tf = 512
    nf = D_FF // tf
    F2 = 2 * D_FF
    sc = state_conv.astype(F32)
    hal1 = jnp.concatenate([sc[:, 1:2], jnp.zeros((nb, T - 1, F2), F32)], axis=1).reshape(M, F2)
    hal2 = jnp.concatenate([sc[:, 0:2], jnp.zeros((nb, T - 2, F2), F32)], axis=1).reshape(M, F2)
    cb = conv_b.reshape(1, F2)
    gfin = g_final.reshape(1, D_MODEL)
    full = lambda w: pl.BlockSpec((M, w), lambda f: (0, 0))
    ca = lambda r, w=tf: pl.BlockSpec((r, w), lambda f: (0, f))
    cbk = lambda r, w=tf: pl.BlockSpec((r, w), lambda f: (0, f + nf))
    y, ua, ub = pl.pallas_call(
        functools.partial(_ffn_sample_kernel, T=T, final_norm=final_norm),
        grid=(nf,),
        in_specs=[
            full(D_MODEL), ca(D_MODEL, 2 * tf), ca(M), cbk(M), ca(M), cbk(M),
            ca(CONV_W), cbk(CONV_W), ca(1), cbk(1),
            pl.BlockSpec((tf, D_MODEL), lambda f: (f, 0)),
            full(D_MODEL), pl.BlockSpec((1, D_MODEL), lambda f: (0, 0)),
        ],
        out_specs=[full(D_MODEL), ca(M), ca(M)],
        out_shape=[
            jax.ShapeDtypeStruct((M, D_MODEL), F32),
            jax.ShapeDtypeStruct((M, D_FF), F32),
            jax.ShapeDtypeStruct((M, D_FF), F32),
        ],
        compiler_params=_params(("arbitrary",), 48),
        name="ffn_sample",
    )(h2, w_up_ab, hal1, hal1, hal2, hal2, conv_w, conv_w, cb, cb, w_down, x1, gfin)
    u = jnp.concatenate([ua, ub], axis=-1).reshape(nb, T, F2)
    return y, u[:, T - (CONV_W - 1):]


def kernel(x_prompt, x_sample, state_ret, cache_kv_w128, cache_kv_w512, cache_kv_w2048, state_conv, g_mix, w_in,
           gn_ret, w_ret_o, w_att_o, w_o, g_ffn, w_up, conv_w, conv_b, w_down, g_final):
    B, S, _ = x_prompt.shape
    NB, T, _ = x_sample.shape
    depth = w_in.shape[0]
    caches = (cache_kv_w128, cache_kv_w512, cache_kv_w2048)
    F2 = 2 * D_FF
    TM = 1024
    tiles_per_seq = S // TM

    xp = x_prompt.reshape(B * S, D_MODEL)
    xs = x_sample.reshape(NB * T, D_MODEL)
    ret_p, ret_s, conv_p, conv_s = [], [], [], []
    kv_p = [[] for _ in range(N_GROUPS)]
    kv_s = [[] for _ in range(N_GROUPS)]
    for l in range(depth):
        ret_tiles = 4 * RET_WIDTH // CAST_TN
        att_tiles = ATT_COLS // CAST_TN
        in_tiles = IN_COLS // CAST_TN
        w_main_l = _cast_cols(w_in[l], lambda n: (n + ret_tiles + att_tiles) % in_tiles, MAIN_COLS // CAST_TN,
                              CAST_TN, "cast_w_main")
        w_att_l = _cast_cols(w_in[l], lambda n: n + ret_tiles, att_tiles, CAST_TN, "cast_w_att")
        half_chunks = D_FF // UP_CHUNK
        w_up_l = _cast_cols(w_up[l], lambda n: n // 2 + (n % 2) * half_chunks, 2 * half_chunks, UP_CHUNK,
                            "cast_w_up")
        w_ret_o_l = w_ret_o[l].astype(BF16)
        w_att_o_l = w_att_o[l].astype(BF16)
        w_o_l = w_o[l].astype(BF16)
        w_down_l = w_down[l].astype(BF16)

        proj, h = _inproj_main(xp, g_mix[l], w_main_l, tm=TM)
        qkv0, qkv1, qkv2, kvf = _inproj_att(h, w_att_l, B=B, S=S, tm=512)
        yret, st = _ret_prompt(proj.reshape(B, S, MAIN_COLS), gn_ret[l])
        ret_p.append(st)
        os_, lses = [], []
        for g, qkv in enumerate((qkv0, qkv1, qkv2)):
            o, lse = _att_prompt(qkv, g)
            os_.append(o)
            lses.append(lse)
        kvf3 = kvf.reshape(B, S, 2 * N_GROUPS * ATT_GW)
        for g, (win, _) in enumerate(ATT_GROUPS):
            w = min(win, S)
            kv_p[g].append(kvf3[:, S - w:, 2 * ATT_GW * g:2 * ATT_GW * (g + 1)].reshape(B, w, 2, ATT_HG, ATT_HD))
        x1, h2 = _epilogue(xp, proj, yret.reshape(B * S, RET_WIDTH), os_, lses,
                           w_ret_o_l, w_att_o_l, w_o_l, g_ffn[l], tm=256, seq_len=S)
        gated, tail_a, tail_b = _up_gate(h2, w_up_l, conv_w[l], conv_b[l], tm=TM, tiles_per_seq=tiles_per_seq)
        utail = jnp.concatenate([tail_a, tail_b], axis=-1)
        conv_p.append(utail[tiles_per_seq - 1::tiles_per_seq, 8 - (CONV_W - 1):])
        xp = _down(gated, x1, w_down_l, g_final, tm=256, final_norm=l == depth - 1)

        proj_s = _inproj_sample(xs, g_mix[l], w_main_l, "inproj_sample_main")
        att_s = _inproj_sample(xs, g_mix[l], w_att_l, "inproj_sample_att")
        yret_s, st_s = _ret_sample(proj_s, state_ret[l], gn_ret[l], T)
        ret_s.append(st_s)
        os_, lses = [], []
        for g in range(N_GROUPS):
            o, lse, new_cache = _att_sample(att_s, caches[g][l], g, T)
            os_.append(o)
            lses.append(lse)
            kv_s[g].append(new_cache)
        x1s, h2s = _epilogue(xs, proj_s, yret_s, os_, lses, w_ret_o_l, w_att_o_l, w_o_l, g_ffn[l], tm=NB * T)
        xs, cv = _ffn_sample(h2s, x1s, state_conv[l], w_up_l, conv_w[l], conv_b[l], w_down_l,
                             g_final, T, final_norm=l == depth - 1)
        conv_s.append(cv)

    return (xp.reshape(B, S, D_MODEL), xs.reshape(NB, T, D_MODEL),
            jnp.stack(ret_p), jnp.stack(ret_s),
            jnp.stack(kv_p[0]), jnp.stack(kv_s[0]),
            jnp.stack(kv_p[1]), jnp.stack(kv_s[1]),
            jnp.stack(kv_p[2]), jnp.stack(kv_s[2]),
            jnp.stack(conv_p), jnp.stack(conv_s))
```

```python
import functools
import math

import jax
import jax.numpy as jnp
from jax import lax
from jax.experimental import pallas as pl
from jax.experimental.pallas import tpu as pltpu

D_MODEL = 2048
RET_HEADS = 8
RET_DK = 128
RET_DV = 128
RET_WIDTH = RET_HEADS * RET_DV
RET_CHUNK = 128
ATT_GROUPS = ((128, 1), (512, 4), (2048, 16))
N_GROUPS = 3
ATT_HG = 4
ATT_HD = 128
ATT_GW = ATT_HG * ATT_HD
ATT_SPAN = 128
ATT_BLK = 128
ALIBI_MAX = 8.0
D_FF = 5632
CONV_W = 3
EPS = 1e-6
NEG = -1e30
IN_COLS = 4 * RET_WIDTH + 3 * N_GROUPS * ATT_GW + 2 * D_MODEL

GATE_COLS = 2 * D_MODEL
RET_COL0 = GATE_COLS
MAIN_COLS = GATE_COLS + 4 * RET_WIDTH
ATT_COLS = 3 * N_GROUPS * ATT_GW
CAST_TN = 512

LANES = 128
MIB = 1024 * 1024
BF16 = jnp.bfloat16
F32 = jnp.float32


def _params(semantics, vmem_mib, flags=None):
    return pltpu.CompilerParams(dimension_semantics=semantics, vmem_limit_bytes=vmem_mib * MIB, flags=flags)


def _dot(a, b):
    return jnp.dot(a, b, preferred_element_type=F32)


def _dot_nt(a, b):
    return lax.dot_general(a, b, (((1,), (1,)), ((), ())), preferred_element_type=F32)


def _rmsnorm_rows(x, g):
    ms = jnp.mean(x * x, axis=-1, keepdims=True)
    return x * lax.rsqrt(ms + EPS) * g


def _cast_cols_kernel(w_ref, o_ref):
    o_ref[...] = w_ref[...].astype(o_ref.dtype)


def _cast_cols(w, src_tile, n_tiles, tn, name):
    K = w.shape[0]
    return pl.pallas_call(
        _cast_cols_kernel,
        grid=(n_tiles,),
        in_specs=[pl.BlockSpec((K, tn), lambda n: (0, src_tile(n)))],
        out_specs=pl.BlockSpec((K, tn), lambda n: (0, n)),
        out_shape=jax.ShapeDtypeStruct((K, n_tiles * tn), BF16),
        compiler_params=_params(("arbitrary",), 32),
        name=name,
    )(w)


def _inproj_sample_kernel(x_ref, g_ref, w_ref, proj_ref, h_scr):
    @pl.when(pl.program_id(0) == 0)
    def _():
        h_scr[...] = _rmsnorm_rows(x_ref[...], g_ref[...]).astype(BF16)

    proj_ref[...] = _dot(h_scr[...], w_ref[...])


def _inproj_sample(x2d, g, w_bf16, name):
    M = x2d.shape[0]
    N = w_bf16.shape[1]
    tn = 512
    return pl.pallas_call(
        _inproj_sample_kernel,
        grid=(N // tn,),
        in_specs=[
            pl.BlockSpec((M, D_MODEL), lambda n: (0, 0)),
            pl.BlockSpec((1, D_MODEL), lambda n: (0, 0)),
            pl.BlockSpec((D_MODEL, tn), lambda n: (0, n)),
        ],
        out_specs=pl.BlockSpec((M, tn), lambda n: (0, n)),
        out_shape=jax.ShapeDtypeStruct((M, N), F32),
        scratch_shapes=[pltpu.VMEM((M, D_MODEL), BF16)],
        compiler_params=_params(("arbitrary",), 32),
        name=name,
    )(x2d, g.reshape(1, D_MODEL), w_bf16)


def _inproj_main_kernel(x_ref, g_ref, w_ref, main_ref, h_ref):
    @pl.when(pl.program_id(1) == 0)
    def _():
        h_ref[...] = _rmsnorm_rows(x_ref[...], g_ref[...]).astype(BF16)

    main_ref[...] = _dot(h_ref[...], w_ref[...]).astype(BF16)


def _inproj_main(x2d, g, w_main, *, tm):
    M = x2d.shape[0]
    N = w_main.shape[1]
    tn = 1024
    return pl.pallas_call(
        _inproj_main_kernel,
        grid=(M // tm, N // tn),
        in_specs=[
            pl.BlockSpec((tm, D_MODEL), lambda m, n: (m, 0)),
            pl.BlockSpec((1, D_MODEL), lambda m, n: (0, 0)),
            pl.BlockSpec((D_MODEL, tn), lambda m, n: (0, n)),
        ],
        out_specs=[pl.BlockSpec((tm, tn), lambda m, n: (m, n)), pl.BlockSpec((tm, D_MODEL), lambda m, n: (m, 0))],
        out_shape=[jax.ShapeDtypeStruct((M, N), BF16), jax.ShapeDtypeStruct((M, D_MODEL), BF16)],
        compiler_params=_params(("arbitrary", "arbitrary"), 48),
        name="inproj_main",
    )(x2d, g.reshape(1, D_MODEL), w_main)


def _inproj_att_kernel(h_ref, w_ref, a_ref, kv_ref, *scratch, dil):
    acc = _dot(h_ref[...], w_ref[...])
    kv_ref[...] = acc[:, ATT_GW:]
    if dil == 1:
        a_ref[0] = acc.astype(BF16)
    else:
        acc_scr, = scratch
        rows = acc_scr.shape[1] // dil
        for cb in range(acc_scr.shape[0]):
            lanes = slice(cb * LANES, (cb + 1) * LANES)
            acc_scr[cb] = acc[:, lanes]
            for r in range(dil):
                a_ref[r, :, lanes] = acc_scr[cb, pl.ds(r, rows, stride=dil), :].astype(BF16)


def _inproj_att(h, w_att, group, *, B, S, tm):
    M = h.shape[0]
    tn = 3 * ATT_GW
    tps = S // tm
    dil = ATT_GROUPS[group][1]
    return pl.pallas_call(
        functools.partial(_inproj_att_kernel, dil=dil),
        grid=(M // tm,),
        in_specs=[
            pl.BlockSpec((tm, D_MODEL), lambda m: (m, 0)),
            pl.BlockSpec((D_MODEL, tn), lambda m: (0, group)),
        ],
        out_specs=[
            pl.BlockSpec((None, dil, tm // dil, tn), lambda m: (m // tps, 0, m % tps, 0)),
            pl.BlockSpec((tm, 2 * ATT_GW), lambda m: (m, 0)),
        ],
        out_shape=[
            jax.ShapeDtypeStruct((B, dil, S // dil, tn), BF16),
            jax.ShapeDtypeStruct((M, 2 * ATT_GW), F32),
        ],
        scratch_shapes=[] if dil == 1 else [pltpu.VMEM((tn // LANES, tm, LANES), F32)],
        compiler_params=_params(("arbitrary",), 48),
        name=f"inproj_att_g{group}",
    )(h, w_att)


def _ret_tables(lq):
    lg = jnp.log(1.0 - jnp.exp2(-5.0 - jnp.arange(RET_HEADS, dtype=F32)))
    scale = RET_DK ** -0.5
    i = jnp.arange(lq, dtype=F32)
    j = jnp.arange(RET_CHUNK, dtype=F32)
    diff = i[:, None] - j[None, :]
    live = (diff >= 0) & (j[None, :] < lq)
    dm = jnp.where(live[None], jnp.exp(jnp.maximum(diff, 0.0)[None] * lg[:, None, None]), 0.0) * scale
    qd = jnp.exp((i[None, :] + 1.0) * lg[:, None])
    qd = jnp.broadcast_to(qd[:, :, None], (RET_HEADS, lq, RET_DV))
    kd = jnp.where(j[None, :] < lq, jnp.exp(jnp.maximum(lq - 1.0 - j, 0.0)[None, :] * lg[:, None]), 0.0) * scale
    kd = jnp.broadcast_to(kd[:, :, None], (RET_HEADS, RET_CHUNK, RET_DK))
    gl = jnp.exp(lq * lg)
    return dm.astype(F32), qd.astype(F32), kd.astype(F32), gl.astype(F32)


def _ret_head(q, k, v, st, dm, qd, kd, gl):
    s = _dot_nt(q, k) * dm
    inner = _dot(s.astype(BF16), v)
    cross = _dot(q, st.astype(BF16)) * qd
    kt = (k.astype(F32) * kd).T.astype(BF16)
    new_st = st * gl + _dot(kt, v)
    return inner + cross, new_st


def _groupnorm_gate(y, gn, rg):
    mu = jnp.mean(y, axis=-1, keepdims=True)
    yc = y - mu
    yn = yc * lax.rsqrt(jnp.mean(yc * yc, axis=-1, keepdims=True) + EPS)
    return (rg * jax.nn.sigmoid(rg)) * (yn * gn)


def _ret_prompt_kernel(gl_ref, q_ref, k_ref, v_ref, rg_ref, dm_ref, qd_ref, kd_ref, gn_ref, y_ref, st_ref):
    c = pl.program_id(1)

    @pl.when(c == 0)
    def _():
        st_ref[...] = jnp.zeros_like(st_ref)

    for h in range(RET_HEADS):
        sl = slice(h * RET_DK, (h + 1) * RET_DK)
        y, new_st = _ret_head(q_ref[:, sl], k_ref[:, sl], v_ref[:, sl], st_ref[h],
                              dm_ref[h], qd_ref[h], kd_ref[h], gl_ref[h])
        st_ref[h] = new_st
        y_ref[:, sl] = _groupnorm_gate(y, gn_ref[:, sl], rg_ref[:, sl].astype(F32)).astype(y_ref.dtype)


def _ret_prompt(proj3, gn_ret):
    B, S, _ = proj3.shape
    L = RET_CHUNK
    dm, qd, kd, gl = _ret_tables(L)
    cb = RET_COL0 // RET_WIDTH
    col = lambda k: pl.BlockSpec((None, L, RET_WIDTH), lambda b, c, k=k: (b, c, cb + k))
    tab = pl.BlockSpec((RET_HEADS, L, RET_DK), lambda b, c: (0, 0, 0))
    return pl.pallas_call(
        _ret_prompt_kernel,
        grid=(B, S // L),
        in_specs=[
            pl.BlockSpec(memory_space=pltpu.SMEM),
            col(0), col(1), col(2), col(3), tab, tab, tab,
            pl.BlockSpec((1, RET_WIDTH), lambda b, c: (0, 0)),
        ],
        out_specs=[
            pl.BlockSpec((None, L, RET_WIDTH), lambda b, c: (b, c, 0)),
            pl.BlockSpec((None, RET_HEADS, RET_DK, RET_DV), lambda b, c: (b, 0, 0, 0)),
        ],
        out_shape=[
            jax.ShapeDtypeStruct((B, S, RET_WIDTH), BF16),
            jax.ShapeDtypeStruct((B, RET_HEADS, RET_DK, RET_DV), F32),
        ],
        compiler_params=_params(("arbitrary", "arbitrary"), 32),
        name="ret_prompt",
    )(gl, proj3, proj3, proj3, proj3, dm, qd, kd, gn_ret.reshape(1, RET_WIDTH))


def _pad_rows(x, rows):
    x = x.astype(F32)
    return jnp.concatenate([x, jnp.zeros((rows - x.shape[0], x.shape[1]), F32)], axis=0)


def _ret_sample_kernel(gl_ref, q_ref, k_ref, v_ref, rg_ref, st_in_ref, dm_ref, qd_ref, kd_ref, gn_ref,
                       y_ref, st_ref):
    for h in range(RET_HEADS):
        sl = slice(h * RET_DK, (h + 1) * RET_DK)
        k = _pad_rows(k_ref[:, sl], RET_CHUNK).astype(BF16)
        v = _pad_rows(v_ref[:, sl], RET_CHUNK).astype(BF16)
        y, new_st = _ret_head(q_ref[:, sl].astype(BF16), k, v, st_in_ref[h],
                              dm_ref[h], qd_ref[h], kd_ref[h], gl_ref[h])
        st_ref[h] = new_st
        y_ref[:, sl] = _groupnorm_gate(y, gn_ref[:, sl], rg_ref[:, sl].astype(F32)).astype(y_ref.dtype)


def _ret_sample(proj_s, state, gn_ret, T):
    nb = state.shape[0]
    dm, qd, kd, gl = _ret_tables(T)
    cb = RET_COL0 // RET_WIDTH
    col = lambda k: pl.BlockSpec((T, RET_WIDTH), lambda b, k=k: (b, cb + k))
    st_spec = pl.BlockSpec((None, RET_HEADS, RET_DK, RET_DV), lambda b: (b, 0, 0, 0))
    return pl.pallas_call(
        _ret_sample_kernel,
        grid=(nb,),
        in_specs=[
            pl.BlockSpec(memory_space=pltpu.SMEM),
            col(0), col(1), col(2), col(3), st_spec,
            pl.BlockSpec((RET_HEADS, T, RET_CHUNK), lambda b: (0, 0, 0)),
            pl.BlockSpec((RET_HEADS, T, RET_DV), lambda b: (0, 0, 0)),
            pl.BlockSpec((RET_HEADS, RET_CHUNK, RET_DK), lambda b: (0, 0, 0)),
            pl.BlockSpec((1, RET_WIDTH), lambda b: (0, 0)),
        ],
        out_specs=[pl.BlockSpec((T, RET_WIDTH), lambda b: (b, 0)), st_spec],
        out_shape=[
            jax.ShapeDtypeStruct((nb * T, RET_WIDTH), F32),
            jax.ShapeDtypeStruct(state.shape, F32),
        ],
        compiler_params=_params(("arbitrary",), 32),
        name="ret_sample",
    )(gl, proj_s, proj_s, proj_s, proj_s, state, dm, qd, kd, gn_ret.reshape(1, RET_WIDTH))


def _alibi_slope(g, h):
    n = N_GROUPS * ATT_HG
    return 2.0 ** (-ALIBI_MAX * (g * ATT_HG + h + 1) / n)


def _softmax_pv(scores, values):
    m = scores[0].max(axis=-1, keepdims=True)
    for s in scores[1:]:
        m = jnp.maximum(m, s.max(axis=-1, keepdims=True))
    l = None
    acc = None
    for s, v in zip(scores, values):
        p = jnp.exp(s - m)
        ls = p.sum(axis=-1, keepdims=True)
        a = _dot(p.astype(BF16), v)
        l = ls if l is None else l + ls
        acc = a if acc is None else acc + a
    return acc / l, m + jnp.log(l)


ATT_SUB = 4


def _att_prompt_kernel(q_ref, kp_ref, kc_ref, vp_ref, vc_ref, o_ref, lse_ref, k_scr, v_scr, bias_scr,
                       *, group, dil, nsub):
    j = pl.program_id(2)
    blk = ATT_BLK
    hd = ATT_HD
    k_scr[0:blk] = kp_ref[...]
    k_scr[blk:] = kc_ref[...]
    ones = jnp.ones((blk * (nsub + 1), hd), BF16)
    for h in range(ATT_HG):
        v_scr[0:blk, 2 * h * hd:(2 * h + 1) * hd] = vp_ref[:, h * hd:(h + 1) * hd]
        v_scr[blk:, 2 * h * hd:(2 * h + 1) * hd] = vc_ref[:, h * hd:(h + 1) * hd]
        v_scr[:, (2 * h + 1) * hd:(2 * h + 2) * hd] = ones

    r = lax.broadcasted_iota(jnp.int32, (blk, 2 * blk), 0)
    c = lax.broadcasted_iota(jnp.int32, (blk, 2 * blk), 1)
    dist = blk + r - c
    in_band = (dist >= 0) & (dist <= ATT_SPAN)
    for h in range(ATT_HG):
        bias_scr[h] = jnp.where(in_band, (-_alibi_slope(group, h) * dil) * dist.astype(F32), NEG)
    no_prev = jnp.where((c < blk) & (j == 0), NEG, 0.0)

    scale = hd ** -0.5
    for i in range(nsub):
        rows = slice(i * blk, (i + 1) * blk)
        keys = slice(i * blk, (i + 2) * blk)
        heads = range(ATT_HG)
        ss = [_dot_nt(q_ref[rows, h * hd:(h + 1) * hd], k_scr[keys, h * hd:(h + 1) * hd]) * scale + bias_scr[h]
              for h in heads]
        if i == 0:
            ss = [s + no_prev for s in ss]
        ms = [s.max(axis=-1, keepdims=True) for s in ss]
        ps = [jnp.exp(s - m).astype(BF16) for s, m in zip(ss, ms)]
        rs = [_dot(p, v_scr[keys, 2 * h * hd:(2 * h + 2) * hd]) for h, p in zip(heads, ps)]
        for h, res, m in zip(heads, rs, ms):
            l = res[:, hd:]
            o_ref[rows, h * hd:(h + 1) * hd] = (res[:, :hd] / l).astype(o_ref.dtype)
            lse_ref[rows, h * hd:(h + 1) * hd] = m + jnp.log(l)


def _att_prompt(qkv, group):
    B, dil, L, _ = qkv.shape
    nsub = math.gcd(ATT_SUB, L // ATT_BLK)
    rows = nsub * ATT_BLK
    cur = lambda which: pl.BlockSpec((None, None, rows, ATT_GW), lambda b, r, j: (b, r, j, which))
    prev = lambda which: pl.BlockSpec(
        (None, None, ATT_BLK, ATT_GW), lambda b, r, j: (b, r, jnp.maximum(j * nsub - 1, 0), which))
    out_spec = pl.BlockSpec((None, None, rows, ATT_GW), lambda b, r, j: (b, r, j, 0))
    return pl.pallas_call(
        functools.partial(_att_prompt_kernel, group=group, dil=dil, nsub=nsub),
        grid=(B, dil, L // rows),
        in_specs=[cur(0), prev(1), cur(1), prev(2), cur(2)],
        out_specs=[out_spec, out_spec],
        out_shape=[
            jax.ShapeDtypeStruct((B, dil, L, ATT_GW), BF16),
            jax.ShapeDtypeStruct((B, dil, L, ATT_GW), F32),
        ],
        scratch_shapes=[
            pltpu.VMEM((rows + ATT_BLK, ATT_GW), BF16),
            pltpu.VMEM((rows + ATT_BLK, 2 * ATT_GW), BF16),
            pltpu.VMEM((ATT_HG, ATT_BLK, 2 * ATT_BLK), F32),
        ],
        compiler_params=_params(("arbitrary", "arbitrary", "arbitrary"), 32),
        name=f"att_prompt_g{group}",
    )(qkv, qkv, qkv, qkv, qkv)


def _att_sample_kernel(q_ref, k_ref, v_ref, c_ref, o_ref, lse_ref, nc_ref, *, group, dil, wb, T):
    slots = 2 * ATT_HG
    nc_ref[pl.ds(0, (wb - T) * slots), :] = c_ref[pl.ds(T * slots, (wb - T) * slots), :]
    t = lax.broadcasted_iota(jnp.int32, (T, wb), 0)
    c = lax.broadcasted_iota(jnp.int32, (T, wb), 1)
    dist_c = wb + t - c
    valid_c = ((dist_c & (dil - 1)) == 0) & (dist_c <= ATT_SPAN * dil)
    tn = lax.broadcasted_iota(jnp.int32, (T, ATT_BLK), 0)
    cn = lax.broadcasted_iota(jnp.int32, (T, ATT_BLK), 1)
    dist_n = tn - cn
    valid_n = (dist_n >= 0) & ((dist_n & (dil - 1)) == 0) & (cn < T)
    scale = ATT_HD ** -0.5
    for h in range(ATT_HG):
        sl = slice(h * ATT_HD, (h + 1) * ATT_HD)
        slope = _alibi_slope(group, h)
        q = q_ref[:, sl].astype(BF16)
        kc = c_ref[pl.ds(h, wb, stride=slots), :].astype(BF16)
        vc = c_ref[pl.ds(ATT_HG + h, wb, stride=slots), :].astype(BF16)
        nc_ref[pl.ds((wb - T) * slots + h, T, stride=slots), :] = k_ref[:, sl]
        nc_ref[pl.ds((wb - T) * slots + ATT_HG + h, T, stride=slots), :] = v_ref[:, sl]
        kn = _pad_rows(k_ref[:, sl], ATT_BLK).astype(BF16)
        vn = _pad_rows(v_ref[:, sl], ATT_BLK).astype(BF16)
        sc = _dot_nt(q, kc) * scale - slope * dist_c.astype(F32)
        sn = _dot_nt(q, kn) * scale - slope * dist_n.astype(F32)
        sc = jnp.where(valid_c, sc, NEG)
        sn = jnp.where(valid_n, sn, NEG)
        o, lse = _softmax_pv([sc, sn], [vc, vn])
        o_ref[:, sl] = o.astype(o_ref.dtype)
        lse_ref[:, sl] = jnp.broadcast_to(lse, (T, ATT_HD))


def _att_sample(proj_s, cache, group, T):
    nb, wb = cache.shape[0], cache.shape[1]
    assert wb > T
    _, dil = ATT_GROUPS[group]
    c0 = 3 * group
    col = lambda k: pl.BlockSpec((T, ATT_GW), lambda b, k=k: (b, c0 + k))
    out_spec = pl.BlockSpec((T, ATT_GW), lambda b: (b, 0))
    cache_rows = cache.reshape(nb, wb * 2 * ATT_HG, ATT_HD)
    cache_spec = pl.BlockSpec((None, wb * 2 * ATT_HG, ATT_HD), lambda b: (b, 0, 0))
    o, lse, new_cache = pl.pallas_call(
        functools.partial(_att_sample_kernel, group=group, dil=dil, wb=wb, T=T),
        grid=(nb,),
        in_specs=[col(0), col(1), col(2), cache_spec],
        out_specs=[out_spec, out_spec, cache_spec],
        out_shape=[
            jax.ShapeDtypeStruct((nb * T, ATT_GW), F32),
            jax.ShapeDtypeStruct((nb * T, ATT_GW), F32),
            jax.ShapeDtypeStruct(cache_rows.shape, F32),
        ],
        compiler_params=_params(("arbitrary",), 48),
        name=f"att_sample_g{group}",
    )(proj_s, proj_s, proj_s, cache_rows)
    return o, lse, new_cache.reshape(cache.shape)


def _epilogue_kernel(x_ref, gr_ref, ga_ref, yret_ref, o0_ref, o1_ref, o2_ref, l0_ref, l1_ref, l2_ref,
                     wr_ref, wa_ref, wo_ref, gf_ref, x1_ref, h2_ref, *scratch, by_residue):
    def position_order(ref, scr):
        dil = ref.shape[0]
        if dil == 1:
            return ref[0].astype(F32)
        for r in range(dil):
            slab = ref[r].astype(F32)
            for cb in range(scr.shape[0]):
                scr[cb, pl.ds(r, ref.shape[1], stride=dil), :] = slab[:, cb * LANES:(cb + 1) * LANES]
        return jnp.concatenate([scr[cb] for cb in range(scr.shape[0])], axis=-1)

    if by_residue:
        o0, o1, o2 = (position_order(ref, scr) for ref, scr in zip((o0_ref, o1_ref, o2_ref), scratch[0:3]))
        l0, l1, l2 = (position_order(ref, scr) for ref, scr in zip((l0_ref, l1_ref, l2_ref), scratch[3:6]))
    else:
        o0, o1, o2 = o0_ref[...], o1_ref[...], o2_ref[...]
        l0, l1, l2 = l0_ref[...], l1_ref[...], l2_ref[...]
    m = jnp.maximum(jnp.maximum(l0, l1), l2)
    e0, e1, e2 = jnp.exp(l0 - m), jnp.exp(l1 - m), jnp.exp(l2 - m)
    inv = 1.0 / (e0 + e1 + e2)
    o = (e0 * inv) * o0 + (e1 * inv) * o1 + (e2 * inv) * o2
    att = _dot(o.astype(BF16), wa_ref[...])
    ret = _dot(yret_ref[...].astype(BF16), wr_ref[...])
    merged = (jax.nn.sigmoid(gr_ref[...].astype(F32)) * ret + jax.nn.sigmoid(ga_ref[...].astype(F32)) * att)
    x1 = x_ref[...] + _dot(merged.astype(BF16), wo_ref[...])
    x1_ref[...] = x1
    h2_ref[...] = _rmsnorm_rows(x1, gf_ref[...]).astype(BF16)


def _epilogue(x2d, proj2, yret, os_, lses, w_ret_o, w_att_o, w_o, g_ffn, *, tm, seq_len=None):
    M = x2d.shape[0]
    row = lambda w: pl.BlockSpec((tm, w), lambda m: (m, 0))
    const = lambda a: pl.BlockSpec(a.shape, lambda m: (0, 0), pipeline_mode=pl.Buffered(1))
    gf = g_ffn.reshape(1, D_MODEL)
    by_residue = seq_len is not None
    scratch = []
    if by_residue:
        tps = seq_len // tm
        att_specs = [pl.BlockSpec((None, a.shape[1], tm // a.shape[1], ATT_GW), lambda m: (m // tps, 0, m % tps, 0))
                     for a in (*os_, *lses)]
        scratch = [pltpu.VMEM((ATT_GW // LANES, tm, LANES), F32)] * 6
    else:
        att_specs = [row(ATT_GW)] * 6
    return pl.pallas_call(
        functools.partial(_epilogue_kernel, by_residue=by_residue),
        grid=(M // tm,),
        in_specs=[
            row(D_MODEL),
            pl.BlockSpec((tm, D_MODEL), lambda m: (m, 0)),
            pl.BlockSpec((tm, D_MODEL), lambda m: (m, 1)),
            row(RET_WIDTH), *att_specs,
            const(w_ret_o), const(w_att_o), const(w_o), const(gf),
        ],
        out_specs=[row(D_MODEL), row(D_MODEL)],
        out_shape=[jax.ShapeDtypeStruct((M, D_MODEL), F32), jax.ShapeDtypeStruct((M, D_MODEL), BF16)],
        scratch_shapes=scratch,
        compiler_params=_params(("arbitrary",), 56),
        name="epilogue",
    )(x2d, proj2, proj2, yret, *os_, *lses, w_ret_o, w_att_o, w_o, gf)


def _gelu(a):
    return 0.5 * a * (1.0 + lax.erf(a * math.sqrt(0.5)))


def _conv3(u, r1, r2, w_ref, b_ref):
    return b_ref[...] + (r2 * w_ref[0:1, :] + r1 * w_ref[1:2, :] + u * w_ref[2:3, :])


UP_CHUNK = 256


def _up_gate_kernel(h_ref, w_ref, cwa_ref, cwb_ref, cba_ref, cbb_ref, g_ref, ta_ref, tb_ref, u_scr, carry_scr,
                    *, tiles_per_seq):
    m = pl.program_id(0)
    f = pl.program_id(1)
    tm = h_ref.shape[0]

    @pl.when(m % tiles_per_seq == 0)
    def _():
        carry_scr[f] = jnp.zeros(carry_scr.shape[1:], F32)

    u_scr[0:8, :] = carry_scr[f]
    u_scr[8:, :] = _dot(h_ref[...], w_ref[...])
    carry_scr[f] = u_scr[tm:, :]
    for c in range(g_ref.shape[1] // UP_CHUNK):
        cols = slice(c * UP_CHUNK, (c + 1) * UP_CHUNK)

        def conv(half, cw_ref, cb_ref, tail_ref):
            pcols = slice((2 * c + half) * UP_CHUNK, (2 * c + half + 1) * UP_CHUNK)
            tail_ref[:, cols] = u_scr[tm:, pcols]
            u, r1, r2 = u_scr[8:, pcols], u_scr[7:tm + 7, pcols], u_scr[6:tm + 6, pcols]
            return cb_ref[:, cols] + (r2 * cw_ref[0:1, cols] + r1 * cw_ref[1:2, cols] + u * cw_ref[2:3, cols])

        a = conv(0, cwa_ref, cba_ref, ta_ref)
        b = conv(1, cwb_ref, cbb_ref, tb_ref)
        g_ref[:, cols] = (_gelu(a) * b).astype(BF16)


def _up_gate(h2, w_up_ab, conv_w, conv_b, *, tm, tiles_per_seq):
    M = h2.shape[0]
    tf = 512
    nf = D_FF // tf
    cb = conv_b.reshape(1, 2 * D_FF)
    return pl.pallas_call(
        functools.partial(_up_gate_kernel, tiles_per_seq=tiles_per_seq),
        grid=(M // tm, nf),
        in_specs=[
            pl.BlockSpec((tm, D_MODEL), lambda m, f: (m, 0)),
            pl.BlockSpec((D_MODEL, 2 * tf), lambda m, f: (0, f)),
            pl.BlockSpec((CONV_W, tf), lambda m, f: (0, f)),
            pl.BlockSpec((CONV_W, tf), lambda m, f: (0, f + nf)),
            pl.BlockSpec((1, tf), lambda m, f: (0, f)),
            pl.BlockSpec((1, tf), lambda m, f: (0, f + nf)),
        ],
        out_specs=[
            pl.BlockSpec((tm, tf), lambda m, f: (m, f)),
            pl.BlockSpec((None, 8, tf), lambda m, f: (m, 0, f)),
            pl.BlockSpec((None, 8, tf), lambda m, f: (m, 0, f)),
        ],
        out_shape=[
            jax.ShapeDtypeStruct((M, D_FF), BF16),
            jax.ShapeDtypeStruct((M // tm, 8, D_FF), F32),
            jax.ShapeDtypeStruct((M // tm, 8, D_FF), F32),
        ],
        scratch_shapes=[pltpu.VMEM((tm + 8, 2 * tf), F32), pltpu.VMEM((nf, 8, 2 * tf), F32)],
        compiler_params=_params(("arbitrary", "arbitrary"), 48),
        name="up_gate",
    )(h2, w_up_ab, conv_w, conv_w, cb, cb)


def _down_kernel(g_ref, wd_ref, x1_ref, gfin_ref, y_ref, *, final_norm):
    y = x1_ref[...] + _dot(g_ref[...], wd_ref[...])
    y_ref[...] = _rmsnorm_rows(y, gfin_ref[...]) if final_norm else y


def _down(g, x1, w_down, g_final, *, tm, final_norm):
    M = g.shape[0]
    return pl.pallas_call(
        functools.partial(_down_kernel, final_norm=final_norm),
        grid=(M // tm,),
        in_specs=[
            pl.BlockSpec((tm, D_FF), lambda m: (m, 0)),
            pl.BlockSpec((D_FF, D_MODEL), lambda m: (0, 0), pipeline_mode=pl.Buffered(1)),
            pl.BlockSpec((tm, D_MODEL), lambda m: (m, 0)),
            pl.BlockSpec((1, D_MODEL), lambda m: (0, 0)),
        ],
        out_specs=pl.BlockSpec((tm, D_MODEL), lambda m: (m, 0)),
        out_shape=jax.ShapeDtypeStruct((M, D_MODEL), F32),
        compiler_params=_params(("arbitrary",), 48),
        name="down",
    )(g, w_down, x1, g_final.reshape(1, D_MODEL))


def _ffn_sample_kernel(h_ref, wu_ref, h1a_ref, h1b_ref, h2a_ref, h2b_ref, cwa_ref, cwb_ref,
                       cba_ref, cbb_ref, wd_ref, x1_ref, gfin_ref, y_ref, ua_ref, ub_ref, *, T, final_norm):
    f = pl.program_id(0)
    t = lax.broadcasted_iota(jnp.int32, ua_ref.shape, 0) & (T - 1)
    u2 = _dot(h_ref[...], wu_ref[...])
    nchunk = ua_ref.shape[1] // UP_CHUNK
    half = lambda k: jnp.concatenate(
        [u2[:, (2 * c + k) * UP_CHUNK:(2 * c + k + 1) * UP_CHUNK] for c in range(nchunk)], axis=-1)

    def conv(u, hal1_ref, hal2_ref, w_ref, b_ref, u_out_ref):
        u_out_ref[...] = u
        r1 = jnp.where(t >= 1, pltpu.roll(u, 1, axis=0), hal1_ref[...])
        r2 = jnp.where(t >= 2, pltpu.roll(u, 2, axis=0), hal2_ref[...])
        return _conv3(u, r1, r2, w_ref, b_ref)

    a = conv(half(0), h1a_ref, h2a_ref, cwa_ref, cba_ref, ua_ref)
    b = conv(half(1), h1b_ref, h2b_ref, cwb_ref, cbb_ref, ub_ref)
    part = _dot((_gelu(a) * b).astype(BF16), wd_ref[...])

    @pl.when(f == 0)
    def _():
        y_ref[...] = x1_ref[...] + part

    @pl.when(f > 0)
    def _():
        y_ref[...] += part

    if final_norm:
        @pl.when(f == pl.num_programs(0) - 1)
        def _():
            y_ref[...] = _rmsnorm_rows(y_ref[...], gfin_ref[...])


def _ffn_sample(h2, x1, state_conv, w_up_ab, conv_w, conv_b, w_down, g_final, T, final_norm):
    M = h2.shape[0]
    nb = M // T
    tf = 512
    nf = D_FF // tf
    F2 = 2 * D_FF
    sc = state_conv.astype(F32)
    hal1 = jnp.concatenate([sc[:, 1:2], jnp.zeros((nb, T - 1, F2), F32)], axis=1).reshape(M, F2)
    hal2 = jnp.concatenate([sc[:, 0:2], jnp.zeros((nb, T - 2, F2), F32)], axis=1).reshape(M, F2)
    cb = conv_b.reshape(1, F2)
    gfin = g_final.reshape(1, D_MODEL)
    full = lambda w: pl.BlockSpec((M, w), lambda f: (0, 0))
    ca = lambda r, w=tf: pl.BlockSpec((r, w), lambda f: (0, f))
    cbk = lambda r, w=tf: pl.BlockSpec((r, w), lambda f: (0, f + nf))
    y, ua, ub = pl.pallas_call(
        functools.partial(_ffn_sample_kernel, T=T, final_norm=final_norm),
        grid=(nf,),
        in_specs=[
            full(D_MODEL), ca(D_MODEL, 2 * tf), ca(M), cbk(M), ca(M), cbk(M),
            ca(CONV_W), cbk(CONV_W), ca(1), cbk(1),
            pl.BlockSpec((tf, D_MODEL), lambda f: (f, 0)),
            full(D_MODEL), pl.BlockSpec((1, D_MODEL), lambda f: (0, 0)),
        ],
        out_specs=[full(D_MODEL), ca(M), ca(M)],
        out_shape=[
            jax.ShapeDtypeStruct((M, D_MODEL), F32),
            jax.ShapeDtypeStruct((M, D_FF), F32),
            jax.ShapeDtypeStruct((M, D_FF), F32),
        ],
        compiler_params=_params(("arbitrary",), 48),
        name="ffn_sample",
    )(h2, w_up_ab, hal1, hal1, hal2, hal2, conv_w, conv_w, cb, cb, w_down, x1, gfin)
    u = jnp.concatenate([ua, ub], axis=-1).reshape(nb, T, F2)
    return y, u[:, T - (CONV_W - 1):]


def kernel(x_prompt, x_sample, state_ret, cache_kv_w128, cache_kv_w512, cache_kv_w2048, state_conv, g_mix, w_in,
           gn_ret, w_ret_o, w_att_o, w_o, g_ffn, w_up, conv_w, conv_b, w_down, g_final):
    B, S, _ = x_prompt.shape
    NB, T, _ = x_sample.shape
    depth = w_in.shape[0]
    caches = (cache_kv_w128, cache_kv_w512, cache_kv_w2048)
    F2 = 2 * D_FF
    TM = 1024
    tiles_per_seq = S // TM

    xp = x_prompt.reshape(B * S, D_MODEL)
    xs = x_sample.reshape(NB * T, D_MODEL)
    ret_p, ret_s, conv_p, conv_s = [], [], [], []
    kv_p = [[] for _ in range(N_GROUPS)]
    kv_s = [[] for _ in range(N_GROUPS)]
    for l in range(depth):
        ret_tiles = 4 * RET_WIDTH // CAST_TN
        att_tiles = ATT_COLS // CAST_TN
        in_tiles = IN_COLS // CAST_TN
        w_main_l = _cast_cols(w_in[l], lambda n: (n + ret_tiles + att_tiles) % in_tiles, MAIN_COLS // CAST_TN,
                              CAST_TN, "cast_w_main")
        w_att_l = _cast_cols(w_in[l], lambda n: n + ret_tiles, att_tiles, CAST_TN, "cast_w_att")
        half_chunks = D_FF // UP_CHUNK
        w_up_l = _cast_cols(w_up[l], lambda n: n // 2 + (n % 2) * half_chunks, 2 * half_chunks, UP_CHUNK,
                            "cast_w_up")
        w_ret_o_l = w_ret_o[l].astype(BF16)
        w_att_o_l = w_att_o[l].astype(BF16)
        w_o_l = w_o[l].astype(BF16)
        w_down_l = w_down[l].astype(BF16)

        proj, h = _inproj_main(xp, g_mix[l], w_main_l, tm=TM)
        yret, st = _ret_prompt(proj.reshape(B, S, MAIN_COLS), gn_ret[l])
        ret_p.append(st)
        os_, lses = [], []
        for g, (win, _) in enumerate(ATT_GROUPS):
            qkv, kvf = _inproj_att(h, w_att_l, g, B=B, S=S, tm=512)
            o, lse = _att_prompt(qkv, g)
            os_.append(o)
            lses.append(lse)
            w = min(win, S)
            kv_p[g].append(kvf.reshape(B, S, 2 * ATT_GW)[:, S - w:].reshape(B, w, 2, ATT_HG, ATT_HD))
        x1, h2 = _epilogue(xp, proj, yret.reshape(B * S, RET_WIDTH), os_, lses,
                           w_ret_o_l, w_att_o_l, w_o_l, g_ffn[l], tm=256, seq_len=S)
        gated, tail_a, tail_b = _up_gate(h2, w_up_l, conv_w[l], conv_b[l], tm=TM, tiles_per_seq=tiles_per_seq)
        utail = jnp.concatenate([tail_a, tail_b], axis=-1)
        conv_p.append(utail[tiles_per_seq - 1::tiles_per_seq, 8 - (CONV_W - 1):])
        xp = _down(gated, x1, w_down_l, g_final, tm=256, final_norm=l == depth - 1)

        proj_s = _inproj_sample(xs, g_mix[l], w_main_l, "inproj_sample_main")
        att_s = _inproj_sample(xs, g_mix[l], w_att_l, "inproj_sample_att")
        yret_s, st_s = _ret_sample(proj_s, state_ret[l], gn_ret[l], T)
        ret_s.append(st_s)
        os_, lses = [], []
        for g in range(N_GROUPS):
            o, lse, new_cache = _att_sample(att_s, caches[g][l], g, T)
            os_.append(o)
            lses.append(lse)
            kv_s[g].append(new_cache)
        x1s, h2s = _epilogue(xs, proj_s, yret_s, os_, lses, w_ret_o_l, w_att_o_l, w_o_l, g_ffn[l], tm=NB * T)
        xs, cv = _ffn_sample(h2s, x1s, state_conv[l], w_up_l, conv_w[l], conv_b[l], w_down_l,
                             g_final, T, final_norm=l == depth - 1)
        conv_s.append(cv)

    return (xp.reshape(B, S, D_MODEL), xs.reshape(NB, T, D_MODEL),
            jnp.stack(ret_p), jnp.stack(ret_s),
            jnp.stack(kv_p[0]), jnp.stack(kv_s[0]),
            jnp.stack(kv_p[1]), jnp.stack(kv_s[1]),
            jnp.stack(kv_p[2]), jnp.stack(kv_s[2]),
            jnp.stack(conv_p), jnp.stack(conv_s))
```

```python
import functools
import math

import jax
import jax.numpy as jnp
from jax import lax
from jax.experimental import pallas as pl
from jax.experimental.pallas import tpu as pltpu

D_MODEL = 2048
RET_HEADS = 8
RET_DK = 128
RET_DV = 128
RET_WIDTH = RET_HEADS * RET_DV
RET_CHUNK = 128
ATT_GROUPS = ((128, 1), (512, 4), (2048, 16))
N_GROUPS = 3
ATT_HG = 4
ATT_HD = 128
ATT_GW = ATT_HG * ATT_HD
ATT_SPAN = 128
ATT_BLK = 128
ALIBI_MAX = 8.0
D_FF = 5632
CONV_W = 3
EPS = 1e-6
NEG = -1e30
IN_COLS = 4 * RET_WIDTH + 3 * N_GROUPS * ATT_GW + 2 * D_MODEL

GATE_COLS = 2 * D_MODEL
RET_COL0 = GATE_COLS
MAIN_COLS = GATE_COLS + 4 * RET_WIDTH
ATT_COLS = 3 * N_GROUPS * ATT_GW
CAST_TN = 512

LANES = 128
MIB = 1024 * 1024
BF16 = jnp.bfloat16
F32 = jnp.float32


def _params(semantics, vmem_mib, flags=None):
    return pltpu.CompilerParams(dimension_semantics=semantics, vmem_limit_bytes=vmem_mib * MIB, flags=flags)


def _dot(a, b):
    return jnp.dot(a, b, preferred_element_type=F32)


def _dot_nt(a, b):
    return lax.dot_general(a, b, (((1,), (1,)), ((), ())), preferred_element_type=F32)


def _rmsnorm_rows(x, g):
    ms = jnp.mean(x * x, axis=-1, keepdims=True)
    return x * lax.rsqrt(ms + EPS) * g


def _cast_cols_kernel(w_ref, o_ref):
    o_ref[...] = w_ref[...].astype(o_ref.dtype)


def _cast_cols(w, src_tile, n_tiles, tn, name):
    K = w.shape[0]
    return pl.pallas_call(
        _cast_cols_kernel,
        grid=(n_tiles,),
        in_specs=[pl.BlockSpec((K, tn), lambda n: (0, src_tile(n)))],
        out_specs=pl.BlockSpec((K, tn), lambda n: (0, n)),
        out_shape=jax.ShapeDtypeStruct((K, n_tiles * tn), BF16),
        compiler_params=_params(("arbitrary",), 32),
        name=name,
    )(w)


def _inproj_sample_kernel(x_ref, g_ref, w_ref, proj_ref, h_scr):
    @pl.when(pl.program_id(0) == 0)
    def _():
        h_scr[...] = _rmsnorm_rows(x_ref[...], g_ref[...]).astype(BF16)

    proj_ref[...] = _dot(h_scr[...], w_ref[...])


def _inproj_sample(x2d, g, w_bf16, name):
    M = x2d.shape[0]
    N = w_bf16.shape[1]
    tn = 512
    return pl.pallas_call(
        _inproj_sample_kernel,
        grid=(N // tn,),
        in_specs=[
            pl.BlockSpec((M, D_MODEL), lambda n: (0, 0)),
            pl.BlockSpec((1, D_MODEL), lambda n: (0, 0)),
            pl.BlockSpec((D_MODEL, tn), lambda n: (0, n)),
        ],
        out_specs=pl.BlockSpec((M, tn), lambda n: (0, n)),
        out_shape=jax.ShapeDtypeStruct((M, N), F32),
        scratch_shapes=[pltpu.VMEM((M, D_MODEL), BF16)],
        compiler_params=_params(("arbitrary",), 32),
        name=name,
    )(x2d, g.reshape(1, D_MODEL), w_bf16)


def _inproj_main_kernel(x_ref, g_ref, w_ref, main_ref, h_ref):
    @pl.when(pl.program_id(1) == 0)
    def _():
        h_ref[...] = _rmsnorm_rows(x_ref[...], g_ref[...]).astype(BF16)

    main_ref[...] = _dot(h_ref[...], w_ref[...]).astype(BF16)


def _inproj_main(x2d, g, w_main, *, tm):
    M = x2d.shape[0]
    N = w_main.shape[1]
    tn = 1024
    return pl.pallas_call(
        _inproj_main_kernel,
        grid=(M // tm, N // tn),
        in_specs=[
            pl.BlockSpec((tm, D_MODEL), lambda m, n: (m, 0)),
            pl.BlockSpec((1, D_MODEL), lambda m, n: (0, 0)),
            pl.BlockSpec((D_MODEL, tn), lambda m, n: (0, n)),
        ],
        out_specs=[pl.BlockSpec((tm, tn), lambda m, n: (m, n)), pl.BlockSpec((tm, D_MODEL), lambda m, n: (m, 0))],
        out_shape=[jax.ShapeDtypeStruct((M, N), BF16), jax.ShapeDtypeStruct((M, D_MODEL), BF16)],
        compiler_params=_params(("arbitrary", "arbitrary"), 48),
        name="inproj_main",
    )(x2d, g.reshape(1, D_MODEL), w_main)


def _inproj_att_kernel(h_ref, w_ref, a_ref, kv_ref, *scratch, dil):
    acc = _dot(h_ref[...], w_ref[...])
    slots = 2 * ATT_HG
    rows = kv_ref.shape[0] // slots
    for slot in range(slots):
        cols = slice(ATT_GW + slot * ATT_HD, ATT_GW + (slot + 1) * ATT_HD)
        kv_ref[pl.ds(slot, rows, stride=slots), :] = acc[acc.shape[0] - rows:, cols]
    if dil == 1:
        a_ref[0] = acc.astype(BF16)
    else:
        acc_scr, = scratch
        rows = acc_scr.shape[1] // dil
        for cb in range(acc_scr.shape[0]):
            lanes = slice(cb * LANES, (cb + 1) * LANES)
            acc_scr[cb] = acc[:, lanes]
            for r in range(dil):
                a_ref[r, :, lanes] = acc_scr[cb, pl.ds(r, rows, stride=dil), :].astype(BF16)


def _inproj_att(h, w_att, group, *, B, S, tm):
    M = h.shape[0]
    tn = 3 * ATT_GW
    tps = S // tm
    win, dil = ATT_GROUPS[group]
    win = min(win, S)
    rows = min(win, tm)
    first = tps - win // rows
    slots = 2 * ATT_HG
    return pl.pallas_call(
        functools.partial(_inproj_att_kernel, dil=dil),
        grid=(M // tm,),
        in_specs=[
            pl.BlockSpec((tm, D_MODEL), lambda m: (m, 0)),
            pl.BlockSpec((D_MODEL, tn), lambda m: (0, group)),
        ],
        out_specs=[
            pl.BlockSpec((None, dil, tm // dil, tn), lambda m: (m // tps, 0, m % tps, 0)),
            pl.BlockSpec((None, rows * slots, ATT_HD), lambda m: (m // tps, jnp.maximum(m % tps - first, 0), 0)),
        ],
        out_shape=[
            jax.ShapeDtypeStruct((B, dil, S // dil, tn), BF16),
            jax.ShapeDtypeStruct((B, win * slots, ATT_HD), F32),
        ],
        scratch_shapes=[] if dil == 1 else [pltpu.VMEM((tn // LANES, tm, LANES), F32)],
        compiler_params=_params(("arbitrary",), 48),
        name=f"inproj_att_g{group}",
    )(h, w_att)


def _ret_tables(lq):
    lg = jnp.log(1.0 - jnp.exp2(-5.0 - jnp.arange(RET_HEADS, dtype=F32)))
    scale = RET_DK ** -0.5
    i = jnp.arange(lq, dtype=F32)
    j = jnp.arange(RET_CHUNK, dtype=F32)
    diff = i[:, None] - j[None, :]
    live = (diff >= 0) & (j[None, :] < lq)
    dm = jnp.where(live[None], jnp.exp(jnp.maximum(diff, 0.0)[None] * lg[:, None, None]), 0.0) * scale
    qd = jnp.exp((i[None, :] + 1.0) * lg[:, None])
    qd = jnp.broadcast_to(qd[:, :, None], (RET_HEADS, lq, RET_DV))
    kd = jnp.where(j[None, :] < lq, jnp.exp(jnp.maximum(lq - 1.0 - j, 0.0)[None, :] * lg[:, None]), 0.0) * scale
    kd = jnp.broadcast_to(kd[:, :, None], (RET_HEADS, RET_CHUNK, RET_DK))
    gl = jnp.exp(lq * lg)
    return dm.astype(F32), qd.astype(F32), kd.astype(F32), gl.astype(F32)


def _ret_heads(qs, ks, vs, sts, dms, qds, kds, gls):
    ss = [(_dot_nt(q, k) * dm).astype(BF16) for q, k, dm in zip(qs, ks, dms)]
    crosses = [_dot(q, st.astype(BF16)) for q, st in zip(qs, sts)]
    kts = [(k.astype(F32) * kd).T.astype(BF16) for k, kd in zip(ks, kds)]
    inners = [_dot(s, v) for s, v in zip(ss, vs)]
    updates = [_dot(kt, v) for kt, v in zip(kts, vs)]
    ys = [inner + cross * qd for inner, cross, qd in zip(inners, crosses, qds)]
    new_sts = [st * gl + upd for st, gl, upd in zip(sts, gls, updates)]
    return ys, new_sts


def _groupnorm_gates(ys, gns, rgs):
    mus = [jnp.mean(y, axis=-1, keepdims=True) for y in ys]
    ycs = [y - mu for y, mu in zip(ys, mus)]
    rstds = [lax.rsqrt(jnp.mean(yc * yc, axis=-1, keepdims=True) + EPS) for yc in ycs]
    return [(rg * jax.nn.sigmoid(rg)) * ((yc * rstd) * gn) for yc, rstd, gn, rg in zip(ycs, rstds, gns, rgs)]


def _head_slices():
    return [slice(h * RET_DK, (h + 1) * RET_DK) for h in range(RET_HEADS)]


def _ret_prompt_kernel(gl_ref, q_ref, k_ref, v_ref, rg_ref, dm_ref, qd_ref, kd_ref, gn_ref, y_ref, st_ref):
    c = pl.program_id(1)

    @pl.when(c == 0)
    def _():
        st_ref[...] = jnp.zeros_like(st_ref)

    sls = _head_slices()
    heads = range(RET_HEADS)
    ys, new_sts = _ret_heads([q_ref[:, sl] for sl in sls], [k_ref[:, sl] for sl in sls], [v_ref[:, sl] for sl in sls],
                             [st_ref[h] for h in heads], [dm_ref[h] for h in heads], [qd_ref[h] for h in heads],
                             [kd_ref[h] for h in heads], [gl_ref[h] for h in heads])
    for h in heads:
        st_ref[h] = new_sts[h]
    outs = _groupnorm_gates(ys, [gn_ref[:, sl] for sl in sls], [rg_ref[:, sl].astype(F32) for sl in sls])
    for sl, out in zip(sls, outs):
        y_ref[:, sl] = out.astype(y_ref.dtype)


def _ret_prompt(proj3, gn_ret):
    B, S, _ = proj3.shape
    L = RET_CHUNK
    dm, qd, kd, gl = _ret_tables(L)
    cb = RET_COL0 // RET_WIDTH
    col = lambda k: pl.BlockSpec((None, L, RET_WIDTH), lambda b, c, k=k: (b, c, cb + k))
    tab = pl.BlockSpec((RET_HEADS, L, RET_DK), lambda b, c: (0, 0, 0))
    return pl.pallas_call(
        _ret_prompt_kernel,
        grid=(B, S // L),
        in_specs=[
            pl.BlockSpec(memory_space=pltpu.SMEM),
            col(0), col(1), col(2), col(3), tab, tab, tab,
            pl.BlockSpec((1, RET_WIDTH), lambda b, c: (0, 0)),
        ],
        out_specs=[
            pl.BlockSpec((None, L, RET_WIDTH), lambda b, c: (b, c, 0)),
            pl.BlockSpec((None, RET_HEADS, RET_DK, RET_DV), lambda b, c: (b, 0, 0, 0)),
        ],
        out_shape=[
            jax.ShapeDtypeStruct((B, S, RET_WIDTH), BF16),
            jax.ShapeDtypeStruct((B, RET_HEADS, RET_DK, RET_DV), F32),
        ],
        compiler_params=_params(("arbitrary", "arbitrary"), 32),
        name="ret_prompt",
    )(gl, proj3, proj3, proj3, proj3, dm, qd, kd, gn_ret.reshape(1, RET_WIDTH))


def _pad_rows(x, rows):
    x = x.astype(F32)
    return jnp.concatenate([x, jnp.zeros((rows - x.shape[0], x.shape[1]), F32)], axis=0)


def _ret_sample_kernel(gl_ref, q_ref, k_ref, v_ref, rg_ref, st_in_ref, dm_ref, qd_ref, kd_ref, gn_ref,
                       y_ref, st_ref):
    sls = _head_slices()
    heads = range(RET_HEADS)
    ys, new_sts = _ret_heads([q_ref[:, sl].astype(BF16) for sl in sls],
                             [_pad_rows(k_ref[:, sl], RET_CHUNK).astype(BF16) for sl in sls],
                             [_pad_rows(v_ref[:, sl], RET_CHUNK).astype(BF16) for sl in sls],
                             [st_in_ref[h] for h in heads], [dm_ref[h] for h in heads], [qd_ref[h] for h in heads],
                             [kd_ref[h] for h in heads], [gl_ref[h] for h in heads])
    for h in heads:
        st_ref[h] = new_sts[h]
    outs = _groupnorm_gates(ys, [gn_ref[:, sl] for sl in sls], [rg_ref[:, sl].astype(F32) for sl in sls])
    for sl, out in zip(sls, outs):
        y_ref[:, sl] = out.astype(y_ref.dtype)


def _ret_sample(proj_s, state, gn_ret, T):
    nb = state.shape[0]
    dm, qd, kd, gl = _ret_tables(T)
    cb = RET_COL0 // RET_WIDTH
    col = lambda k: pl.BlockSpec((T, RET_WIDTH), lambda b, k=k: (b, cb + k))
    st_spec = pl.BlockSpec((None, RET_HEADS, RET_DK, RET_DV), lambda b: (b, 0, 0, 0))
    return pl.pallas_call(
        _ret_sample_kernel,
        grid=(nb,),
        in_specs=[
            pl.BlockSpec(memory_space=pltpu.SMEM),
            col(0), col(1), col(2), col(3), st_spec,
            pl.BlockSpec((RET_HEADS, T, RET_CHUNK), lambda b: (0, 0, 0)),
            pl.BlockSpec((RET_HEADS, T, RET_DV), lambda b: (0, 0, 0)),
            pl.BlockSpec((RET_HEADS, RET_CHUNK, RET_DK), lambda b: (0, 0, 0)),
            pl.BlockSpec((1, RET_WIDTH), lambda b: (0, 0)),
        ],
        out_specs=[pl.BlockSpec((T, RET_WIDTH), lambda b: (b, 0)), st_spec],
        out_shape=[
            jax.ShapeDtypeStruct((nb * T, RET_WIDTH), F32),
            jax.ShapeDtypeStruct(state.shape, F32),
        ],
        compiler_params=_params(("arbitrary",), 32),
        name="ret_sample",
    )(gl, proj_s, proj_s, proj_s, proj_s, state, dm, qd, kd, gn_ret.reshape(1, RET_WIDTH))


def _alibi_slope(g, h):
    n = N_GROUPS * ATT_HG
    return 2.0 ** (-ALIBI_MAX * (g * ATT_HG + h + 1) / n)


ATT_SUB = 4


def _att_prompt_kernel(q_ref, kp_ref, kc_ref, vp_ref, vc_ref, o_ref, lse_ref, k_scr, v_scr, bias_scr,
                       *, group, dil, nsub):
    j = pl.program_id(2)
    blk = ATT_BLK
    hd = ATT_HD
    k_scr[0:blk] = kp_ref[...]
    k_scr[blk:] = kc_ref[...]
    ones = jnp.ones((blk * (nsub + 1), hd), BF16)
    for h in range(ATT_HG):
        v_scr[0:blk, 2 * h * hd:(2 * h + 1) * hd] = vp_ref[:, h * hd:(h + 1) * hd]
        v_scr[blk:, 2 * h * hd:(2 * h + 1) * hd] = vc_ref[:, h * hd:(h + 1) * hd]
        v_scr[:, (2 * h + 1) * hd:(2 * h + 2) * hd] = ones

    r = lax.broadcasted_iota(jnp.int32, (blk, 2 * blk), 0)
    c = lax.broadcasted_iota(jnp.int32, (blk, 2 * blk), 1)
    dist = blk + r - c
    in_band = (dist >= 0) & (dist <= ATT_SPAN)
    for h in range(ATT_HG):
        bias_scr[h] = jnp.where(in_band, (-_alibi_slope(group, h) * dil) * dist.astype(F32), NEG)
    no_prev = jnp.where((c < blk) & (j == 0), NEG, 0.0)

    scale = hd ** -0.5
    for i in range(nsub):
        rows = slice(i * blk, (i + 1) * blk)
        keys = slice(i * blk, (i + 2) * blk)
        heads = range(ATT_HG)
        ss = [_dot_nt(q_ref[rows, h * hd:(h + 1) * hd], k_scr[keys, h * hd:(h + 1) * hd]) * scale + bias_scr[h]
              for h in heads]
        if i == 0:
            ss = [s + no_prev for s in ss]
        ms = [s.max(axis=-1, keepdims=True) for s in ss]
        ps = [jnp.exp(s - m).astype(BF16) for s, m in zip(ss, ms)]
        rs = [_dot(p, v_scr[keys, 2 * h * hd:(2 * h + 2) * hd]) for h, p in zip(heads, ps)]
        for h, res, m in zip(heads, rs, ms):
            l = res[:, hd:]
            o_ref[rows, h * hd:(h + 1) * hd] = (res[:, :hd] / l).astype(o_ref.dtype)
            lse_ref[rows, h * hd:(h + 1) * hd] = m + jnp.log(l)


def _att_prompt(qkv, group):
    B, dil, L, _ = qkv.shape
    nsub = math.gcd(ATT_SUB, L // ATT_BLK)
    rows = nsub * ATT_BLK
    cur = lambda which: pl.BlockSpec((None, None, rows, ATT_GW), lambda b, r, j: (b, r, j, which))
    prev = lambda which: pl.BlockSpec(
        (None, None, ATT_BLK, ATT_GW), lambda b, r, j: (b, r, jnp.maximum(j * nsub - 1, 0), which))
    out_spec = pl.BlockSpec((None, None, rows, ATT_GW), lambda b, r, j: (b, r, j, 0))
    return pl.pallas_call(
        functools.partial(_att_prompt_kernel, group=group, dil=dil, nsub=nsub),
        grid=(B, dil, L // rows),
        in_specs=[cur(0), prev(1), cur(1), prev(2), cur(2)],
        out_specs=[out_spec, out_spec],
        out_shape=[
            jax.ShapeDtypeStruct((B, dil, L, ATT_GW), BF16),
            jax.ShapeDtypeStruct((B, dil, L, ATT_GW), F32),
        ],
        scratch_shapes=[
            pltpu.VMEM((rows + ATT_BLK, ATT_GW), BF16),
            pltpu.VMEM((rows + ATT_BLK, 2 * ATT_GW), BF16),
            pltpu.VMEM((ATT_HG, ATT_BLK, 2 * ATT_BLK), F32),
        ],
        compiler_params=_params(("arbitrary", "arbitrary", "arbitrary"), 32),
        name=f"att_prompt_g{group}",
    )(qkv, qkv, qkv, qkv, qkv)


def _att_sample_kernel(q_ref, k_ref, v_ref, c_ref, o_ref, lse_ref, nc_ref, *, group, dil, wb, T):
    slots = 2 * ATT_HG
    nc_ref[pl.ds(0, (wb - T) * slots), :] = c_ref[pl.ds(T * slots, (wb - T) * slots), :]
    t = lax.broadcasted_iota(jnp.int32, (T, wb), 0)
    c = lax.broadcasted_iota(jnp.int32, (T, wb), 1)
    dist_c = wb + t - c
    valid_c = ((dist_c & (dil - 1)) == 0) & (dist_c <= ATT_SPAN * dil)
    tn = lax.broadcasted_iota(jnp.int32, (T, ATT_BLK), 0)
    cn = lax.broadcasted_iota(jnp.int32, (T, ATT_BLK), 1)
    dist_n = tn - cn
    valid_n = (dist_n >= 0) & ((dist_n & (dil - 1)) == 0) & (cn < T)
    scale = ATT_HD ** -0.5
    heads = range(ATT_HG)
    sls = [slice(h * ATT_HD, (h + 1) * ATT_HD) for h in heads]
    for h, sl in zip(heads, sls):
        nc_ref[pl.ds((wb - T) * slots + h, T, stride=slots), :] = k_ref[:, sl]
        nc_ref[pl.ds((wb - T) * slots + ATT_HG + h, T, stride=slots), :] = v_ref[:, sl]
    qs = [q_ref[:, sl].astype(BF16) for sl in sls]
    kcs = [c_ref[pl.ds(h, wb, stride=slots), :].astype(BF16) for h in heads]
    vcs = [c_ref[pl.ds(ATT_HG + h, wb, stride=slots), :].astype(BF16) for h in heads]
    kns = [_pad_rows(k_ref[:, sl], ATT_BLK).astype(BF16) for sl in sls]
    vns = [_pad_rows(v_ref[:, sl], ATT_BLK).astype(BF16) for sl in sls]
    bias_c = jnp.where(valid_c, dist_c.astype(F32), -NEG)
    bias_n = jnp.where(valid_n, dist_n.astype(F32), -NEG)
    scs = [_dot_nt(q, kc) * scale - _alibi_slope(group, h) * bias_c for h, q, kc in zip(heads, qs, kcs)]
    sns = [_dot_nt(q, kn) * scale - _alibi_slope(group, h) * bias_n for h, q, kn in zip(heads, qs, kns)]
    ms = [jnp.maximum(sc.max(axis=-1, keepdims=True), sn.max(axis=-1, keepdims=True)) for sc, sn in zip(scs, sns)]
    pcs = [jnp.exp(sc - m) for sc, m in zip(scs, ms)]
    pns = [jnp.exp(sn - m) for sn, m in zip(sns, ms)]
    ls = [pc.sum(axis=-1, keepdims=True) + pn.sum(axis=-1, keepdims=True) for pc, pn in zip(pcs, pns)]
    accs = [_dot(pc.astype(BF16), vc) + _dot(pn.astype(BF16), vn) for pc, pn, vc, vn in zip(pcs, pns, vcs, vns)]
    for sl, acc, l, m in zip(sls, accs, ls, ms):
        o_ref[:, sl] = (acc / l).astype(o_ref.dtype)
        lse_ref[:, sl] = jnp.broadcast_to(m + jnp.log(l), (T, ATT_HD))


def _att_sample(proj_s, cache, group, T):
    nb, wb = cache.shape[0], cache.shape[1]
    assert wb > T
    _, dil = ATT_GROUPS[group]
    c0 = 3 * group
    col = lambda k: pl.BlockSpec((T, ATT_GW), lambda b, k=k: (b, c0 + k))
    out_spec = pl.BlockSpec((T, ATT_GW), lambda b: (b, 0))
    cache_rows = cache.reshape(nb, wb * 2 * ATT_HG, ATT_HD)
    cache_spec = pl.BlockSpec((None, wb * 2 * ATT_HG, ATT_HD), lambda b: (b, 0, 0))
    o, lse, new_cache = pl.pallas_call(
        functools.partial(_att_sample_kernel, group=group, dil=dil, wb=wb, T=T),
        grid=(nb,),
        in_specs=[col(0), col(1), col(2), cache_spec],
        out_specs=[out_spec, out_spec, cache_spec],
        out_shape=[
            jax.ShapeDtypeStruct((nb * T, ATT_GW), F32),
            jax.ShapeDtypeStruct((nb * T, ATT_GW), F32),
            jax.ShapeDtypeStruct(cache_rows.shape, F32),
        ],
        compiler_params=_params(("arbitrary",), 48),
        name=f"att_sample_g{group}",
    )(proj_s, proj_s, proj_s, cache_rows)
    return o, lse, new_cache.reshape(cache.shape)


def _epilogue_kernel(x_ref, gr_ref, ga_ref, yret_ref, o0_ref, o1_ref, o2_ref, l0_ref, l1_ref, l2_ref,
                     wr_ref, wa_ref, wo_ref, gf_ref, x1_ref, h2_ref, *scratch, by_residue):
    def position_order(ref, scr):
        dil = ref.shape[0]
        if dil == 1:
            return ref[0].astype(F32)
        for r in range(dil):
            slab = ref[r].astype(F32)
            for cb in range(scr.shape[0]):
                scr[cb, pl.ds(r, ref.shape[1], stride=dil), :] = slab[:, cb * LANES:(cb + 1) * LANES]
        return jnp.concatenate([scr[cb] for cb in range(scr.shape[0])], axis=-1)

    if by_residue:
        o0, o1, o2 = (position_order(ref, scr) for ref, scr in zip((o0_ref, o1_ref, o2_ref), scratch[0:3]))
        l0, l1, l2 = (position_order(ref, scr) for ref, scr in zip((l0_ref, l1_ref, l2_ref), scratch[3:6]))
    else:
        o0, o1, o2 = o0_ref[...], o1_ref[...], o2_ref[...]
        l0, l1, l2 = l0_ref[...], l1_ref[...], l2_ref[...]
    m = jnp.maximum(jnp.maximum(l0, l1), l2)
    e0, e1, e2 = jnp.exp(l0 - m), jnp.exp(l1 - m), jnp.exp(l2 - m)
    inv = 1.0 / (e0 + e1 + e2)
    o = (e0 * inv) * o0 + (e1 * inv) * o1 + (e2 * inv) * o2
    att = _dot(o.astype(BF16), wa_ref[...])
    ret = _dot(yret_ref[...].astype(BF16), wr_ref[...])
    merged = (jax.nn.sigmoid(gr_ref[...].astype(F32)) * ret + jax.nn.sigmoid(ga_ref[...].astype(F32)) * att)
    x1 = x_ref[...] + _dot(merged.astype(BF16), wo_ref[...])
    x1_ref[...] = x1
    h2_ref[...] = _rmsnorm_rows(x1, gf_ref[...]).astype(BF16)


def _epilogue(x2d, proj2, yret, os_, lses, w_ret_o, w_att_o, w_o, g_ffn, *, tm, seq_len=None):
    M = x2d.shape[0]
    row = lambda w: pl.BlockSpec((tm, w), lambda m: (m, 0))
    const = lambda a: pl.BlockSpec(a.shape, lambda m: (0, 0), pipeline_mode=pl.Buffered(1))
    gf = g_ffn.reshape(1, D_MODEL)
    by_residue = seq_len is not None
    scratch = []
    if by_residue:
        tps = seq_len // tm
        att_specs = [pl.BlockSpec((None, a.shape[1], tm // a.shape[1], ATT_GW), lambda m: (m // tps, 0, m % tps, 0))
                     for a in (*os_, *lses)]
        scratch = [pltpu.VMEM((ATT_GW // LANES, tm, LANES), F32)] * 6
    else:
        att_specs = [row(ATT_GW)] * 6
    return pl.pallas_call(
        functools.partial(_epilogue_kernel, by_residue=by_residue),
        grid=(M // tm,),
        in_specs=[
            row(D_MODEL),
            pl.BlockSpec((tm, D_MODEL), lambda m: (m, 0)),
            pl.BlockSpec((tm, D_MODEL), lambda m: (m, 1)),
            row(RET_WIDTH), *att_specs,
            const(w_ret_o), const(w_att_o), const(w_o), const(gf),
        ],
        out_specs=[row(D_MODEL), row(D_MODEL)],
        out_shape=[jax.ShapeDtypeStruct((M, D_MODEL), F32), jax.ShapeDtypeStruct((M, D_MODEL), BF16)],
        scratch_shapes=scratch,
        compiler_params=_params(("arbitrary",), 56),
        name="epilogue",
    )(x2d, proj2, proj2, yret, *os_, *lses, w_ret_o, w_att_o, w_o, gf)


def _gelu_gate(a, half_b):
    return (a * (1.0 + lax.erf(a * math.sqrt(0.5)))) * half_b


def _halve_b_half(x):
    return jnp.concatenate([x[..., :D_FF], 0.5 * x[..., D_FF:]], axis=-1)


def _conv3(u, r1, r2, w_ref, b_ref):
    return b_ref[...] + (r2 * w_ref[0:1, :] + r1 * w_ref[1:2, :] + u * w_ref[2:3, :])


UP_CHUNK = 256


def _up_gate_kernel(h_ref, w_ref, cwa_ref, cwb_ref, cba_ref, cbb_ref, g_ref, ta_ref, tb_ref, u_scr, carry_scr,
                    *, tiles_per_seq):
    m = pl.program_id(0)
    f = pl.program_id(1)
    tm = h_ref.shape[0]

    @pl.when(m % tiles_per_seq == 0)
    def _():
        carry_scr[f] = jnp.zeros(carry_scr.shape[1:], F32)

    u_scr[0:8, :] = carry_scr[f]
    u_scr[8:, :] = _dot(h_ref[...], w_ref[...])
    carry_scr[f] = u_scr[tm:, :]
    for c in range(g_ref.shape[1] // UP_CHUNK):
        cols = slice(c * UP_CHUNK, (c + 1) * UP_CHUNK)

        def conv(half, cw_ref, cb_ref, tail_ref):
            pcols = slice((2 * c + half) * UP_CHUNK, (2 * c + half + 1) * UP_CHUNK)
            tail_ref[:, cols] = u_scr[tm:, pcols]
            u, r1, r2 = u_scr[8:, pcols], u_scr[7:tm + 7, pcols], u_scr[6:tm + 6, pcols]
            return cb_ref[:, cols] + (r2 * cw_ref[0:1, cols] + r1 * cw_ref[1:2, cols] + u * cw_ref[2:3, cols])

        a = conv(0, cwa_ref, cba_ref, ta_ref)
        b = conv(1, cwb_ref, cbb_ref, tb_ref)
        g_ref[:, cols] = _gelu_gate(a, b).astype(BF16)


def _up_gate(h2, w_up_ab, conv_w, conv_b, *, tm, tiles_per_seq):
    M = h2.shape[0]
    tf = 512
    nf = D_FF // tf
    cb = conv_b.reshape(1, 2 * D_FF)
    return pl.pallas_call(
        functools.partial(_up_gate_kernel, tiles_per_seq=tiles_per_seq),
        grid=(M // tm, nf),
        in_specs=[
            pl.BlockSpec((tm, D_MODEL), lambda m, f: (m, 0)),
            pl.BlockSpec((D_MODEL, 2 * tf), lambda m, f: (0, f)),
            pl.BlockSpec((CONV_W, tf), lambda m, f: (0, f)),
            pl.BlockSpec((CONV_W, tf), lambda m, f: (0, f + nf)),
            pl.BlockSpec((1, tf), lambda m, f: (0, f)),
            pl.BlockSpec((1, tf), lambda m, f: (0, f + nf)),
        ],
        out_specs=[
            pl.BlockSpec((tm, tf), lambda m, f: (m, f)),
            pl.BlockSpec((None, 8, tf), lambda m, f: (m, 0, f)),
            pl.BlockSpec((None, 8, tf), lambda m, f: (m, 0, f)),
        ],
        out_shape=[
            jax.ShapeDtypeStruct((M, D_FF), BF16),
            jax.ShapeDtypeStruct((M // tm, 8, D_FF), F32),
            jax.ShapeDtypeStruct((M // tm, 8, D_FF), F32),
        ],
        scratch_shapes=[pltpu.VMEM((tm + 8, 2 * tf), F32), pltpu.VMEM((nf, 8, 2 * tf), F32)],
        compiler_params=_params(("arbitrary", "arbitrary"), 48),
        name="up_gate",
    )(h2, w_up_ab, conv_w, conv_w, cb, cb)


def _down_kernel(g_ref, wd_ref, x1_ref, gfin_ref, y_ref, *, final_norm):
    y = x1_ref[...] + _dot(g_ref[...], wd_ref[...])
    y_ref[...] = _rmsnorm_rows(y, gfin_ref[...]) if final_norm else y


def _down(g, x1, w_down, g_final, *, tm, final_norm):
    M = g.shape[0]
    return pl.pallas_call(
        functools.partial(_down_kernel, final_norm=final_norm),
        grid=(M // tm,),
        in_specs=[
            pl.BlockSpec((tm, D_FF), lambda m: (m, 0)),
            pl.BlockSpec((D_FF, D_MODEL), lambda m: (0, 0), pipeline_mode=pl.Buffered(1)),
            pl.BlockSpec((tm, D_MODEL), lambda m: (m, 0)),
            pl.BlockSpec((1, D_MODEL), lambda m: (0, 0)),
        ],
        out_specs=pl.BlockSpec((tm, D_MODEL), lambda m: (m, 0)),
        out_shape=jax.ShapeDtypeStruct((M, D_MODEL), F32),
        compiler_params=_params(("arbitrary",), 48),
        name="down",
    )(g, w_down, x1, g_final.reshape(1, D_MODEL))


def _ffn_sample_kernel(h_ref, wu_ref, h1a_ref, h1b_ref, h2a_ref, h2b_ref, cwa_ref, cwb_ref,
                       cba_ref, cbb_ref, wd_ref, x1_ref, gfin_ref, y_ref, ua_ref, ub_ref, *, T, final_norm):
    f = pl.program_id(0)
    t = lax.broadcasted_iota(jnp.int32, ua_ref.shape, 0) & (T - 1)
    u2 = _dot(h_ref[...], wu_ref[...])
    nchunk = ua_ref.shape[1] // UP_CHUNK
    half = lambda k: jnp.concatenate(
        [u2[:, (2 * c + k) * UP_CHUNK:(2 * c + k + 1) * UP_CHUNK] for c in range(nchunk)], axis=-1)

    def conv(u, hal1_ref, hal2_ref, w_ref, b_ref, u_out_ref):
        u_out_ref[...] = u
        r1 = jnp.where(t >= 1, pltpu.roll(u, 1, axis=0), hal1_ref[...])
        r2 = jnp.where(t >= 2, pltpu.roll(u, 2, axis=0), hal2_ref[...])
        return _conv3(u, r1, r2, w_ref, b_ref)

    a = conv(half(0), h1a_ref, h2a_ref, cwa_ref, cba_ref, ua_ref)
    b = conv(half(1), h1b_ref, h2b_ref, cwb_ref, cbb_ref, ub_ref)
    part = _dot(_gelu_gate(a, b).astype(BF16), wd_ref[...])

    @pl.when(f == 0)
    def _():
        y_ref[...] = x1_ref[...] + part

    @pl.when(f > 0)
    def _():
        y_ref[...] += part

    if final_norm:
        @pl.when(f == pl.num_programs(0) - 1)
        def _():
            y_ref[...] = _rmsnorm_rows(y_ref[...], gfin_ref[...])


def _ffn_sample(h2, x1, state_conv, w_up_ab, conv_w, conv_b, w_down, g_final, T, final_norm):
    M = h2.shape[0]
    nb = M // T
    tf = 512
    nf = D_FF // tf
    F2 = 2 * D_FF
    sc = state_conv.astype(F32)
    hal1 = jnp.concatenate([sc[:, 1:2], jnp.zeros((nb, T - 1, F2), F32)], axis=1).reshape(M, F2)
    hal2 = jnp.concatenate([sc[:, 0:2], jnp.zeros((nb, T - 2, F2), F32)], axis=1).reshape(M, F2)
    cb = conv_b.reshape(1, F2)
    gfin = g_final.reshape(1, D_MODEL)
    full = lambda w: pl.BlockSpec((M, w), lambda f: (0, 0))
    ca = lambda r, w=tf: pl.BlockSpec((r, w), lambda f: (0, f))
    cbk = lambda r, w=tf: pl.BlockSpec((r, w), lambda f: (0, f + nf))
    y, ua, ub = pl.pallas_call(
        functools.partial(_ffn_sample_kernel, T=T, final_norm=final_norm),
        grid=(nf,),
        in_specs=[
            full(D_MODEL), ca(D_MODEL, 2 * tf), ca(M), cbk(M), ca(M), cbk(M),
            ca(CONV_W), cbk(CONV_W), ca(1), cbk(1),
            pl.BlockSpec((tf, D_MODEL), lambda f: (f, 0)),
            full(D_MODEL), pl.BlockSpec((1, D_MODEL), lambda f: (0, 0)),
        ],
        out_specs=[full(D_MODEL), ca(M), ca(M)],
        out_shape=[
            jax.ShapeDtypeStruct((M, D_MODEL), F32),
            jax.ShapeDtypeStruct((M, D_FF), F32),
            jax.ShapeDtypeStruct((M, D_FF), F32),
        ],
        compiler_params=_params(("arbitrary",), 48),
        name="ffn_sample",
    )(h2, w_up_ab, hal1, hal1, hal2, hal2, conv_w, conv_w, cb, cb, w_down, x1, gfin)
    u = jnp.concatenate([ua, ub], axis=-1).reshape(nb, T, F2)
    return y, u[:, T - (CONV_W - 1):]


def kernel(x_prompt, x_sample, state_ret, cache_kv_w128, cache_kv_w512, cache_kv_w2048, state_conv, g_mix, w_in,
           gn_ret, w_ret_o, w_att_o, w_o, g_ffn, w_up, conv_w, conv_b, w_down, g_final):
    B, S, _ = x_prompt.shape
    NB, T, _ = x_sample.shape
    depth = w_in.shape[0]
    caches = (cache_kv_w128, cache_kv_w512, cache_kv_w2048)
    F2 = 2 * D_FF
    TM = 1024
    tiles_per_seq = S // TM

    xp = x_prompt.reshape(B * S, D_MODEL)
    xs = x_sample.reshape(NB * T, D_MODEL)
    ret_p, ret_s, conv_p, conv_s = [], [], [], []
    kv_p = [[] for _ in range(N_GROUPS)]
    kv_s = [[] for _ in range(N_GROUPS)]
    for l in range(depth):
        ret_tiles = 4 * RET_WIDTH // CAST_TN
        att_tiles = ATT_COLS // CAST_TN
        in_tiles = IN_COLS // CAST_TN
        w_main_l = _cast_cols(w_in[l], lambda n: (n + ret_tiles + att_tiles) % in_tiles, MAIN_COLS // CAST_TN,
                              CAST_TN, "cast_w_main")
        w_att_l = _cast_cols(w_in[l], lambda n: n + ret_tiles, att_tiles, CAST_TN, "cast_w_att")
        half_chunks = D_FF // UP_CHUNK
        w_up_l = _cast_cols(w_up[l], lambda n: n // 2 + (n % 2) * half_chunks, 2 * half_chunks, UP_CHUNK,
                            "cast_w_up")
        w_ret_o_l = w_ret_o[l].astype(BF16)
        w_att_o_l = w_att_o[l].astype(BF16)
        w_o_l = w_o[l].astype(BF16)
        w_down_l = w_down[l].astype(BF16)

        proj, h = _inproj_main(xp, g_mix[l], w_main_l, tm=TM)
        yret, st = _ret_prompt(proj.reshape(B, S, MAIN_COLS), gn_ret[l])
        ret_p.append(st)
        os_, lses = [], []
        for g, (win, _) in enumerate(ATT_GROUPS):
            qkv, kvf = _inproj_att(h, w_att_l, g, B=B, S=S, tm=512)
            o, lse = _att_prompt(qkv, g)
            os_.append(o)
            lses.append(lse)
            kv_p[g].append(kvf.reshape(B, min(win, S), 2, ATT_HG, ATT_HD))
        x1, h2 = _epilogue(xp, proj, yret.reshape(B * S, RET_WIDTH), os_, lses,
                           w_ret_o_l, w_att_o_l, w_o_l, g_ffn[l], tm=256, seq_len=S)
        conv_w_l, conv_b_l = _halve_b_half(conv_w[l]), _halve_b_half(conv_b[l])
        gated, tail_a, tail_b = _up_gate(h2, w_up_l, conv_w_l, conv_b_l, tm=TM, tiles_per_seq=tiles_per_seq)
        utail = jnp.concatenate([tail_a, tail_b], axis=-1)
        conv_p.append(utail[tiles_per_seq - 1::tiles_per_seq, 8 - (CONV_W - 1):])
        xp = _down(gated, x1, w_down_l, g_final, tm=256, final_norm=l == depth - 1)

        proj_s = _inproj_sample(xs, g_mix[l], w_main_l, "inproj_sample_main")
        att_s = _inproj_sample(xs, g_mix[l], w_att_l, "inproj_sample_att")
        yret_s, st_s = _ret_sample(proj_s, state_ret[l], gn_ret[l], T)
        ret_s.append(st_s)
        os_, lses = [], []
        for g in range(N_GROUPS):
            o, lse, new_cache = _att_sample(att_s, caches[g][l], g, T)
            os_.append(o)
            lses.append(lse)
            kv_s[g].append(new_cache)
        x1s, h2s = _epilogue(xs, proj_s, yret_s, os_, lses, w_ret_o_l, w_att_o_l, w_o_l, g_ffn[l], tm=NB * T)
        xs, cv = _ffn_sample(h2s, x1s, state_conv[l], w_up_l, conv_w_l, conv_b_l, w_down_l,
                             g_final, T, final_norm=l == depth - 1)
        conv_s.append(cv)

    return (xp.reshape(B, S, D_MODEL), xs.reshape(NB, T, D_MODEL),
            jnp.stack(ret_p), jnp.stack(ret_s),
            jnp.stack(kv_p[0]), jnp.stack(kv_s[0]),
            jnp.stack(kv_p[1]), jnp.stack(kv_s[1]),
            jnp.stack(kv_p[2]), jnp.stack(kv_s[2]),
            jnp.stack(conv_p), jnp.stack(conv_s))
```

```python
import functools
import math

import jax
import jax.numpy as jnp
from jax import lax
from jax.experimental import pallas as pl
from jax.experimental.pallas import tpu as pltpu

D_MODEL = 2048
RET_HEADS = 8
RET_DK = 128
RET_DV = 128
RET_WIDTH = RET_HEADS * RET_DV
RET_CHUNK = 128
ATT_GROUPS = ((128, 1), (512, 4), (2048, 16))
N_GROUPS = 3
ATT_HG = 4
ATT_HD = 128
ATT_GW = ATT_HG * ATT_HD
ATT_SPAN = 128
ATT_BLK = 128
ALIBI_MAX = 8.0
D_FF = 5632
CONV_W = 3
EPS = 1e-6
NEG = -1e30
IN_COLS = 4 * RET_WIDTH + 3 * N_GROUPS * ATT_GW + 2 * D_MODEL

GATE_COLS = 2 * D_MODEL
RET_COL0 = GATE_COLS
MAIN_COLS = GATE_COLS + 4 * RET_WIDTH
ATT_COLS = 3 * N_GROUPS * ATT_GW
CAST_TN = 512

LANES = 128
MIB = 1024 * 1024
BF16 = jnp.bfloat16
F32 = jnp.float32


def _params(semantics, vmem_mib, flags=None):
    return pltpu.CompilerParams(dimension_semantics=semantics, vmem_limit_bytes=vmem_mib * MIB, flags=flags)


def _dot(a, b):
    return jnp.dot(a, b, preferred_element_type=F32)


def _dot_nt(a, b):
    return lax.dot_general(a, b, (((1,), (1,)), ((), ())), preferred_element_type=F32)


def _rmsnorm_rows(x, g):
    ms = jnp.mean(x * x, axis=-1, keepdims=True)
    return x * lax.rsqrt(ms + EPS) * g


def _cast_cols_kernel(w_ref, o_ref):
    o_ref[...] = w_ref[...].astype(o_ref.dtype)


def _cast_cols(w, src_tile, n_tiles, tn, name):
    K = w.shape[0]
    return pl.pallas_call(
        _cast_cols_kernel,
        grid=(n_tiles,),
        in_specs=[pl.BlockSpec((K, tn), lambda n: (0, src_tile(n)))],
        out_specs=pl.BlockSpec((K, tn), lambda n: (0, n)),
        out_shape=jax.ShapeDtypeStruct((K, n_tiles * tn), BF16),
        compiler_params=_params(("arbitrary",), 32),
        name=name,
    )(w)


def _inproj_sample_kernel(x_ref, g_ref, w_ref, proj_ref, h_scr):
    @pl.when(pl.program_id(0) == 0)
    def _():
        h_scr[...] = _rmsnorm_rows(x_ref[...], g_ref[...]).astype(BF16)

    proj_ref[...] = _dot(h_scr[...], w_ref[...])


def _inproj_sample(x2d, g, w_bf16, name):
    M = x2d.shape[0]
    N = w_bf16.shape[1]
    tn = 512
    return pl.pallas_call(
        _inproj_sample_kernel,
        grid=(N // tn,),
        in_specs=[
            pl.BlockSpec((M, D_MODEL), lambda n: (0, 0)),
            pl.BlockSpec((1, D_MODEL), lambda n: (0, 0)),
            pl.BlockSpec((D_MODEL, tn), lambda n: (0, n)),
        ],
        out_specs=pl.BlockSpec((M, tn), lambda n: (0, n)),
        out_shape=jax.ShapeDtypeStruct((M, N), F32),
        scratch_shapes=[pltpu.VMEM((M, D_MODEL), BF16)],
        compiler_params=_params(("arbitrary",), 32),
        name=name,
    )(x2d, g.reshape(1, D_MODEL), w_bf16)


def _inproj_main_kernel(x_ref, g_ref, w_ref, main_ref, h_ref):
    @pl.when(pl.program_id(1) == 0)
    def _():
        h_ref[...] = _rmsnorm_rows(x_ref[...], g_ref[...]).astype(BF16)

    main_ref[...] = _dot(h_ref[...], w_ref[...]).astype(BF16)


def _inproj_main(x2d, g, w_main, *, tm):
    M = x2d.shape[0]
    N = w_main.shape[1]
    tn = 1024
    return pl.pallas_call(
        _inproj_main_kernel,
        grid=(M // tm, N // tn),
        in_specs=[
            pl.BlockSpec((tm, D_MODEL), lambda m, n: (m, 0)),
            pl.BlockSpec((1, D_MODEL), lambda m, n: (0, 0)),
            pl.BlockSpec((D_MODEL, tn), lambda m, n: (0, n)),
        ],
        out_specs=[pl.BlockSpec((tm, tn), lambda m, n: (m, n)), pl.BlockSpec((tm, D_MODEL), lambda m, n: (m, 0))],
        out_shape=[jax.ShapeDtypeStruct((M, N), BF16), jax.ShapeDtypeStruct((M, D_MODEL), BF16)],
        compiler_params=_params(("arbitrary", "arbitrary"), 48),
        name="inproj_main",
    )(x2d, g.reshape(1, D_MODEL), w_main)


def _inproj_att_kernel(h_ref, w_ref, a_ref, kv_ref, *scratch, dil):
    acc = _dot(h_ref[...], w_ref[...])
    slots = 2 * ATT_HG
    rows = kv_ref.shape[0] // slots
    for slot in range(slots):
        cols = slice(ATT_GW + slot * ATT_HD, ATT_GW + (slot + 1) * ATT_HD)
        kv_ref[pl.ds(slot, rows, stride=slots), :] = acc[acc.shape[0] - rows:, cols]
    if dil == 1:
        a_ref[0] = acc.astype(BF16)
    else:
        acc_scr, = scratch
        rows = acc_scr.shape[1] // dil
        for cb in range(acc_scr.shape[0]):
            lanes = slice(cb * LANES, (cb + 1) * LANES)
            acc_scr[cb] = acc[:, lanes]
            for r in range(dil):
                a_ref[r, :, lanes] = acc_scr[cb, pl.ds(r, rows, stride=dil), :].astype(BF16)


def _inproj_att(h, w_att, group, *, B, S, tm):
    M = h.shape[0]
    tn = 3 * ATT_GW
    tps = S // tm
    win, dil = ATT_GROUPS[group]
    win = min(win, S)
    rows = min(win, tm)
    first = tps - win // rows
    slots = 2 * ATT_HG
    return pl.pallas_call(
        functools.partial(_inproj_att_kernel, dil=dil),
        grid=(M // tm,),
        in_specs=[
            pl.BlockSpec((tm, D_MODEL), lambda m: (m, 0)),
            pl.BlockSpec((D_MODEL, tn), lambda m: (0, group)),
        ],
        out_specs=[
            pl.BlockSpec((None, dil, tm // dil, tn), lambda m: (m // tps, 0, m % tps, 0)),
            pl.BlockSpec((None, rows * slots, ATT_HD), lambda m: (m // tps, jnp.maximum(m % tps - first, 0), 0)),
        ],
        out_shape=[
            jax.ShapeDtypeStruct((B, dil, S // dil, tn), BF16),
            jax.ShapeDtypeStruct((B, win * slots, ATT_HD), F32),
        ],
        scratch_shapes=[] if dil == 1 else [pltpu.VMEM((tn // LANES, tm, LANES), F32)],
        compiler_params=_params(("arbitrary",), 48),
        name=f"inproj_att_g{group}",
    )(h, w_att)


def _ret_tables(lq):
    lg = jnp.log(1.0 - jnp.exp2(-5.0 - jnp.arange(RET_HEADS, dtype=F32)))
    scale = RET_DK ** -0.5
    i = jnp.arange(lq, dtype=F32)
    j = jnp.arange(RET_CHUNK, dtype=F32)
    diff = i[:, None] - j[None, :]
    live = (diff >= 0) & (j[None, :] < lq)
    dm = jnp.where(live[None], jnp.exp(jnp.maximum(diff, 0.0)[None] * lg[:, None, None]), 0.0) * scale
    qd = jnp.exp((i[None, :] + 1.0) * lg[:, None])
    qd = jnp.broadcast_to(qd[:, :, None], (RET_HEADS, lq, RET_DV))
    kd = jnp.where(j[None, :] < lq, jnp.exp(jnp.maximum(lq - 1.0 - j, 0.0)[None, :] * lg[:, None]), 0.0) * scale
    kd = jnp.broadcast_to(kd[:, :, None], (RET_HEADS, RET_CHUNK, RET_DK))
    gl = jnp.exp(lq * lg)
    return dm.astype(F32), qd.astype(F32), kd.astype(F32), gl.astype(F32)


def _ret_heads(qs, ks, vs, sts, dms, qds, kds, gls):
    ss = [(_dot_nt(q, k) * dm).astype(BF16) for q, k, dm in zip(qs, ks, dms)]
    crosses = [_dot(q, st.astype(BF16)) for q, st in zip(qs, sts)]
    kts = [(k.astype(F32) * kd).T.astype(BF16) for k, kd in zip(ks, kds)]
    inners = [_dot(s, v) for s, v in zip(ss, vs)]
    updates = [_dot(kt, v) for kt, v in zip(kts, vs)]
    ys = [inner + cross * qd for inner, cross, qd in zip(inners, crosses, qds)]
    new_sts = [st * gl + upd for st, gl, upd in zip(sts, gls, updates)]
    return ys, new_sts


def _groupnorm_gates(ys, gns, rgs):
    mus = [jnp.mean(y, axis=-1, keepdims=True) for y in ys]
    ycs = [y - mu for y, mu in zip(ys, mus)]
    rstds = [lax.rsqrt(jnp.mean(yc * yc, axis=-1, keepdims=True) + EPS) for yc in ycs]
    return [(rg * jax.nn.sigmoid(rg)) * ((yc * rstd) * gn) for yc, rstd, gn, rg in zip(ycs, rstds, gns, rgs)]


def _head_slices():
    return [slice(h * RET_DK, (h + 1) * RET_DK) for h in range(RET_HEADS)]


def _ret_prompt_kernel(gl_ref, q_ref, k_ref, v_ref, rg_ref, dm_ref, qd_ref, kd_ref, gn_ref, y_ref, st_ref):
    c = pl.program_id(1)

    @pl.when(c == 0)
    def _():
        st_ref[...] = jnp.zeros_like(st_ref)

    sls = _head_slices()
    heads = range(RET_HEADS)
    ys, new_sts = _ret_heads([q_ref[:, sl] for sl in sls], [k_ref[:, sl] for sl in sls], [v_ref[:, sl] for sl in sls],
                             [st_ref[h] for h in heads], [dm_ref[h] for h in heads], [qd_ref[h] for h in heads],
                             [kd_ref[h] for h in heads], [gl_ref[h] for h in heads])
    for h in heads:
        st_ref[h] = new_sts[h]
    outs = _groupnorm_gates(ys, [gn_ref[:, sl] for sl in sls], [rg_ref[:, sl].astype(F32) for sl in sls])
    for sl, out in zip(sls, outs):
        y_ref[:, sl] = out.astype(y_ref.dtype)


def _ret_prompt(proj3, gn_ret):
    B, S, _ = proj3.shape
    L = RET_CHUNK
    dm, qd, kd, gl = _ret_tables(L)
    cb = RET_COL0 // RET_WIDTH
    col = lambda k: pl.BlockSpec((None, L, RET_WIDTH), lambda b, c, k=k: (b, c, cb + k))
    tab = pl.BlockSpec((RET_HEADS, L, RET_DK), lambda b, c: (0, 0, 0))
    return pl.pallas_call(
        _ret_prompt_kernel,
        grid=(B, S // L),
        in_specs=[
            pl.BlockSpec(memory_space=pltpu.SMEM),
            col(0), col(1), col(2), col(3), tab, tab, tab,
            pl.BlockSpec((1, RET_WIDTH), lambda b, c: (0, 0)),
        ],
        out_specs=[
            pl.BlockSpec((None, L, RET_WIDTH), lambda b, c: (b, c, 0)),
            pl.BlockSpec((None, RET_HEADS, RET_DK, RET_DV), lambda b, c: (b, 0, 0, 0)),
        ],
        out_shape=[
            jax.ShapeDtypeStruct((B, S, RET_WIDTH), BF16),
            jax.ShapeDtypeStruct((B, RET_HEADS, RET_DK, RET_DV), F32),
        ],
        compiler_params=_params(("arbitrary", "arbitrary"), 32),
        name="ret_prompt",
    )(gl, proj3, proj3, proj3, proj3, dm, qd, kd, gn_ret.reshape(1, RET_WIDTH))


def _pad_rows(x, rows):
    x = x.astype(F32)
    return jnp.concatenate([x, jnp.zeros((rows - x.shape[0], x.shape[1]), F32)], axis=0)


def _ret_sample_kernel(gl_ref, q_ref, k_ref, v_ref, rg_ref, st_in_ref, dm_ref, qd_ref, kd_ref, gn_ref,
                       y_ref, st_ref):
    sls = _head_slices()
    heads = range(RET_HEADS)
    ys, new_sts = _ret_heads([q_ref[:, sl].astype(BF16) for sl in sls],
                             [_pad_rows(k_ref[:, sl], RET_CHUNK).astype(BF16) for sl in sls],
                             [_pad_rows(v_ref[:, sl], RET_CHUNK).astype(BF16) for sl in sls],
                             [st_in_ref[h] for h in heads], [dm_ref[h] for h in heads], [qd_ref[h] for h in heads],
                             [kd_ref[h] for h in heads], [gl_ref[h] for h in heads])
    for h in heads:
        st_ref[h] = new_sts[h]
    outs = _groupnorm_gates(ys, [gn_ref[:, sl] for sl in sls], [rg_ref[:, sl].astype(F32) for sl in sls])
    for sl, out in zip(sls, outs):
        y_ref[:, sl] = out.astype(y_ref.dtype)


def _ret_sample(proj_s, state, gn_ret, T):
    nb = state.shape[0]
    dm, qd, kd, gl = _ret_tables(T)
    cb = RET_COL0 // RET_WIDTH
    col = lambda k: pl.BlockSpec((T, RET_WIDTH), lambda b, k=k: (b, cb + k))
    st_spec = pl.BlockSpec((None, RET_HEADS, RET_DK, RET_DV), lambda b: (b, 0, 0, 0))
    return pl.pallas_call(
        _ret_sample_kernel,
        grid=(nb,),
        in_specs=[
            pl.BlockSpec(memory_space=pltpu.SMEM),
            col(0), col(1), col(2), col(3), st_spec,
            pl.BlockSpec((RET_HEADS, T, RET_CHUNK), lambda b: (0, 0, 0)),
            pl.BlockSpec((RET_HEADS, T, RET_DV), lambda b: (0, 0, 0)),
            pl.BlockSpec((RET_HEADS, RET_CHUNK, RET_DK), lambda b: (0, 0, 0)),
            pl.BlockSpec((1, RET_WIDTH), lambda b: (0, 0)),
        ],
        out_specs=[pl.BlockSpec((T, RET_WIDTH), lambda b: (b, 0)), st_spec],
        out_shape=[
            jax.ShapeDtypeStruct((nb * T, RET_WIDTH), F32),
            jax.ShapeDtypeStruct(state.shape, F32),
        ],
        compiler_params=_params(("arbitrary",), 32),
        name="ret_sample",
    )(gl, proj_s, proj_s, proj_s, proj_s, state, dm, qd, kd, gn_ret.reshape(1, RET_WIDTH))


def _alibi_slope(g, h):
    n = N_GROUPS * ATT_HG
    return 2.0 ** (-ALIBI_MAX * (g * ATT_HG + h + 1) / n)


ATT_SUB = 4


def _att_prompt_kernel(q_ref, kp_ref, kc_ref, vp_ref, vc_ref, o_ref, lse_ref, k_scr, v_scr, bias_scr,
                       *, group, dil, nsub):
    j = pl.program_id(2)
    blk = ATT_BLK
    hd = ATT_HD
    k_scr[0:blk] = kp_ref[...]
    k_scr[blk:] = kc_ref[...]
    ones = jnp.ones((blk * (nsub + 1), hd), BF16)
    for h in range(ATT_HG):
        v_scr[0:blk, 2 * h * hd:(2 * h + 1) * hd] = vp_ref[:, h * hd:(h + 1) * hd]
        v_scr[blk:, 2 * h * hd:(2 * h + 1) * hd] = vc_ref[:, h * hd:(h + 1) * hd]
        v_scr[:, (2 * h + 1) * hd:(2 * h + 2) * hd] = ones

    r = lax.broadcasted_iota(jnp.int32, (blk, 2 * blk), 0)
    c = lax.broadcasted_iota(jnp.int32, (blk, 2 * blk), 1)
    dist = blk + r - c
    in_band = (dist >= 0) & (dist <= ATT_SPAN)
    for h in range(ATT_HG):
        bias_scr[h] = jnp.where(in_band, (-_alibi_slope(group, h) * dil) * dist.astype(F32), NEG)
    no_prev = jnp.where((c < blk) & (j == 0), NEG, 0.0)

    scale = hd ** -0.5
    for i in range(nsub):
        rows = slice(i * blk, (i + 1) * blk)
        keys = slice(i * blk, (i + 2) * blk)
        heads = range(ATT_HG)
        ss = [_dot_nt(q_ref[rows, h * hd:(h + 1) * hd], k_scr[keys, h * hd:(h + 1) * hd]) * scale + bias_scr[h]
              for h in heads]
        if i == 0:
            ss = [s + no_prev for s in ss]
        ms = [s.max(axis=-1, keepdims=True) for s in ss]
        ps = [jnp.exp(s - m).astype(BF16) for s, m in zip(ss, ms)]
        rs = [_dot(p, v_scr[keys, 2 * h * hd:(2 * h + 2) * hd]) for h, p in zip(heads, ps)]
        for h, res, m in zip(heads, rs, ms):
            l = res[:, hd:]
            o_ref[rows, h * hd:(h + 1) * hd] = (res[:, :hd] / l).astype(o_ref.dtype)
            lse_ref[rows, h * hd:(h + 1) * hd] = m + jnp.log(l)


def _att_prompt(qkv, group):
    B, dil, L, _ = qkv.shape
    nsub = math.gcd(ATT_SUB, L // ATT_BLK)
    rows = nsub * ATT_BLK
    cur = lambda which: pl.BlockSpec((None, None, rows, ATT_GW), lambda b, r, j: (b, r, j, which))
    prev = lambda which: pl.BlockSpec(
        (None, None, ATT_BLK, ATT_GW), lambda b, r, j: (b, r, jnp.maximum(j * nsub - 1, 0), which))
    out_spec = pl.BlockSpec((None, None, rows, ATT_GW), lambda b, r, j: (b, r, j, 0))
    return pl.pallas_call(
        functools.partial(_att_prompt_kernel, group=group, dil=dil, nsub=nsub),
        grid=(B, dil, L // rows),
        in_specs=[cur(0), prev(1), cur(1), prev(2), cur(2)],
        out_specs=[out_spec, out_spec],
        out_shape=[
            jax.ShapeDtypeStruct((B, dil, L, ATT_GW), BF16),
            jax.ShapeDtypeStruct((B, dil, L, ATT_GW), F32),
        ],
        scratch_shapes=[
            pltpu.VMEM((rows + ATT_BLK, ATT_GW), BF16),
            pltpu.VMEM((rows + ATT_BLK, 2 * ATT_GW), BF16),
            pltpu.VMEM((ATT_HG, ATT_BLK, 2 * ATT_BLK), F32),
        ],
        compiler_params=_params(("arbitrary", "arbitrary", "arbitrary"), 32),
        name=f"att_prompt_g{group}",
    )(qkv, qkv, qkv, qkv, qkv)


def _att_sample_kernel(q_ref, k_ref, v_ref, c_ref, shifted_ref, o_ref, lse_ref, nc_ref, *, group, dil, wb, T):
    del shifted_ref
    slots = 2 * ATT_HG
    t = lax.broadcasted_iota(jnp.int32, (T, wb), 0)
    c = lax.broadcasted_iota(jnp.int32, (T, wb), 1)
    dist_c = wb + t - c
    valid_c = ((dist_c & (dil - 1)) == 0) & (dist_c <= ATT_SPAN * dil)
    tn = lax.broadcasted_iota(jnp.int32, (T, ATT_BLK), 0)
    cn = lax.broadcasted_iota(jnp.int32, (T, ATT_BLK), 1)
    dist_n = tn - cn
    valid_n = (dist_n >= 0) & ((dist_n & (dil - 1)) == 0) & (cn < T)
    scale = ATT_HD ** -0.5
    heads = range(ATT_HG)
    sls = [slice(h * ATT_HD, (h + 1) * ATT_HD) for h in heads]
    for h, sl in zip(heads, sls):
        nc_ref[pl.ds(h, T, stride=slots), :] = k_ref[:, sl]
        nc_ref[pl.ds(ATT_HG + h, T, stride=slots), :] = v_ref[:, sl]
    qs = [q_ref[:, sl].astype(BF16) for sl in sls]
    kcs = [c_ref[pl.ds(h, wb, stride=slots), :].astype(BF16) for h in heads]
    vcs = [c_ref[pl.ds(ATT_HG + h, wb, stride=slots), :].astype(BF16) for h in heads]
    kns = [_pad_rows(k_ref[:, sl], ATT_BLK).astype(BF16) for sl in sls]
    vns = [_pad_rows(v_ref[:, sl], ATT_BLK).astype(BF16) for sl in sls]
    bias_c = jnp.where(valid_c, dist_c.astype(F32), -NEG)
    bias_n = jnp.where(valid_n, dist_n.astype(F32), -NEG)
    scs = [_dot_nt(q, kc) * scale - _alibi_slope(group, h) * bias_c for h, q, kc in zip(heads, qs, kcs)]
    sns = [_dot_nt(q, kn) * scale - _alibi_slope(group, h) * bias_n for h, q, kn in zip(heads, qs, kns)]
    ms = [jnp.maximum(sc.max(axis=-1, keepdims=True), sn.max(axis=-1, keepdims=True)) for sc, sn in zip(scs, sns)]
    pcs = [jnp.exp(sc - m) for sc, m in zip(scs, ms)]
    pns = [jnp.exp(sn - m) for sn, m in zip(sns, ms)]
    ls = [pc.sum(axis=-1, keepdims=True) + pn.sum(axis=-1, keepdims=True) for pc, pn in zip(pcs, pns)]
    accs = [_dot(pc.astype(BF16), vc) + _dot(pn.astype(BF16), vn) for pc, pn, vc, vn in zip(pcs, pns, vcs, vns)]
    for sl, acc, l, m in zip(sls, accs, ls, ms):
        o_ref[:, sl] = (acc / l).astype(o_ref.dtype)
        lse_ref[:, sl] = jnp.broadcast_to(m + jnp.log(l), (T, ATT_HD))


def _cache_rows(cache):
    return cache.reshape(cache.shape[0], cache.shape[1] * 2 * ATT_HG, ATT_HD)


def _att_sample(proj_s, cache, shifted, group, T):
    nb, wb = cache.shape[0], cache.shape[1]
    assert wb > T and wb % T == 0
    _, dil = ATT_GROUPS[group]
    c0 = 3 * group
    slots = 2 * ATT_HG
    col = lambda k: pl.BlockSpec((T, ATT_GW), lambda b, k=k: (b, c0 + k))
    out_spec = pl.BlockSpec((T, ATT_GW), lambda b: (b, 0))
    o, lse, new_cache = pl.pallas_call(
        functools.partial(_att_sample_kernel, group=group, dil=dil, wb=wb, T=T),
        grid=(nb,),
        in_specs=[col(0), col(1), col(2), pl.BlockSpec((None, wb * slots, ATT_HD), lambda b: (b, 0, 0)),
                  pl.BlockSpec(memory_space=pl.ANY)],
        out_specs=[out_spec, out_spec, pl.BlockSpec((None, T * slots, ATT_HD), lambda b: (b, wb // T - 1, 0))],
        out_shape=[
            jax.ShapeDtypeStruct((nb * T, ATT_GW), F32),
            jax.ShapeDtypeStruct((nb * T, ATT_GW), F32),
            jax.ShapeDtypeStruct(shifted.shape, F32),
        ],
        input_output_aliases={4: 2},
        compiler_params=_params(("arbitrary",), 48),
        name=f"att_sample_g{group}",
    )(proj_s, proj_s, proj_s, _cache_rows(cache), shifted)
    return o, lse, new_cache.reshape(cache.shape)


def _epilogue_kernel(x_ref, gr_ref, ga_ref, yret_ref, o0_ref, o1_ref, o2_ref, l0_ref, l1_ref, l2_ref,
                     wr_ref, wa_ref, wo_ref, gf_ref, x1_ref, h2_ref, *scratch, by_residue):
    def position_order(ref, scr):
        dil = ref.shape[0]
        if dil == 1:
            return ref[0].astype(F32)
        for r in range(dil):
            slab = ref[r].astype(F32)
            for cb in range(scr.shape[0]):
                scr[cb, pl.ds(r, ref.shape[1], stride=dil), :] = slab[:, cb * LANES:(cb + 1) * LANES]
        return jnp.concatenate([scr[cb] for cb in range(scr.shape[0])], axis=-1)

    if by_residue:
        o0, o1, o2 = (position_order(ref, scr) for ref, scr in zip((o0_ref, o1_ref, o2_ref), scratch[0:3]))
        l0, l1, l2 = (position_order(ref, scr) for ref, scr in zip((l0_ref, l1_ref, l2_ref), scratch[3:6]))
    else:
        o0, o1, o2 = o0_ref[...], o1_ref[...], o2_ref[...]
        l0, l1, l2 = l0_ref[...], l1_ref[...], l2_ref[...]
    m = jnp.maximum(jnp.maximum(l0, l1), l2)
    e0, e1, e2 = jnp.exp(l0 - m), jnp.exp(l1 - m), jnp.exp(l2 - m)
    inv = 1.0 / (e0 + e1 + e2)
    o = (e0 * inv) * o0 + (e1 * inv) * o1 + (e2 * inv) * o2
    att = _dot(o.astype(BF16), wa_ref[...])
    ret = _dot(yret_ref[...].astype(BF16), wr_ref[...])
    merged = (jax.nn.sigmoid(gr_ref[...].astype(F32)) * ret + jax.nn.sigmoid(ga_ref[...].astype(F32)) * att)
    x1 = x_ref[...] + _dot(merged.astype(BF16), wo_ref[...])
    x1_ref[...] = x1
    h2_ref[...] = _rmsnorm_rows(x1, gf_ref[...]).astype(BF16)


def _epilogue(x2d, proj2, yret, os_, lses, w_ret_o, w_att_o, w_o, g_ffn, *, tm, seq_len=None):
    M = x2d.shape[0]
    row = lambda w: pl.BlockSpec((tm, w), lambda m: (m, 0))
    const = lambda a: pl.BlockSpec(a.shape, lambda m: (0, 0), pipeline_mode=pl.Buffered(1))
    gf = g_ffn.reshape(1, D_MODEL)
    by_residue = seq_len is not None
    scratch = []
    if by_residue:
        tps = seq_len // tm
        att_specs = [pl.BlockSpec((None, a.shape[1], tm // a.shape[1], ATT_GW), lambda m: (m // tps, 0, m % tps, 0))
                     for a in (*os_, *lses)]
        scratch = [pltpu.VMEM((ATT_GW // LANES, tm, LANES), F32)] * 6
    else:
        att_specs = [row(ATT_GW)] * 6
    return pl.pallas_call(
        functools.partial(_epilogue_kernel, by_residue=by_residue),
        grid=(M // tm,),
        in_specs=[
            row(D_MODEL),
            pl.BlockSpec((tm, D_MODEL), lambda m: (m, 0)),
            pl.BlockSpec((tm, D_MODEL), lambda m: (m, 1)),
            row(RET_WIDTH), *att_specs,
            const(w_ret_o), const(w_att_o), const(w_o), const(gf),
        ],
        out_specs=[row(D_MODEL), row(D_MODEL)],
        out_shape=[jax.ShapeDtypeStruct((M, D_MODEL), F32), jax.ShapeDtypeStruct((M, D_MODEL), BF16)],
        scratch_shapes=scratch,
        compiler_params=_params(("arbitrary",), 56),
        name="epilogue",
    )(x2d, proj2, proj2, yret, *os_, *lses, w_ret_o, w_att_o, w_o, gf)


def _gelu_gate(a, half_b):
    return (a * (1.0 + lax.erf(a * math.sqrt(0.5)))) * half_b


def _halve_b_half(x):
    return jnp.concatenate([x[..., :D_FF], 0.5 * x[..., D_FF:]], axis=-1)


def _conv3(u, r1, r2, w_ref, b_ref):
    return b_ref[...] + (r2 * w_ref[0:1, :] + r1 * w_ref[1:2, :] + u * w_ref[2:3, :])


UP_CHUNK = 256


SHIFT_CHUNK_ROWS = 4096


def _shift_plan(cache_rows, T, steps):
    nb, n, _ = cache_rows.shape
    shift = T * 2 * ATT_HG
    body = n - shift
    per_seq = -(-body // SHIFT_CHUNK_ROWS)
    while body % per_seq or (body // per_seq) % 8:
        per_seq += 1
    plan = dict(shift=shift, chunk=body // per_seq, per_seq=per_seq, count=nb * per_seq)
    assert plan["count"] < steps, "the background copy needs one grid step per chunk plus one to drain"
    return plan


def _shift_copy_step(s, src_ref, dst_ref, buf, sem_in, sem_out, sem_tail, *, shift, chunk, per_seq, count):
    n = src_ref.shape[1]
    tail = pltpu.make_async_copy(src_ref.at[:, pl.ds(n - shift, shift)], dst_ref.at[:, pl.ds(n - shift, shift)], sem_tail)

    @pl.when(s == 0)
    def _():
        tail.start()

    @pl.when(s == count)
    def _():
        tail.wait()

    def read(c, slot):
        start = pl.multiple_of(shift + (c % per_seq) * chunk, 8)
        return pltpu.make_async_copy(src_ref.at[c // per_seq, pl.ds(start, chunk)], buf.at[slot], sem_in.at[slot])

    def write(c, slot):
        start = pl.multiple_of((c % per_seq) * chunk, 8)
        return pltpu.make_async_copy(buf.at[slot], dst_ref.at[c // per_seq, pl.ds(start, chunk)], sem_out.at[slot])

    slot = s % 2

    @pl.when(s == 0)
    def _():
        read(0, 0).start()

    @pl.when(s < count)
    def _():
        read(s, slot).wait()
        write(s, slot).start()

    @pl.when((s >= 1) & (s <= count))
    def _():
        write(s - 1, 1 - slot).wait()

    @pl.when(s + 1 < count)
    def _():
        read(s + 1, 1 - slot).start()


def _up_gate_kernel(h_ref, w_ref, cwa_ref, cwb_ref, cba_ref, cbb_ref, *rest, tiles_per_seq, shift_plans):
    ng = len(shift_plans)
    caches, rest = rest[:ng], rest[ng:]
    g_ref, ta_ref, tb_ref = rest[:3]
    shifted, rest = rest[3:3 + ng], rest[3 + ng:]
    u_scr, carry_scr = rest[:2]
    bufs, (sem_in, sem_out, sem_tail) = rest[2:2 + ng], rest[2 + ng:]
    m = pl.program_id(0)
    f = pl.program_id(1)
    tm = h_ref.shape[0]

    step = m * pl.num_programs(1) + f
    for i, plan in enumerate(shift_plans):
        _shift_copy_step(step, caches[i], shifted[i], bufs[i], sem_in.at[i], sem_out.at[i], sem_tail.at[i], **plan)

    @pl.when(m % tiles_per_seq == 0)
    def _():
        carry_scr[f] = jnp.zeros(carry_scr.shape[1:], F32)

    u_scr[0:8, :] = carry_scr[f]
    u_scr[8:, :] = _dot(h_ref[...], w_ref[...])
    carry_scr[f] = u_scr[tm:, :]
    for c in range(g_ref.shape[1] // UP_CHUNK):
        cols = slice(c * UP_CHUNK, (c + 1) * UP_CHUNK)

        def conv(half, cw_ref, cb_ref, tail_ref):
            pcols = slice((2 * c + half) * UP_CHUNK, (2 * c + half + 1) * UP_CHUNK)
            tail_ref[:, cols] = u_scr[tm:, pcols]
            u, r1, r2 = u_scr[8:, pcols], u_scr[7:tm + 7, pcols], u_scr[6:tm + 6, pcols]
            return cb_ref[:, cols] + (r2 * cw_ref[0:1, cols] + r1 * cw_ref[1:2, cols] + u * cw_ref[2:3, cols])

        a = conv(0, cwa_ref, cba_ref, ta_ref)
        b = conv(1, cwb_ref, cbb_ref, tb_ref)
        g_ref[:, cols] = _gelu_gate(a, b).astype(BF16)


def _up_gate(h2, w_up_ab, conv_w, conv_b, caches, T, *, tm, tiles_per_seq):
    M = h2.shape[0]
    tf = 512
    nf = D_FF // tf
    cb = conv_b.reshape(1, 2 * D_FF)
    grid = (M // tm, nf)
    plans = tuple(_shift_plan(c, T, grid[0] * grid[1]) for c in caches)
    any_spec = pl.BlockSpec(memory_space=pl.ANY)
    return pl.pallas_call(
        functools.partial(_up_gate_kernel, tiles_per_seq=tiles_per_seq, shift_plans=plans),
        grid=grid,
        in_specs=[
            pl.BlockSpec((tm, D_MODEL), lambda m, f: (m, 0)),
            pl.BlockSpec((D_MODEL, 2 * tf), lambda m, f: (0, f)),
            pl.BlockSpec((CONV_W, tf), lambda m, f: (0, f)),
            pl.BlockSpec((CONV_W, tf), lambda m, f: (0, f + nf)),
            pl.BlockSpec((1, tf), lambda m, f: (0, f)),
            pl.BlockSpec((1, tf), lambda m, f: (0, f + nf)),
            *[any_spec for _ in caches],
        ],
        out_specs=[
            pl.BlockSpec((tm, tf), lambda m, f: (m, f)),
            pl.BlockSpec((None, 8, tf), lambda m, f: (m, 0, f)),
            pl.BlockSpec((None, 8, tf), lambda m, f: (m, 0, f)),
            *[any_spec for _ in caches],
        ],
        out_shape=[
            jax.ShapeDtypeStruct((M, D_FF), BF16),
            jax.ShapeDtypeStruct((M // tm, 8, D_FF), F32),
            jax.ShapeDtypeStruct((M // tm, 8, D_FF), F32),
            *[jax.ShapeDtypeStruct(c.shape, F32) for c in caches],
        ],
        scratch_shapes=[
            pltpu.VMEM((tm + 8, 2 * tf), F32),
            pltpu.VMEM((nf, 8, 2 * tf), F32),
            *[pltpu.VMEM((2, p["chunk"], ATT_HD), F32) for p in plans],
            pltpu.SemaphoreType.DMA((len(caches), 2)),
            pltpu.SemaphoreType.DMA((len(caches), 2)),
            pltpu.SemaphoreType.DMA((len(caches),)),
        ],
        compiler_params=_params(("arbitrary", "arbitrary"), 56),
        name="up_gate",
    )(h2, w_up_ab, conv_w, conv_w, cb, cb, *caches)


def _down_kernel(g_ref, wd_ref, x1_ref, gfin_ref, y_ref, *, final_norm):
    y = x1_ref[...] + _dot(g_ref[...], wd_ref[...])
    y_ref[...] = _rmsnorm_rows(y, gfin_ref[...]) if final_norm else y


def _down(g, x1, w_down, g_final, *, tm, final_norm):
    M = g.shape[0]
    return pl.pallas_call(
        functools.partial(_down_kernel, final_norm=final_norm),
        grid=(M // tm,),
        in_specs=[
            pl.BlockSpec((tm, D_FF), lambda m: (m, 0)),
            pl.BlockSpec((D_FF, D_MODEL), lambda m: (0, 0), pipeline_mode=pl.Buffered(1)),
            pl.BlockSpec((tm, D_MODEL), lambda m: (m, 0)),
            pl.BlockSpec((1, D_MODEL), lambda m: (0, 0)),
        ],
        out_specs=pl.BlockSpec((tm, D_MODEL), lambda m: (m, 0)),
        out_shape=jax.ShapeDtypeStruct((M, D_MODEL), F32),
        compiler_params=_params(("arbitrary",), 48),
        name="down",
    )(g, w_down, x1, g_final.reshape(1, D_MODEL))


def _ffn_sample_kernel(h_ref, wu_ref, h1a_ref, h1b_ref, h2a_ref, h2b_ref, cwa_ref, cwb_ref,
                       cba_ref, cbb_ref, wd_ref, x1_ref, gfin_ref, y_ref, ua_ref, ub_ref, *, T, final_norm):
    f = pl.program_id(0)
    t = lax.broadcasted_iota(jnp.int32, ua_ref.shape, 0) & (T - 1)
    u2 = _dot(h_ref[...], wu_ref[...])
    nchunk = ua_ref.shape[1] // UP_CHUNK
    half = lambda k: jnp.concatenate(
        [u2[:, (2 * c + k) * UP_CHUNK:(2 * c + k + 1) * UP_CHUNK] for c in range(nchunk)], axis=-1)

    def conv(u, hal1_ref, hal2_ref, w_ref, b_ref, u_out_ref):
        u_out_ref[...] = u
        r1 = jnp.where(t >= 1, pltpu.roll(u, 1, axis=0), hal1_ref[...])
        r2 = jnp.where(t >= 2, pltpu.roll(u, 2, axis=0), hal2_ref[...])
        return _conv3(u, r1, r2, w_ref, b_ref)

    a = conv(half(0), h1a_ref, h2a_ref, cwa_ref, cba_ref, ua_ref)
    b = conv(half(1), h1b_ref, h2b_ref, cwb_ref, cbb_ref, ub_ref)
    part = _dot(_gelu_gate(a, b).astype(BF16), wd_ref[...])

    @pl.when(f == 0)
    def _():
        y_ref[...] = x1_ref[...] + part

    @pl.when(f > 0)
    def _():
        y_ref[...] += part

    if final_norm:
        @pl.when(f == pl.num_programs(0) - 1)
        def _():
            y_ref[...] = _rmsnorm_rows(y_ref[...], gfin_ref[...])


def _ffn_sample(h2, x1, state_conv, w_up_ab, conv_w, conv_b, w_down, g_final, T, final_norm):
    M = h2.shape[0]
    nb = M // T
    tf = 512
    nf = D_FF // tf
    F2 = 2 * D_FF
    sc = state_conv.astype(F32)
    hal1 = jnp.concatenate([sc[:, 1:2], jnp.zeros((nb, T - 1, F2), F32)], axis=1).reshape(M, F2)
    hal2 = jnp.concatenate([sc[:, 0:2], jnp.zeros((nb, T - 2, F2), F32)], axis=1).reshape(M, F2)
    cb = conv_b.reshape(1, F2)
    gfin = g_final.reshape(1, D_MODEL)
    full = lambda w: pl.BlockSpec((M, w), lambda f: (0, 0))
    ca = lambda r, w=tf: pl.BlockSpec((r, w), lambda f: (0, f))
    cbk = lambda r, w=tf: pl.BlockSpec((r, w), lambda f: (0, f + nf))
    y, ua, ub = pl.pallas_call(
        functools.partial(_ffn_sample_kernel, T=T, final_norm=final_norm),
        grid=(nf,),
        in_specs=[
            full(D_MODEL), ca(D_MODEL, 2 * tf), ca(M), cbk(M), ca(M), cbk(M),
            ca(CONV_W), cbk(CONV_W), ca(1), cbk(1),
            pl.BlockSpec((tf, D_MODEL), lambda f: (f, 0)),
            full(D_MODEL), pl.BlockSpec((1, D_MODEL), lambda f: (0, 0)),
        ],
        out_specs=[full(D_MODEL), ca(M), ca(M)],
        out_shape=[
            jax.ShapeDtypeStruct((M, D_MODEL), F32),
            jax.ShapeDtypeStruct((M, D_FF), F32),
            jax.ShapeDtypeStruct((M, D_FF), F32),
        ],
        compiler_params=_params(("arbitrary",), 48),
        name="ffn_sample",
    )(h2, w_up_ab, hal1, hal1, hal2, hal2, conv_w, conv_w, cb, cb, w_down, x1, gfin)
    u = jnp.concatenate([ua, ub], axis=-1).reshape(nb, T, F2)
    return y, u[:, T - (CONV_W - 1):]


def kernel(x_prompt, x_sample, state_ret, cache_kv_w128, cache_kv_w512, cache_kv_w2048, state_conv, g_mix, w_in,
           gn_ret, w_ret_o, w_att_o, w_o, g_ffn, w_up, conv_w, conv_b, w_down, g_final):
    B, S, _ = x_prompt.shape
    NB, T, _ = x_sample.shape
    depth = w_in.shape[0]
    caches = (cache_kv_w128, cache_kv_w512, cache_kv_w2048)
    F2 = 2 * D_FF
    TM = 1024
    tiles_per_seq = S // TM

    xp = x_prompt.reshape(B * S, D_MODEL)
    xs = x_sample.reshape(NB * T, D_MODEL)
    ret_p, ret_s, conv_p, conv_s = [], [], [], []
    kv_p = [[] for _ in range(N_GROUPS)]
    kv_s = [[] for _ in range(N_GROUPS)]
    for l in range(depth):
        ret_tiles = 4 * RET_WIDTH // CAST_TN
        att_tiles = ATT_COLS // CAST_TN
        in_tiles = IN_COLS // CAST_TN
        w_main_l = _cast_cols(w_in[l], lambda n: (n + ret_tiles + att_tiles) % in_tiles, MAIN_COLS // CAST_TN,
                              CAST_TN, "cast_w_main")
        w_att_l = _cast_cols(w_in[l], lambda n: n + ret_tiles, att_tiles, CAST_TN, "cast_w_att")
        half_chunks = D_FF // UP_CHUNK
        w_up_l = _cast_cols(w_up[l], lambda n: n // 2 + (n % 2) * half_chunks, 2 * half_chunks, UP_CHUNK,
                            "cast_w_up")
        w_ret_o_l = w_ret_o[l].astype(BF16)
        w_att_o_l = w_att_o[l].astype(BF16)
        w_o_l = w_o[l].astype(BF16)
        w_down_l = w_down[l].astype(BF16)

        proj, h = _inproj_main(xp, g_mix[l], w_main_l, tm=TM)
        yret, st = _ret_prompt(proj.reshape(B, S, MAIN_COLS), gn_ret[l])
        ret_p.append(st)
        os_, lses = [], []
        for g, (win, _) in enumerate(ATT_GROUPS):
            qkv, kvf = _inproj_att(h, w_att_l, g, B=B, S=S, tm=512)
            o, lse = _att_prompt(qkv, g)
            os_.append(o)
            lses.append(lse)
            kv_p[g].append(kvf.reshape(B, min(win, S), 2, ATT_HG, ATT_HD))
        x1, h2 = _epilogue(xp, proj, yret.reshape(B * S, RET_WIDTH), os_, lses,
                           w_ret_o_l, w_att_o_l, w_o_l, g_ffn[l], tm=256, seq_len=S)
        conv_w_l, conv_b_l = _halve_b_half(conv_w[l]), _halve_b_half(conv_b[l])
        gated, tail_a, tail_b, *shifted = _up_gate(h2, w_up_l, conv_w_l, conv_b_l,
                                                   [_cache_rows(c[l]) for c in caches], T,
                                                   tm=TM, tiles_per_seq=tiles_per_seq)
        utail = jnp.concatenate([tail_a, tail_b], axis=-1)
        conv_p.append(utail[tiles_per_seq - 1::tiles_per_seq, 8 - (CONV_W - 1):])
        xp = _down(gated, x1, w_down_l, g_final, tm=256, final_norm=l == depth - 1)

        proj_s = _inproj_sample(xs, g_mix[l], w_main_l, "inproj_sample_main")
        att_s = _inproj_sample(xs, g_mix[l], w_att_l, "inproj_sample_att")
        yret_s, st_s = _ret_sample(proj_s, state_ret[l], gn_ret[l], T)
        ret_s.append(st_s)
        os_, lses = [], []
        for g in range(N_GROUPS):
            o, lse, new_cache = _att_sample(att_s, caches[g][l], shifted[g], g, T)
            os_.append(o)
            lses.append(lse)
            kv_s[g].append(new_cache)
        x1s, h2s = _epilogue(xs, proj_s, yret_s, os_, lses, w_ret_o_l, w_att_o_l, w_o_l, g_ffn[l], tm=NB * T)
        xs, cv = _ffn_sample(h2s, x1s, state_conv[l], w_up_l, conv_w_l, conv_b_l, w_down_l,
                             g_final, T, final_norm=l == depth - 1)
        conv_s.append(cv)

    return (xp.reshape(B, S, D_MODEL), xs.reshape(NB, T, D_MODEL),
            jnp.stack(ret_p), jnp.stack(ret_s),
            jnp.stack(kv_p[0]), jnp.stack(kv_s[0]),
            jnp.stack(kv_p[1]), jnp.stack(kv_s[1]),
            jnp.stack(kv_p[2]), jnp.stack(kv_s[2]),
            jnp.stack(conv_p), jnp.stack(conv_s))
```

```python
import functools
import math

import jax
import jax.numpy as jnp
from jax import lax
from jax.experimental import pallas as pl
from jax.experimental.pallas import tpu as pltpu

D_MODEL = 2048
RET_HEADS = 8
RET_DK = 128
RET_DV = 128
RET_WIDTH = RET_HEADS * RET_DV
RET_CHUNK = 128
ATT_GROUPS = ((128, 1), (512, 4), (2048, 16))
N_GROUPS = 3
ATT_HG = 4
ATT_HD = 128
ATT_GW = ATT_HG * ATT_HD
ATT_SPAN = 128
ATT_BLK = 128
ALIBI_MAX = 8.0
D_FF = 5632
CONV_W = 3
EPS = 1e-6
NEG = -1e30
IN_COLS = 4 * RET_WIDTH + 3 * N_GROUPS * ATT_GW + 2 * D_MODEL

GATE_COLS = 2 * D_MODEL
RET_COL0 = GATE_COLS
MAIN_COLS = GATE_COLS + 4 * RET_WIDTH
ATT_COLS = 3 * N_GROUPS * ATT_GW
CAST_TN = 512

LANES = 128
MIB = 1024 * 1024
BF16 = jnp.bfloat16
F32 = jnp.float32


def _params(semantics, vmem_mib, flags=None):
    return pltpu.CompilerParams(dimension_semantics=semantics, vmem_limit_bytes=vmem_mib * MIB, flags=flags)


def _dot(a, b):
    return jnp.dot(a, b, preferred_element_type=F32)


def _dot_nt(a, b):
    return lax.dot_general(a, b, (((1,), (1,)), ((), ())), preferred_element_type=F32)


def _rmsnorm_rows(x, g):
    ms = jnp.mean(x * x, axis=-1, keepdims=True)
    return x * lax.rsqrt(ms + EPS) * g


def _cast_cols_kernel(w_ref, o_ref):
    o_ref[...] = w_ref[...].astype(o_ref.dtype)


def _cast_cols(w, src_tile, n_tiles, tn, name):
    K = w.shape[0]
    return pl.pallas_call(
        _cast_cols_kernel,
        grid=(n_tiles,),
        in_specs=[pl.BlockSpec((K, tn), lambda n: (0, src_tile(n)))],
        out_specs=pl.BlockSpec((K, tn), lambda n: (0, n)),
        out_shape=jax.ShapeDtypeStruct((K, n_tiles * tn), BF16),
        compiler_params=_params(("arbitrary",), 32),
        name=name,
    )(w)


def _inproj_sample_kernel(x_ref, g_ref, w_ref, proj_ref, h_scr):
    @pl.when(pl.program_id(0) == 0)
    def _():
        h_scr[...] = _rmsnorm_rows(x_ref[...], g_ref[...]).astype(BF16)

    proj_ref[...] = _dot(h_scr[...], w_ref[...])


def _inproj_sample(x2d, g, w_bf16, name):
    M = x2d.shape[0]
    N = w_bf16.shape[1]
    tn = 512
    return pl.pallas_call(
        _inproj_sample_kernel,
        grid=(N // tn,),
        in_specs=[
            pl.BlockSpec((M, D_MODEL), lambda n: (0, 0)),
            pl.BlockSpec((1, D_MODEL), lambda n: (0, 0)),
            pl.BlockSpec((D_MODEL, tn), lambda n: (0, n)),
        ],
        out_specs=pl.BlockSpec((M, tn), lambda n: (0, n)),
        out_shape=jax.ShapeDtypeStruct((M, N), F32),
        scratch_shapes=[pltpu.VMEM((M, D_MODEL), BF16)],
        compiler_params=_params(("arbitrary",), 32),
        name=name,
    )(x2d, g.reshape(1, D_MODEL), w_bf16)


def _inproj_main_kernel(x_ref, g_ref, w_ref, main_ref, h_ref):
    @pl.when(pl.program_id(1) == 0)
    def _():
        h_ref[...] = _rmsnorm_rows(x_ref[...], g_ref[...]).astype(BF16)

    main_ref[...] = _dot(h_ref[...], w_ref[...]).astype(BF16)


def _inproj_main(x2d, g, w_main, *, tm):
    M = x2d.shape[0]
    N = w_main.shape[1]
    tn = 1024
    return pl.pallas_call(
        _inproj_main_kernel,
        grid=(M // tm, N // tn),
        in_specs=[
            pl.BlockSpec((tm, D_MODEL), lambda m, n: (m, 0)),
            pl.BlockSpec((1, D_MODEL), lambda m, n: (0, 0)),
            pl.BlockSpec((D_MODEL, tn), lambda m, n: (0, n)),
        ],
        out_specs=[pl.BlockSpec((tm, tn), lambda m, n: (m, n)), pl.BlockSpec((tm, D_MODEL), lambda m, n: (m, 0))],
        out_shape=[jax.ShapeDtypeStruct((M, N), BF16), jax.ShapeDtypeStruct((M, D_MODEL), BF16)],
        compiler_params=_params(("arbitrary", "arbitrary"), 48),
        name="inproj_main",
    )(x2d, g.reshape(1, D_MODEL), w_main)


def _inproj_att_kernel(h_ref, w_ref, a_ref, kv_ref, *scratch, dil):
    acc = _dot(h_ref[...], w_ref[...])
    slots = 2 * ATT_HG
    rows = kv_ref.shape[0] // slots
    for slot in range(slots):
        cols = slice(ATT_GW + slot * ATT_HD, ATT_GW + (slot + 1) * ATT_HD)
        kv_ref[pl.ds(slot, rows, stride=slots), :] = acc[acc.shape[0] - rows:, cols]
    if dil == 1:
        a_ref[0] = acc.astype(BF16)
    else:
        acc_scr, = scratch
        rows = acc_scr.shape[1] // dil
        for cb in range(acc_scr.shape[0]):
            lanes = slice(cb * LANES, (cb + 1) * LANES)
            acc_scr[cb] = acc[:, lanes]
            for r in range(dil):
                a_ref[r, :, lanes] = acc_scr[cb, pl.ds(r, rows, stride=dil), :].astype(BF16)


def _inproj_att(h, w_att, group, *, B, S, tm):
    M = h.shape[0]
    tn = 3 * ATT_GW
    tps = S // tm
    win, dil = ATT_GROUPS[group]
    win = min(win, S)
    rows = min(win, tm)
    first = tps - win // rows
    slots = 2 * ATT_HG
    return pl.pallas_call(
        functools.partial(_inproj_att_kernel, dil=dil),
        grid=(M // tm,),
        in_specs=[
            pl.BlockSpec((tm, D_MODEL), lambda m: (m, 0)),
            pl.BlockSpec((D_MODEL, tn), lambda m: (0, group)),
        ],
        out_specs=[
            pl.BlockSpec((None, dil, tm // dil, tn), lambda m: (m // tps, 0, m % tps, 0)),
            pl.BlockSpec((None, rows * slots, ATT_HD), lambda m: (m // tps, jnp.maximum(m % tps - first, 0), 0)),
        ],
        out_shape=[
            jax.ShapeDtypeStruct((B, dil, S // dil, tn), BF16),
            jax.ShapeDtypeStruct((B, win * slots, ATT_HD), F32),
        ],
        scratch_shapes=[] if dil == 1 else [pltpu.VMEM((tn // LANES, tm, LANES), F32)],
        compiler_params=_params(("arbitrary",), 48),
        name=f"inproj_att_g{group}",
    )(h, w_att)


def _ret_tables(lq):
    lg = jnp.log(1.0 - jnp.exp2(-5.0 - jnp.arange(RET_HEADS, dtype=F32)))
    scale = RET_DK ** -0.5
    i = jnp.arange(lq, dtype=F32)
    j = jnp.arange(RET_CHUNK, dtype=F32)
    diff = i[:, None] - j[None, :]
    live = (diff >= 0) & (j[None, :] < lq)
    dm = jnp.where(live[None], jnp.exp(jnp.maximum(diff, 0.0)[None] * lg[:, None, None]), 0.0) * scale
    qd = jnp.exp((i[None, :] + 1.0) * lg[:, None])
    qd = jnp.broadcast_to(qd[:, :, None], (RET_HEADS, lq, RET_DV))
    kd = jnp.where(j[None, :] < lq, jnp.exp(jnp.maximum(lq - 1.0 - j, 0.0)[None, :] * lg[:, None]), 0.0) * scale
    kd = jnp.broadcast_to(kd[:, :, None], (RET_HEADS, RET_CHUNK, RET_DK))
    gl = jnp.exp(lq * lg)
    return dm.astype(F32), qd.astype(F32), kd.astype(F32), gl.astype(F32)


def _ret_heads(qs, ks, vs, sts, dms, qds, kds, gls):
    ss = [(_dot_nt(q, k) * dm).astype(BF16) for q, k, dm in zip(qs, ks, dms)]
    crosses = [_dot(q, st.astype(BF16)) for q, st in zip(qs, sts)]
    kts = [(k.astype(F32) * kd).T.astype(BF16) for k, kd in zip(ks, kds)]
    inners = [_dot(s, v) for s, v in zip(ss, vs)]
    updates = [_dot(kt, v) for kt, v in zip(kts, vs)]
    ys = [inner + cross * qd for inner, cross, qd in zip(inners, crosses, qds)]
    new_sts = [st * gl + upd for st, gl, upd in zip(sts, gls, updates)]
    return ys, new_sts


def _groupnorm_gates(ys, gns, rgs):
    mus = [jnp.mean(y, axis=-1, keepdims=True) for y in ys]
    ycs = [y - mu for y, mu in zip(ys, mus)]
    rstds = [lax.rsqrt(jnp.mean(yc * yc, axis=-1, keepdims=True) + EPS) for yc in ycs]
    return [(rg * jax.nn.sigmoid(rg)) * ((yc * rstd) * gn) for yc, rstd, gn, rg in zip(ycs, rstds, gns, rgs)]


def _head_slices():
    return [slice(h * RET_DK, (h + 1) * RET_DK) for h in range(RET_HEADS)]


def _ret_prompt_kernel(gl_ref, q_ref, k_ref, v_ref, rg_ref, dm_ref, qd_ref, kd_ref, gn_ref, y_ref, st_ref):
    c = pl.program_id(1)

    @pl.when(c == 0)
    def _():
        st_ref[...] = jnp.zeros_like(st_ref)

    sls = _head_slices()
    heads = range(RET_HEADS)
    sts = [st_ref[h] for h in heads]
    for i in range(q_ref.shape[0] // RET_CHUNK):
        rows = slice(i * RET_CHUNK, (i + 1) * RET_CHUNK)
        ys, sts = _ret_heads([q_ref[rows, sl] for sl in sls], [k_ref[rows, sl] for sl in sls],
                             [v_ref[rows, sl] for sl in sls], sts, [dm_ref[h] for h in heads],
                             [qd_ref[h] for h in heads], [kd_ref[h] for h in heads], [gl_ref[h] for h in heads])
        outs = _groupnorm_gates(ys, [gn_ref[:, sl] for sl in sls], [rg_ref[rows, sl].astype(F32) for sl in sls])
        for sl, out in zip(sls, outs):
            y_ref[rows, sl] = out.astype(y_ref.dtype)
    for h in heads:
        st_ref[h] = sts[h]


RET_STEP_CHUNKS = 2


def _ret_prompt(proj3, gn_ret):
    B, S, _ = proj3.shape
    L = RET_CHUNK
    dm, qd, kd, gl = _ret_tables(L)
    cb = RET_COL0 // RET_WIDTH
    rows = L * math.gcd(RET_STEP_CHUNKS, S // L)
    col = lambda k: pl.BlockSpec((None, rows, RET_WIDTH), lambda b, c, k=k: (b, c, cb + k))
    tab = pl.BlockSpec((RET_HEADS, L, RET_DK), lambda b, c: (0, 0, 0))
    return pl.pallas_call(
        _ret_prompt_kernel,
        grid=(B, S // rows),
        in_specs=[
            pl.BlockSpec(memory_space=pltpu.SMEM),
            col(0), col(1), col(2), col(3), tab, tab, tab,
            pl.BlockSpec((1, RET_WIDTH), lambda b, c: (0, 0)),
        ],
        out_specs=[
            pl.BlockSpec((None, rows, RET_WIDTH), lambda b, c: (b, c, 0)),
            pl.BlockSpec((None, RET_HEADS, RET_DK, RET_DV), lambda b, c: (b, 0, 0, 0)),
        ],
        out_shape=[
            jax.ShapeDtypeStruct((B, S, RET_WIDTH), BF16),
            jax.ShapeDtypeStruct((B, RET_HEADS, RET_DK, RET_DV), F32),
        ],
        compiler_params=_params(("arbitrary", "arbitrary"), 32),
        name="ret_prompt",
    )(gl, proj3, proj3, proj3, proj3, dm, qd, kd, gn_ret.reshape(1, RET_WIDTH))


def _pad_rows(x, rows):
    x = x.astype(F32)
    return jnp.concatenate([x, jnp.zeros((rows - x.shape[0], x.shape[1]), F32)], axis=0)


def _ret_sample_kernel(gl_ref, q_ref, k_ref, v_ref, rg_ref, st_in_ref, dm_ref, qd_ref, kd_ref, gn_ref,
                       y_ref, st_ref):
    sls = _head_slices()
    heads = range(RET_HEADS)
    ys, new_sts = _ret_heads([q_ref[:, sl].astype(BF16) for sl in sls],
                             [_pad_rows(k_ref[:, sl], RET_CHUNK).astype(BF16) for sl in sls],
                             [_pad_rows(v_ref[:, sl], RET_CHUNK).astype(BF16) for sl in sls],
                             [st_in_ref[h] for h in heads], [dm_ref[h] for h in heads], [qd_ref[h] for h in heads],
                             [kd_ref[h] for h in heads], [gl_ref[h] for h in heads])
    for h in heads:
        st_ref[h] = new_sts[h]
    outs = _groupnorm_gates(ys, [gn_ref[:, sl] for sl in sls], [rg_ref[:, sl].astype(F32) for sl in sls])
    for sl, out in zip(sls, outs):
        y_ref[:, sl] = out.astype(y_ref.dtype)


def _ret_sample(proj_s, state, gn_ret, T):
    nb = state.shape[0]
    dm, qd, kd, gl = _ret_tables(T)
    cb = RET_COL0 // RET_WIDTH
    col = lambda k: pl.BlockSpec((T, RET_WIDTH), lambda b, k=k: (b, cb + k))
    st_spec = pl.BlockSpec((None, RET_HEADS, RET_DK, RET_DV), lambda b: (b, 0, 0, 0))
    return pl.pallas_call(
        _ret_sample_kernel,
        grid=(nb,),
        in_specs=[
            pl.BlockSpec(memory_space=pltpu.SMEM),
            col(0), col(1), col(2), col(3), st_spec,
            pl.BlockSpec((RET_HEADS, T, RET_CHUNK), lambda b: (0, 0, 0)),
            pl.BlockSpec((RET_HEADS, T, RET_DV), lambda b: (0, 0, 0)),
            pl.BlockSpec((RET_HEADS, RET_CHUNK, RET_DK), lambda b: (0, 0, 0)),
            pl.BlockSpec((1, RET_WIDTH), lambda b: (0, 0)),
        ],
        out_specs=[pl.BlockSpec((T, RET_WIDTH), lambda b: (b, 0)), st_spec],
        out_shape=[
            jax.ShapeDtypeStruct((nb * T, RET_WIDTH), F32),
            jax.ShapeDtypeStruct(state.shape, F32),
        ],
        compiler_params=_params(("arbitrary",), 32),
        name="ret_sample",
    )(gl, proj_s, proj_s, proj_s, proj_s, state, dm, qd, kd, gn_ret.reshape(1, RET_WIDTH))


def _alibi_slope(g, h):
    n = N_GROUPS * ATT_HG
    return 2.0 ** (-ALIBI_MAX * (g * ATT_HG + h + 1) / n)


ATT_SUB = 8


def _att_prompt_kernel(q_ref, kp_ref, kc_ref, vp_ref, vc_ref, o_ref, lse_ref, k_scr, v_scr, bias_scr,
                       *, group, dil, nsub):
    j = pl.program_id(2)
    blk = ATT_BLK
    hd = ATT_HD
    k_scr[0:blk] = kp_ref[...]
    k_scr[blk:] = kc_ref[...]
    ones = jnp.ones((blk * (nsub + 1), hd), BF16)
    for h in range(ATT_HG):
        v_scr[0:blk, 2 * h * hd:(2 * h + 1) * hd] = vp_ref[:, h * hd:(h + 1) * hd]
        v_scr[blk:, 2 * h * hd:(2 * h + 1) * hd] = vc_ref[:, h * hd:(h + 1) * hd]
        v_scr[:, (2 * h + 1) * hd:(2 * h + 2) * hd] = ones

    r = lax.broadcasted_iota(jnp.int32, (blk, 2 * blk), 0)
    c = lax.broadcasted_iota(jnp.int32, (blk, 2 * blk), 1)
    dist = blk + r - c
    in_band = (dist >= 0) & (dist <= ATT_SPAN)
    for h in range(ATT_HG):
        bias_scr[h] = jnp.where(in_band, (-_alibi_slope(group, h) * dil) * dist.astype(F32), NEG)
    no_prev = jnp.where((c < blk) & (j == 0), NEG, 0.0)

    scale = hd ** -0.5
    for i in range(nsub):
        rows = slice(i * blk, (i + 1) * blk)
        keys = slice(i * blk, (i + 2) * blk)
        heads = range(ATT_HG)
        ss = [_dot_nt(q_ref[rows, h * hd:(h + 1) * hd], k_scr[keys, h * hd:(h + 1) * hd]) * scale + bias_scr[h]
              for h in heads]
        if i == 0:
            ss = [s + no_prev for s in ss]
        ms = [s.max(axis=-1, keepdims=True) for s in ss]
        ps = [jnp.exp(s - m).astype(BF16) for s, m in zip(ss, ms)]
        rs = [_dot(p, v_scr[keys, 2 * h * hd:(2 * h + 2) * hd]) for h, p in zip(heads, ps)]
        for h, res, m in zip(heads, rs, ms):
            l = res[:, hd:]
            o_ref[rows, h * hd:(h + 1) * hd] = (res[:, :hd] / l).astype(o_ref.dtype)
            lse_ref[rows, h * hd:(h + 1) * hd] = m + jnp.log(l)


def _att_prompt(qkv, group):
    B, dil, L, _ = qkv.shape
    nsub = math.gcd(ATT_SUB, L // ATT_BLK)
    rows = nsub * ATT_BLK
    cur = lambda which: pl.BlockSpec((None, None, rows, ATT_GW), lambda b, r, j: (b, r, j, which))
    prev = lambda which: pl.BlockSpec(
        (None, None, ATT_BLK, ATT_GW), lambda b, r, j: (b, r, jnp.maximum(j * nsub - 1, 0), which))
    out_spec = pl.BlockSpec((None, None, rows, ATT_GW), lambda b, r, j: (b, r, j, 0))
    return pl.pallas_call(
        functools.partial(_att_prompt_kernel, group=group, dil=dil, nsub=nsub),
        grid=(B, dil, L // rows),
        in_specs=[cur(0), prev(1), cur(1), prev(2), cur(2)],
        out_specs=[out_spec, out_spec],
        out_shape=[
            jax.ShapeDtypeStruct((B, dil, L, ATT_GW), BF16),
            jax.ShapeDtypeStruct((B, dil, L, ATT_GW), F32),
        ],
        scratch_shapes=[
            pltpu.VMEM((rows + ATT_BLK, ATT_GW), BF16),
            pltpu.VMEM((rows + ATT_BLK, 2 * ATT_GW), BF16),
            pltpu.VMEM((ATT_HG, ATT_BLK, 2 * ATT_BLK), F32),
        ],
        compiler_params=_params(("arbitrary", "arbitrary", "arbitrary"), 32),
        name=f"att_prompt_g{group}",
    )(qkv, qkv, qkv, qkv, qkv)


def _att_sample_kernel(q_ref, k_ref, v_ref, c_ref, shifted_ref, o_ref, lse_ref, nc_ref, *, group, dil, wb, T):
    del shifted_ref
    slots = 2 * ATT_HG
    t = lax.broadcasted_iota(jnp.int32, (T, wb), 0)
    c = lax.broadcasted_iota(jnp.int32, (T, wb), 1)
    dist_c = wb + t - c
    valid_c = ((dist_c & (dil - 1)) == 0) & (dist_c <= ATT_SPAN * dil)
    tn = lax.broadcasted_iota(jnp.int32, (T, ATT_BLK), 0)
    cn = lax.broadcasted_iota(jnp.int32, (T, ATT_BLK), 1)
    dist_n = tn - cn
    valid_n = (dist_n >= 0) & ((dist_n & (dil - 1)) == 0) & (cn < T)
    scale = ATT_HD ** -0.5
    heads = range(ATT_HG)
    sls = [slice(h * ATT_HD, (h + 1) * ATT_HD) for h in heads]
    for h, sl in zip(heads, sls):
        nc_ref[pl.ds(h, T, stride=slots), :] = k_ref[:, sl]
        nc_ref[pl.ds(ATT_HG + h, T, stride=slots), :] = v_ref[:, sl]
    qs = [q_ref[:, sl].astype(BF16) for sl in sls]
    kcs = [c_ref[pl.ds(h, wb, stride=slots), :].astype(BF16) for h in heads]
    vcs = [c_ref[pl.ds(ATT_HG + h, wb, stride=slots), :].astype(BF16) for h in heads]
    kns = [_pad_rows(k_ref[:, sl], ATT_BLK).astype(BF16) for sl in sls]
    vns = [_pad_rows(v_ref[:, sl], ATT_BLK).astype(BF16) for sl in sls]
    bias_c = jnp.where(valid_c, dist_c.astype(F32), -NEG)
    bias_n = jnp.where(valid_n, dist_n.astype(F32), -NEG)
    scs = [_dot_nt(q, kc) * scale - _alibi_slope(group, h) * bias_c for h, q, kc in zip(heads, qs, kcs)]
    sns = [_dot_nt(q, kn) * scale - _alibi_slope(group, h) * bias_n for h, q, kn in zip(heads, qs, kns)]
    ms = [jnp.maximum(sc.max(axis=-1, keepdims=True), sn.max(axis=-1, keepdims=True)) for sc, sn in zip(scs, sns)]
    pcs = [jnp.exp(sc - m) for sc, m in zip(scs, ms)]
    pns = [jnp.exp(sn - m) for sn, m in zip(sns, ms)]
    ls = [pc.sum(axis=-1, keepdims=True) + pn.sum(axis=-1, keepdims=True) for pc, pn in zip(pcs, pns)]
    accs = [_dot(pc.astype(BF16), vc) + _dot(pn.astype(BF16), vn) for pc, pn, vc, vn in zip(pcs, pns, vcs, vns)]
    for sl, acc, l, m in zip(sls, accs, ls, ms):
        o_ref[:, sl] = (acc / l).astype(o_ref.dtype)
        lse_ref[:, sl] = jnp.broadcast_to(m + jnp.log(l), (T, ATT_HD))


def _cache_rows(cache):
    return cache.reshape(cache.shape[0], cache.shape[1] * 2 * ATT_HG, ATT_HD)


def _att_sample(proj_s, cache, shifted, group, T):
    nb, wb = cache.shape[0], cache.shape[1]
    assert wb > T and wb % T == 0
    _, dil = ATT_GROUPS[group]
    c0 = 3 * group
    slots = 2 * ATT_HG
    col = lambda k: pl.BlockSpec((T, ATT_GW), lambda b, k=k: (b, c0 + k))
    out_spec = pl.BlockSpec((T, ATT_GW), lambda b: (b, 0))
    o, lse, new_cache = pl.pallas_call(
        functools.partial(_att_sample_kernel, group=group, dil=dil, wb=wb, T=T),
        grid=(nb,),
        in_specs=[col(0), col(1), col(2), pl.BlockSpec((None, wb * slots, ATT_HD), lambda b: (b, 0, 0)),
                  pl.BlockSpec(memory_space=pl.ANY)],
        out_specs=[out_spec, out_spec, pl.BlockSpec((None, T * slots, ATT_HD), lambda b: (b, wb // T - 1, 0))],
        out_shape=[
            jax.ShapeDtypeStruct((nb * T, ATT_GW), F32),
            jax.ShapeDtypeStruct((nb * T, ATT_GW), F32),
            jax.ShapeDtypeStruct(shifted.shape, F32),
        ],
        input_output_aliases={4: 2},
        compiler_params=_params(("arbitrary",), 48),
        name=f"att_sample_g{group}",
    )(proj_s, proj_s, proj_s, _cache_rows(cache), shifted)
    return o, lse, new_cache.reshape(cache.shape)


def _epilogue_kernel(x_ref, gr_ref, ga_ref, yret_ref, o0_ref, o1_ref, o2_ref, l0_ref, l1_ref, l2_ref,
                     wr_ref, wa_ref, wo_ref, gf_ref, x1_ref, h2_ref, *scratch, by_residue):
    def position_order(ref, scr):
        dil = ref.shape[0]
        if dil == 1:
            return ref[0].astype(F32)
        for r in range(dil):
            slab = ref[r].astype(F32)
            for cb in range(scr.shape[0]):
                scr[cb, pl.ds(r, ref.shape[1], stride=dil), :] = slab[:, cb * LANES:(cb + 1) * LANES]
        return jnp.concatenate([scr[cb] for cb in range(scr.shape[0])], axis=-1)

    if by_residue:
        o0, o1, o2 = (position_order(ref, scr) for ref, scr in zip((o0_ref, o1_ref, o2_ref), scratch[0:3]))
        l0, l1, l2 = (position_order(ref, scr) for ref, scr in zip((l0_ref, l1_ref, l2_ref), scratch[3:6]))
    else:
        o0, o1, o2 = o0_ref[...], o1_ref[...], o2_ref[...]
        l0, l1, l2 = l0_ref[...], l1_ref[...], l2_ref[...]
    m = jnp.maximum(jnp.maximum(l0, l1), l2)
    e0, e1, e2 = jnp.exp(l0 - m), jnp.exp(l1 - m), jnp.exp(l2 - m)
    inv = 1.0 / (e0 + e1 + e2)
    o = (e0 * inv) * o0 + (e1 * inv) * o1 + (e2 * inv) * o2
    att = _dot(o.astype(BF16), wa_ref[...])
    ret = _dot(yret_ref[...].astype(BF16), wr_ref[...])
    merged = (jax.nn.sigmoid(gr_ref[...].astype(F32)) * ret + jax.nn.sigmoid(ga_ref[...].astype(F32)) * att)
    x1 = x_ref[...] + _dot(merged.astype(BF16), wo_ref[...])
    x1_ref[...] = x1
    h2_ref[...] = _rmsnorm_rows(x1, gf_ref[...]).astype(BF16)


def _epilogue(x2d, proj2, yret, os_, lses, w_ret_o, w_att_o, w_o, g_ffn, *, tm, seq_len=None):
    M = x2d.shape[0]
    row = lambda w: pl.BlockSpec((tm, w), lambda m: (m, 0))
    const = lambda a: pl.BlockSpec(a.shape, lambda m: (0, 0), pipeline_mode=pl.Buffered(1))
    gf = g_ffn.reshape(1, D_MODEL)
    by_residue = seq_len is not None
    scratch = []
    if by_residue:
        tps = seq_len // tm
        att_specs = [pl.BlockSpec((None, a.shape[1], tm // a.shape[1], ATT_GW), lambda m: (m // tps, 0, m % tps, 0))
                     for a in (*os_, *lses)]
        scratch = [pltpu.VMEM((ATT_GW // LANES, tm, LANES), F32)] * 6
    else:
        att_specs = [row(ATT_GW)] * 6
    return pl.pallas_call(
        functools.partial(_epilogue_kernel, by_residue=by_residue),
        grid=(M // tm,),
        in_specs=[
            row(D_MODEL),
            pl.BlockSpec((tm, D_MODEL), lambda m: (m, 0)),
            pl.BlockSpec((tm, D_MODEL), lambda m: (m, 1)),
            row(RET_WIDTH), *att_specs,
            const(w_ret_o), const(w_att_o), const(w_o), const(gf),
        ],
        out_specs=[row(D_MODEL), row(D_MODEL)],
        out_shape=[jax.ShapeDtypeStruct((M, D_MODEL), F32), jax.ShapeDtypeStruct((M, D_MODEL), BF16)],
        scratch_shapes=scratch,
        compiler_params=_params(("arbitrary",), 56),
        name="epilogue",
    )(x2d, proj2, proj2, yret, *os_, *lses, w_ret_o, w_att_o, w_o, gf)


def _gelu_gate(a, half_b):
    return (a * (1.0 + lax.erf(a * math.sqrt(0.5)))) * half_b


def _halve_b_half(x):
    return jnp.concatenate([x[..., :D_FF], 0.5 * x[..., D_FF:]], axis=-1)


def _conv3(u, r1, r2, w_ref, b_ref):
    return b_ref[...] + (r2 * w_ref[0:1, :] + r1 * w_ref[1:2, :] + u * w_ref[2:3, :])


UP_CHUNK = 256


SHIFT_CHUNK_ROWS = 4096


def _shift_plan(cache_rows, T, steps):
    nb, n, _ = cache_rows.shape
    shift = T * 2 * ATT_HG
    body = n - shift
    per_seq = -(-body // SHIFT_CHUNK_ROWS)
    while body % per_seq or (body // per_seq) % 8:
        per_seq += 1
    plan = dict(shift=shift, chunk=body // per_seq, per_seq=per_seq, count=nb * per_seq)
    assert plan["count"] < steps, "the background copy needs one grid step per chunk plus one to drain"
    return plan


def _shift_copy_step(s, src_ref, dst_ref, buf, sems, i, *, shift, chunk, per_seq, count):
    n = src_ref.shape[1]
    tail = pltpu.make_async_copy(src_ref.at[:, pl.ds(n - shift, shift)], dst_ref.at[:, pl.ds(n - shift, shift)],
                                 sems.at[i, 4])

    @pl.when(s == 0)
    def _():
        tail.start()

    @pl.when(s == count)
    def _():
        tail.wait()

    def read(c, slot):
        start = pl.multiple_of(shift + (c % per_seq) * chunk, 8)
        return pltpu.make_async_copy(src_ref.at[c // per_seq, pl.ds(start, chunk)], buf.at[slot], sems.at[i, slot])

    def write(c, slot):
        start = pl.multiple_of((c % per_seq) * chunk, 8)
        return pltpu.make_async_copy(buf.at[slot], dst_ref.at[c // per_seq, pl.ds(start, chunk)],
                                     sems.at[i, 2 + slot])

    slot = s % 2

    @pl.when(s == 0)
    def _():
        read(0, 0).start()

    @pl.when(s < count)
    def _():
        read(s, slot).wait()
        write(s, slot).start()

    @pl.when((s >= 1) & (s <= count))
    def _():
        write(s - 1, 1 - slot).wait()

    @pl.when(s + 1 < count)
    def _():
        read(s + 1, 1 - slot).start()


def _up_gate_kernel(h_ref, w_ref, cwa_ref, cwb_ref, cba_ref, cbb_ref, *rest, tiles_per_seq, shift_plans):
    ng = len(shift_plans)
    caches, rest = rest[:ng], rest[ng:]
    g_ref, ta_ref, tb_ref = rest[:3]
    shifted, rest = rest[3:3 + ng], rest[3 + ng:]
    u_scr, carry_scr = rest[:2]
    bufs, (sems,) = rest[2:2 + ng], rest[2 + ng:]
    m = pl.program_id(0)
    f = pl.program_id(1)
    tm = h_ref.shape[0]

    step = m * pl.num_programs(1) + f
    for i, plan in enumerate(shift_plans):
        _shift_copy_step(step, caches[i], shifted[i], bufs[i], sems, i, **plan)

    @pl.when(m % tiles_per_seq == 0)
    def _():
        carry_scr[f] = jnp.zeros(carry_scr.shape[1:], F32)

    u_scr[0:8, :] = carry_scr[f]
    u_scr[8:, :] = _dot(h_ref[...], w_ref[...])
    carry_scr[f] = u_scr[tm:, :]
    for c in range(g_ref.shape[1] // UP_CHUNK):
        cols = slice(c * UP_CHUNK, (c + 1) * UP_CHUNK)

        def conv(half, cw_ref, cb_ref, tail_ref):
            pcols = slice((2 * c + half) * UP_CHUNK, (2 * c + half + 1) * UP_CHUNK)
            tail_ref[:, cols] = u_scr[tm:, pcols]
            u, r1, r2 = u_scr[8:, pcols], u_scr[7:tm + 7, pcols], u_scr[6:tm + 6, pcols]
            return cb_ref[:, cols] + (r2 * cw_ref[0:1, cols] + r1 * cw_ref[1:2, cols] + u * cw_ref[2:3, cols])

        a = conv(0, cwa_ref, cba_ref, ta_ref)
        b = conv(1, cwb_ref, cbb_ref, tb_ref)
        g_ref[:, cols] = _gelu_gate(a, b).astype(BF16)


def _up_gate(h2, w_up_ab, conv_w, conv_b, caches, T, *, tm, tiles_per_seq):
    M = h2.shape[0]
    tf = 512
    nf = D_FF // tf
    cb = conv_b.reshape(1, 2 * D_FF)
    grid = (M // tm, nf)
    plans = tuple(_shift_plan(c, T, grid[0] * grid[1]) for c in caches)
    any_spec = pl.BlockSpec(memory_space=pl.ANY)
    return pl.pallas_call(
        functools.partial(_up_gate_kernel, tiles_per_seq=tiles_per_seq, shift_plans=plans),
        grid=grid,
        in_specs=[
            pl.BlockSpec((tm, D_MODEL), lambda m, f: (m, 0)),
            pl.BlockSpec((D_MODEL, 2 * tf), lambda m, f: (0, f)),
            pl.BlockSpec((CONV_W, tf), lambda m, f: (0, f)),
            pl.BlockSpec((CONV_W, tf), lambda m, f: (0, f + nf)),
            pl.BlockSpec((1, tf), lambda m, f: (0, f)),
            pl.BlockSpec((1, tf), lambda m, f: (0, f + nf)),
            *[any_spec for _ in caches],
        ],
        out_specs=[
            pl.BlockSpec((tm, tf), lambda m, f: (m, f)),
            pl.BlockSpec((None, 8, tf), lambda m, f: (m, 0, f)),
            pl.BlockSpec((None, 8, tf), lambda m, f: (m, 0, f)),
            *[any_spec for _ in caches],
        ],
        out_shape=[
            jax.ShapeDtypeStruct((M, D_FF), BF16),
            jax.ShapeDtypeStruct((M // tm, 8, D_FF), F32),
            jax.ShapeDtypeStruct((M // tm, 8, D_FF), F32),
            *[jax.ShapeDtypeStruct(c.shape, F32) for c in caches],
        ],
        scratch_shapes=[
            pltpu.VMEM((tm + 8, 2 * tf), F32),
            pltpu.VMEM((nf, 8, 2 * tf), F32),
            *[pltpu.VMEM((2, p["chunk"], ATT_HD), F32) for p in plans],
            pltpu.SemaphoreType.DMA((len(caches), 5)),
        ],
        compiler_params=_params(("arbitrary", "arbitrary"), 56),
        name="up_gate",
    )(h2, w_up_ab, conv_w, conv_w, cb, cb, *caches)


def _down_kernel(g_ref, wd_ref, x1_ref, gfin_ref, y_ref, *, final_norm):
    y = x1_ref[...] + _dot(g_ref[...], wd_ref[...])
    y_ref[...] = _rmsnorm_rows(y, gfin_ref[...]) if final_norm else y


def _down(g, x1, w_down, g_final, *, tm, final_norm):
    M = g.shape[0]
    return pl.pallas_call(
        functools.partial(_down_kernel, final_norm=final_norm),
        grid=(M // tm,),
        in_specs=[
            pl.BlockSpec((tm, D_FF), lambda m: (m, 0)),
            pl.BlockSpec((D_FF, D_MODEL), lambda m: (0, 0), pipeline_mode=pl.Buffered(1)),
            pl.BlockSpec((tm, D_MODEL), lambda m: (m, 0)),
            pl.BlockSpec((1, D_MODEL), lambda m: (0, 0)),
        ],
        out_specs=pl.BlockSpec((tm, D_MODEL), lambda m: (m, 0)),
        out_shape=jax.ShapeDtypeStruct((M, D_MODEL), F32),
        compiler_params=_params(("arbitrary",), 48),
        name="down",
    )(g, w_down, x1, g_final.reshape(1, D_MODEL))


def _ffn_sample_kernel(h_ref, wu_ref, s0a_ref, s0b_ref, s1a_ref, s1b_ref, cwa_ref, cwb_ref,
                       cba_ref, cbb_ref, wd_ref, x1_ref, gfin_ref, y_ref, ua_ref, ub_ref, *, T, final_norm):
    f = pl.program_id(0)
    M, tf = ua_ref.shape
    t = lax.broadcasted_iota(jnp.int32, (M, tf), 0) & (T - 1)
    u2 = _dot(h_ref[...], wu_ref[...])
    nchunk = tf // UP_CHUNK
    half = lambda k: jnp.concatenate(
        [u2[:, (2 * c + k) * UP_CHUNK:(2 * c + k + 1) * UP_CHUNK] for c in range(nchunk)], axis=-1)
    per_row = lambda s_ref: jnp.broadcast_to(s_ref[...][:, None, :], (M // T, T, tf)).reshape(M, tf)

    def conv(u, s0_ref, s1_ref, w_ref, b_ref, u_out_ref):
        u_out_ref[...] = u
        s0, s1 = per_row(s0_ref), per_row(s1_ref)
        r1 = jnp.where(t >= 1, pltpu.roll(u, 1, axis=0), s1)
        r2 = jnp.where(t >= 2, pltpu.roll(u, 2, axis=0), jnp.where(t == 0, s0, s1))
        return _conv3(u, r1, r2, w_ref, b_ref)

    a = conv(half(0), s0a_ref, s1a_ref, cwa_ref, cba_ref, ua_ref)
    b = conv(half(1), s0b_ref, s1b_ref, cwb_ref, cbb_ref, ub_ref)
    part = _dot(_gelu_gate(a, b).astype(BF16), wd_ref[...])

    @pl.when(f == 0)
    def _():
        y_ref[...] = x1_ref[...] + part

    @pl.when(f > 0)
    def _():
        y_ref[...] += part

    if final_norm:
        @pl.when(f == pl.num_programs(0) - 1)
        def _():
            y_ref[...] = _rmsnorm_rows(y_ref[...], gfin_ref[...])


def _ffn_sample(h2, x1, state_conv, w_up_ab, conv_w, conv_b, w_down, g_final, T, final_norm):
    M = h2.shape[0]
    nb = M // T
    tf = 512
    nf = D_FF // tf
    F2 = 2 * D_FF
    assert state_conv.shape[1] == CONV_W - 1 == 2
    s0, s1 = state_conv[:, 0].astype(F32), state_conv[:, 1].astype(F32)
    cb = conv_b.reshape(1, F2)
    gfin = g_final.reshape(1, D_MODEL)
    full = lambda w: pl.BlockSpec((M, w), lambda f: (0, 0))
    ca = lambda r, w=tf: pl.BlockSpec((r, w), lambda f: (0, f))
    cbk = lambda r, w=tf: pl.BlockSpec((r, w), lambda f: (0, f + nf))
    y, ua, ub = pl.pallas_call(
        functools.partial(_ffn_sample_kernel, T=T, final_norm=final_norm),
        grid=(nf,),
        in_specs=[
            full(D_MODEL), ca(D_MODEL, 2 * tf), ca(nb), cbk(nb), ca(nb), cbk(nb),
            ca(CONV_W), cbk(CONV_W), ca(1), cbk(1),
            pl.BlockSpec((tf, D_MODEL), lambda f: (f, 0)),
            full(D_MODEL), pl.BlockSpec((1, D_MODEL), lambda f: (0, 0)),
        ],
        out_specs=[full(D_MODEL), ca(M), ca(M)],
        out_shape=[
            jax.ShapeDtypeStruct((M, D_MODEL), F32),
            jax.ShapeDtypeStruct((M, D_FF), F32),
            jax.ShapeDtypeStruct((M, D_FF), F32),
        ],
        compiler_params=_params(("arbitrary",), 48),
        name="ffn_sample",
    )(h2, w_up_ab, s0, s0, s1, s1, conv_w, conv_w, cb, cb, w_down, x1, gfin)
    u = jnp.concatenate([ua, ub], axis=-1).reshape(nb, T, F2)
    return y, u[:, T - (CONV_W - 1):]


def kernel(x_prompt, x_sample, state_ret, cache_kv_w128, cache_kv_w512, cache_kv_w2048, state_conv, g_mix, w_in,
           gn_ret, w_ret_o, w_att_o, w_o, g_ffn, w_up, conv_w, conv_b, w_down, g_final):
    B, S, _ = x_prompt.shape
    NB, T, _ = x_sample.shape
    depth = w_in.shape[0]
    caches = (cache_kv_w128, cache_kv_w512, cache_kv_w2048)
    F2 = 2 * D_FF
    TM = 1024
    tiles_per_seq = S // TM

    xp = x_prompt.reshape(B * S, D_MODEL)
    xs = x_sample.reshape(NB * T, D_MODEL)
    ret_p, ret_s, conv_p, conv_s = [], [], [], []
    kv_p = [[] for _ in range(N_GROUPS)]
    kv_s = [[] for _ in range(N_GROUPS)]
    for l in range(depth):
        ret_tiles = 4 * RET_WIDTH // CAST_TN
        att_tiles = ATT_COLS // CAST_TN
        in_tiles = IN_COLS // CAST_TN
        w_main_l = _cast_cols(w_in[l], lambda n: (n + ret_tiles + att_tiles) % in_tiles, MAIN_COLS // CAST_TN,
                              CAST_TN, "cast_w_main")
        w_att_l = _cast_cols(w_in[l], lambda n: n + ret_tiles, att_tiles, CAST_TN, "cast_w_att")
        half_chunks = D_FF // UP_CHUNK
        w_up_l = _cast_cols(w_up[l], lambda n: n // 2 + (n % 2) * half_chunks, 2 * half_chunks, UP_CHUNK,
                            "cast_w_up")
        w_ret_o_l = w_ret_o[l].astype(BF16)
        w_att_o_l = w_att_o[l].astype(BF16)
        w_o_l = w_o[l].astype(BF16)
        w_down_l = w_down[l].astype(BF16)

        proj, h = _inproj_main(xp, g_mix[l], w_main_l, tm=TM)
        yret, st = _ret_prompt(proj.reshape(B, S, MAIN_COLS), gn_ret[l])
        ret_p.append(st)
        os_, lses = [], []
        for g, (win, _) in enumerate(ATT_GROUPS):
            qkv, kvf = _inproj_att(h, w_att_l, g, B=B, S=S, tm=512)
            o, lse = _att_prompt(qkv, g)
            os_.append(o)
            lses.append(lse)
            kv_p[g].append(kvf.reshape(B, min(win, S), 2, ATT_HG, ATT_HD))
        x1, h2 = _epilogue(xp, proj, yret.reshape(B * S, RET_WIDTH), os_, lses,
                           w_ret_o_l, w_att_o_l, w_o_l, g_ffn[l], tm=256, seq_len=S)
        conv_w_l, conv_b_l = _halve_b_half(conv_w[l]), _halve_b_half(conv_b[l])
        gated, tail_a, tail_b, *shifted = _up_gate(h2, w_up_l, conv_w_l, conv_b_l,
                                                   [_cache_rows(c[l]) for c in caches], T,
                                                   tm=TM, tiles_per_seq=tiles_per_seq)
        utail = jnp.concatenate([tail_a, tail_b], axis=-1)
        conv_p.append(utail[tiles_per_seq - 1::tiles_per_seq, 8 - (CONV_W - 1):])
        xp = _down(gated, x1, w_down_l, g_final, tm=256, final_norm=l == depth - 1)

        proj_s = _inproj_sample(xs, g_mix[l], w_main_l, "inproj_sample_main")
        att_s = _inproj_sample(xs, g_mix[l], w_att_l, "inproj_sample_att")
        yret_s, st_s = _ret_sample(proj_s, state_ret[l], gn_ret[l], T)
        ret_s.append(st_s)
        os_, lses = [], []
        for g in range(N_GROUPS):
            o, lse, new_cache = _att_sample(att_s, caches[g][l], shifted[g], g, T)
            os_.append(o)
            lses.append(lse)
            kv_s[g].append(new_cache)
        x1s, h2s = _epilogue(xs, proj_s, yret_s, os_, lses, w_ret_o_l, w_att_o_l, w_o_l, g_ffn[l], tm=NB * T)
        xs, cv = _ffn_sample(h2s, x1s, state_conv[l], w_up_l, conv_w_l, conv_b_l, w_down_l,
                             g_final, T, final_norm=l == depth - 1)
        conv_s.append(cv)

    return (xp.reshape(B, S, D_MODEL), xs.reshape(NB, T, D_MODEL),
            jnp.stack(ret_p), jnp.stack(ret_s),
            jnp.stack(kv_p[0]), jnp.stack(kv_s[0]),
            jnp.stack(kv_p[1]), jnp.stack(kv_s[1]),
            jnp.stack(kv_p[2]), jnp.stack(kv_s[2]),
            jnp.stack(conv_p), jnp.stack(conv_s))
```

```python
import functools
import math

import jax
import jax.numpy as jnp
from jax import lax
from jax.experimental import pallas as pl
from jax.experimental.pallas import tpu as pltpu

D_MODEL = 2048
RET_HEADS = 8
RET_DK = 128
RET_DV = 128
RET_WIDTH = RET_HEADS * RET_DV
RET_CHUNK = 128
ATT_GROUPS = ((128, 1), (512, 4), (2048, 16))
N_GROUPS = 3
ATT_HG = 4
ATT_HD = 128
ATT_GW = ATT_HG * ATT_HD
ATT_SPAN = 128
ATT_BLK = 128
ALIBI_MAX = 8.0
D_FF = 5632
CONV_W = 3
EPS = 1e-6
NEG = -1e30
IN_COLS = 4 * RET_WIDTH + 3 * N_GROUPS * ATT_GW + 2 * D_MODEL

GATE_COLS = 2 * D_MODEL
RET_COL0 = GATE_COLS
MAIN_COLS = GATE_COLS + 4 * RET_WIDTH
ATT_COLS = 3 * N_GROUPS * ATT_GW
CAST_TN = 512

LANES = 128
MIB = 1024 * 1024
BF16 = jnp.bfloat16
F32 = jnp.float32


def _params(semantics, vmem_mib, flags=None):
    return pltpu.CompilerParams(dimension_semantics=semantics, vmem_limit_bytes=vmem_mib * MIB, flags=flags)


def _dot(a, b):
    return jnp.dot(a, b, preferred_element_type=F32)


def _dot_nt(a, b):
    return lax.dot_general(a, b, (((1,), (1,)), ((), ())), preferred_element_type=F32)


def _rmsnorm_rows(x, g):
    ms = jnp.mean(x * x, axis=-1, keepdims=True)
    return x * lax.rsqrt(ms + EPS) * g


def _cast_cols_kernel(w_ref, o_ref):
    o_ref[...] = w_ref[...].astype(o_ref.dtype)


def _cast_cols(w, src_tile, n_tiles, tn, name):
    K = w.shape[0]
    return pl.pallas_call(
        _cast_cols_kernel,
        grid=(n_tiles,),
        in_specs=[pl.BlockSpec((K, tn), lambda n: (0, src_tile(n)))],
        out_specs=pl.BlockSpec((K, tn), lambda n: (0, n)),
        out_shape=jax.ShapeDtypeStruct((K, n_tiles * tn), BF16),
        compiler_params=_params(("arbitrary",), 32),
        name=name,
    )(w)


def _inproj_sample_kernel(x_ref, g_ref, w_ref, proj_ref, h_scr):
    @pl.when(pl.program_id(0) == 0)
    def _():
        h_scr[...] = _rmsnorm_rows(x_ref[...], g_ref[...]).astype(BF16)

    proj_ref[...] = _dot(h_scr[...], w_ref[...])


def _inproj_sample(x2d, g, w_bf16, name):
    M = x2d.shape[0]
    N = w_bf16.shape[1]
    tn = 512
    return pl.pallas_call(
        _inproj_sample_kernel,
        grid=(N // tn,),
        in_specs=[
            pl.BlockSpec((M, D_MODEL), lambda n: (0, 0)),
            pl.BlockSpec((1, D_MODEL), lambda n: (0, 0)),
            pl.BlockSpec((D_MODEL, tn), lambda n: (0, n)),
        ],
        out_specs=pl.BlockSpec((M, tn), lambda n: (0, n)),
        out_shape=jax.ShapeDtypeStruct((M, N), F32),
        scratch_shapes=[pltpu.VMEM((M, D_MODEL), BF16)],
        compiler_params=_params(("arbitrary",), 32),
        name=name,
    )(x2d, g.reshape(1, D_MODEL), w_bf16)


def _inproj_main_kernel(x_ref, g_ref, w_ref, main_ref, h_ref):
    @pl.when(pl.program_id(1) == 0)
    def _():
        h_ref[...] = _rmsnorm_rows(x_ref[...], g_ref[...]).astype(BF16)

    main_ref[...] = _dot(h_ref[...], w_ref[...]).astype(BF16)


def _inproj_main(x2d, g, w_main, *, tm):
    M = x2d.shape[0]
    N = w_main.shape[1]
    tn = 1024
    return pl.pallas_call(
        _inproj_main_kernel,
        grid=(M // tm, N // tn),
        in_specs=[
            pl.BlockSpec((tm, D_MODEL), lambda m, n: (m, 0)),
            pl.BlockSpec((1, D_MODEL), lambda m, n: (0, 0)),
            pl.BlockSpec((D_MODEL, tn), lambda m, n: (0, n)),
        ],
        out_specs=[pl.BlockSpec((tm, tn), lambda m, n: (m, n)), pl.BlockSpec((tm, D_MODEL), lambda m, n: (m, 0))],
        out_shape=[jax.ShapeDtypeStruct((M, N), BF16), jax.ShapeDtypeStruct((M, D_MODEL), BF16)],
        compiler_params=_params(("arbitrary", "arbitrary"), 48),
        name="inproj_main",
    )(x2d, g.reshape(1, D_MODEL), w_main)


def _inproj_att_kernel(h_ref, w_ref, a_ref, kv_ref, *scratch, dil):
    acc = _dot(h_ref[...], w_ref[...])
    slots = 2 * ATT_HG
    rows = kv_ref.shape[0] // slots
    for slot in range(slots):
        cols = slice(ATT_GW + slot * ATT_HD, ATT_GW + (slot + 1) * ATT_HD)
        kv_ref[pl.ds(slot, rows, stride=slots), :] = acc[acc.shape[0] - rows:, cols]
    if dil == 1:
        a_ref[0] = acc.astype(BF16)
    else:
        acc_scr, = scratch
        rows = acc_scr.shape[1] // dil
        for cb in range(acc_scr.shape[0]):
            lanes = slice(cb * LANES, (cb + 1) * LANES)
            acc_scr[cb] = acc[:, lanes]
            for r in range(dil):
                a_ref[r, :, lanes] = acc_scr[cb, pl.ds(r, rows, stride=dil), :].astype(BF16)


def _inproj_att(h, w_att, group, *, B, S, tm):
    M = h.shape[0]
    tn = 3 * ATT_GW
    tps = S // tm
    win, dil = ATT_GROUPS[group]
    win = min(win, S)
    rows = min(win, tm)
    first = tps - win // rows
    slots = 2 * ATT_HG
    return pl.pallas_call(
        functools.partial(_inproj_att_kernel, dil=dil),
        grid=(M // tm,),
        in_specs=[
            pl.BlockSpec((tm, D_MODEL), lambda m: (m, 0)),
            pl.BlockSpec((D_MODEL, tn), lambda m: (0, group)),
        ],
        out_specs=[
            pl.BlockSpec((None, dil, tm // dil, tn), lambda m: (m // tps, 0, m % tps, 0)),
            pl.BlockSpec((None, rows * slots, ATT_HD), lambda m: (m // tps, jnp.maximum(m % tps - first, 0), 0)),
        ],
        out_shape=[
            jax.ShapeDtypeStruct((B, dil, S // dil, tn), BF16),
            jax.ShapeDtypeStruct((B, win * slots, ATT_HD), F32),
        ],
        scratch_shapes=[] if dil == 1 else [pltpu.VMEM((tn // LANES, tm, LANES), F32)],
        compiler_params=_params(("arbitrary",), 48),
        name=f"inproj_att_g{group}",
    )(h, w_att)


def _ret_tables(lq):
    lg = jnp.log(1.0 - jnp.exp2(-5.0 - jnp.arange(RET_HEADS, dtype=F32)))
    scale = RET_DK ** -0.5
    i = jnp.arange(lq, dtype=F32)
    j = jnp.arange(RET_CHUNK, dtype=F32)
    diff = i[:, None] - j[None, :]
    live = (diff >= 0) & (j[None, :] < lq)
    dm = jnp.where(live[None], jnp.exp(jnp.maximum(diff, 0.0)[None] * lg[:, None, None]), 0.0) * scale
    qd = jnp.exp((i[None, :] + 1.0) * lg[:, None])
    qd = jnp.broadcast_to(qd[:, :, None], (RET_HEADS, lq, RET_DV))
    kd = jnp.where(j[None, :] < lq, jnp.exp(jnp.maximum(lq - 1.0 - j, 0.0)[None, :] * lg[:, None]), 0.0) * scale
    kd = jnp.broadcast_to(kd[:, :, None], (RET_HEADS, RET_CHUNK, RET_DK))
    gl = jnp.exp(lq * lg)
    return dm.astype(F32), qd.astype(F32), kd.astype(F32), gl.astype(F32)


def _ret_heads(qs, ks, vs, sts, dms, qds, kds, gls):
    ss = [(_dot_nt(q, k) * dm).astype(BF16) for q, k, dm in zip(qs, ks, dms)]
    crosses = [_dot(q, st.astype(BF16)) for q, st in zip(qs, sts)]
    kts = [(k.astype(F32) * kd).T.astype(BF16) for k, kd in zip(ks, kds)]
    inners = [_dot(s, v) for s, v in zip(ss, vs)]
    updates = [_dot(kt, v) for kt, v in zip(kts, vs)]
    ys = [inner + cross * qd for inner, cross, qd in zip(inners, crosses, qds)]
    new_sts = [st * gl + upd for st, gl, upd in zip(sts, gls, updates)]
    return ys, new_sts


def _groupnorm_gates(ys, gns, rgs):
    mus = [jnp.mean(y, axis=-1, keepdims=True) for y in ys]
    ycs = [y - mu for y, mu in zip(ys, mus)]
    rstds = [lax.rsqrt(jnp.mean(yc * yc, axis=-1, keepdims=True) + EPS) for yc in ycs]
    return [(rg * jax.nn.sigmoid(rg)) * ((yc * rstd) * gn) for yc, rstd, gn, rg in zip(ycs, rstds, gns, rgs)]


def _head_slices():
    return [slice(h * RET_DK, (h + 1) * RET_DK) for h in range(RET_HEADS)]


def _ret_prompt_kernel(gl_ref, q_ref, k_ref, v_ref, rg_ref, dm_ref, qd_ref, kd_ref, gn_ref, y_ref, st_ref):
    c = pl.program_id(1)

    @pl.when(c == 0)
    def _():
        st_ref[...] = jnp.zeros_like(st_ref)

    sls = _head_slices()
    heads = range(RET_HEADS)
    sts = [st_ref[h] for h in heads]
    for i in range(q_ref.shape[0] // RET_CHUNK):
        rows = slice(i * RET_CHUNK, (i + 1) * RET_CHUNK)
        ys, sts = _ret_heads([q_ref[rows, sl] for sl in sls], [k_ref[rows, sl] for sl in sls],
                             [v_ref[rows, sl] for sl in sls], sts, [dm_ref[h] for h in heads],
                             [qd_ref[h] for h in heads], [kd_ref[h] for h in heads], [gl_ref[h] for h in heads])
        outs = _groupnorm_gates(ys, [gn_ref[:, sl] for sl in sls], [rg_ref[rows, sl].astype(F32) for sl in sls])
        for sl, out in zip(sls, outs):
            y_ref[rows, sl] = out.astype(y_ref.dtype)
    for h in heads:
        st_ref[h] = sts[h]


RET_STEP_CHUNKS = 2


def _ret_prompt(proj3, gn_ret):
    B, S, _ = proj3.shape
    L = RET_CHUNK
    dm, qd, kd, gl = _ret_tables(L)
    cb = RET_COL0 // RET_WIDTH
    rows = L * math.gcd(RET_STEP_CHUNKS, S // L)
    col = lambda k: pl.BlockSpec((None, rows, RET_WIDTH), lambda b, c, k=k: (b, c, cb + k))
    tab = pl.BlockSpec((RET_HEADS, L, RET_DK), lambda b, c: (0, 0, 0))
    return pl.pallas_call(
        _ret_prompt_kernel,
        grid=(B, S // rows),
        in_specs=[
            pl.BlockSpec(memory_space=pltpu.SMEM),
            col(0), col(1), col(2), col(3), tab, tab, tab,
            pl.BlockSpec((1, RET_WIDTH), lambda b, c: (0, 0)),
        ],
        out_specs=[
            pl.BlockSpec((None, rows, RET_WIDTH), lambda b, c: (b, c, 0)),
            pl.BlockSpec((None, RET_HEADS, RET_DK, RET_DV), lambda b, c: (b, 0, 0, 0)),
        ],
        out_shape=[
            jax.ShapeDtypeStruct((B, S, RET_WIDTH), BF16),
            jax.ShapeDtypeStruct((B, RET_HEADS, RET_DK, RET_DV), F32),
        ],
        compiler_params=_params(("arbitrary", "arbitrary"), 32),
        name="ret_prompt",
    )(gl, proj3, proj3, proj3, proj3, dm, qd, kd, gn_ret.reshape(1, RET_WIDTH))


def _pad_rows(x, rows):
    x = x.astype(F32)
    return jnp.concatenate([x, jnp.zeros((rows - x.shape[0], x.shape[1]), F32)], axis=0)


def _ret_sample_kernel(gl_ref, q_ref, k_ref, v_ref, rg_ref, st_in_ref, dm_ref, qd_ref, kd_ref, gn_ref,
                       y_ref, st_ref):
    sls = _head_slices()
    heads = range(RET_HEADS)
    ys, new_sts = _ret_heads([q_ref[:, sl].astype(BF16) for sl in sls],
                             [_pad_rows(k_ref[:, sl], RET_CHUNK).astype(BF16) for sl in sls],
                             [_pad_rows(v_ref[:, sl], RET_CHUNK).astype(BF16) for sl in sls],
                             [st_in_ref[h] for h in heads], [dm_ref[h] for h in heads], [qd_ref[h] for h in heads],
                             [kd_ref[h] for h in heads], [gl_ref[h] for h in heads])
    for h in heads:
        st_ref[h] = new_sts[h]
    outs = _groupnorm_gates(ys, [gn_ref[:, sl] for sl in sls], [rg_ref[:, sl].astype(F32) for sl in sls])
    for sl, out in zip(sls, outs):
        y_ref[:, sl] = out.astype(y_ref.dtype)


def _ret_sample(proj_s, state, gn_ret, T):
    nb = state.shape[0]
    dm, qd, kd, gl = _ret_tables(T)
    cb = RET_COL0 // RET_WIDTH
    col = lambda k: pl.BlockSpec((T, RET_WIDTH), lambda b, k=k: (b, cb + k))
    st_spec = pl.BlockSpec((None, RET_HEADS, RET_DK, RET_DV), lambda b: (b, 0, 0, 0))
    return pl.pallas_call(
        _ret_sample_kernel,
        grid=(nb,),
        in_specs=[
            pl.BlockSpec(memory_space=pltpu.SMEM),
            col(0), col(1), col(2), col(3), st_spec,
            pl.BlockSpec((RET_HEADS, T, RET_CHUNK), lambda b: (0, 0, 0)),
            pl.BlockSpec((RET_HEADS, T, RET_DV), lambda b: (0, 0, 0)),
            pl.BlockSpec((RET_HEADS, RET_CHUNK, RET_DK), lambda b: (0, 0, 0)),
            pl.BlockSpec((1, RET_WIDTH), lambda b: (0, 0)),
        ],
        out_specs=[pl.BlockSpec((T, RET_WIDTH), lambda b: (b, 0)), st_spec],
        out_shape=[
            jax.ShapeDtypeStruct((nb * T, RET_WIDTH), F32),
            jax.ShapeDtypeStruct(state.shape, F32),
        ],
        compiler_params=_params(("arbitrary",), 32),
        name="ret_sample",
    )(gl, proj_s, proj_s, proj_s, proj_s, state, dm, qd, kd, gn_ret.reshape(1, RET_WIDTH))


def _alibi_slope(g, h):
    n = N_GROUPS * ATT_HG
    return 2.0 ** (-ALIBI_MAX * (g * ATT_HG + h + 1) / n)


ATT_SUB = 8


def _pack_heads(xs):
    lane = lax.broadcasted_iota(jnp.int32, xs[0].shape, 1)
    out = xs[-1]
    for h in range(len(xs) - 2, -1, -1):
        out = jnp.where(lane == h, xs[h], out)
    return out


def _unpack_heads(x):
    return jnp.concatenate([jnp.broadcast_to(x[:, h:h + 1], (x.shape[0], ATT_HD)) for h in range(ATT_HG)], axis=-1)


def _att_prompt_kernel(q_ref, kp_ref, kc_ref, vp_ref, vc_ref, o_ref, lse_ref, k_scr, v_scr, bias_scr,
                       *, group, dil, nsub):
    j = pl.program_id(2)
    blk = ATT_BLK
    hd = ATT_HD
    k_scr[0:blk] = kp_ref[...]
    k_scr[blk:] = kc_ref[...]
    ones = jnp.ones((blk * (nsub + 1), hd), BF16)
    for h in range(ATT_HG):
        v_scr[0:blk, 2 * h * hd:(2 * h + 1) * hd] = vp_ref[:, h * hd:(h + 1) * hd]
        v_scr[blk:, 2 * h * hd:(2 * h + 1) * hd] = vc_ref[:, h * hd:(h + 1) * hd]
        v_scr[:, (2 * h + 1) * hd:(2 * h + 2) * hd] = ones

    r = lax.broadcasted_iota(jnp.int32, (blk, 2 * blk), 0)
    c = lax.broadcasted_iota(jnp.int32, (blk, 2 * blk), 1)
    dist = blk + r - c
    in_band = (dist >= 0) & (dist <= ATT_SPAN)
    for h in range(ATT_HG):
        bias_scr[h] = jnp.where(in_band, (-_alibi_slope(group, h) * dil) * dist.astype(F32), NEG)
    no_prev = jnp.where((c < blk) & (j == 0), NEG, 0.0)

    scale = hd ** -0.5
    for i in range(nsub):
        rows = slice(i * blk, (i + 1) * blk)
        keys = slice(i * blk, (i + 2) * blk)
        heads = range(ATT_HG)
        ss = [_dot_nt(q_ref[rows, h * hd:(h + 1) * hd], k_scr[keys, h * hd:(h + 1) * hd]) * scale + bias_scr[h]
              for h in heads]
        if i == 0:
            ss = [s + no_prev for s in ss]
        ms = [s.max(axis=-1, keepdims=True) for s in ss]
        ps = [jnp.exp(s - m).astype(BF16) for s, m in zip(ss, ms)]
        rs = [_dot(p, v_scr[keys, 2 * h * hd:(2 * h + 2) * hd]) for h, p in zip(heads, ps)]
        lses = []
        for h, res, m in zip(heads, rs, ms):
            l = res[:, hd:]
            o_ref[rows, h * hd:(h + 1) * hd] = (res[:, :hd] / l).astype(o_ref.dtype)
            lses.append(m + jnp.log(l))
        lse_ref[rows, :] = _pack_heads(lses)


def _att_prompt(qkv, group):
    B, dil, L, _ = qkv.shape
    nsub = math.gcd(ATT_SUB, L // ATT_BLK)
    rows = nsub * ATT_BLK
    cur = lambda which: pl.BlockSpec((None, None, rows, ATT_GW), lambda b, r, j: (b, r, j, which))
    prev = lambda which: pl.BlockSpec(
        (None, None, ATT_BLK, ATT_GW), lambda b, r, j: (b, r, jnp.maximum(j * nsub - 1, 0), which))
    out_spec = lambda w: pl.BlockSpec((None, None, rows, w), lambda b, r, j: (b, r, j, 0))
    return pl.pallas_call(
        functools.partial(_att_prompt_kernel, group=group, dil=dil, nsub=nsub),
        grid=(B, dil, L // rows),
        in_specs=[cur(0), prev(1), cur(1), prev(2), cur(2)],
        out_specs=[out_spec(ATT_GW), out_spec(LANES)],
        out_shape=[
            jax.ShapeDtypeStruct((B, dil, L, ATT_GW), BF16),
            jax.ShapeDtypeStruct((B, dil, L, LANES), F32),
        ],
        scratch_shapes=[
            pltpu.VMEM((rows + ATT_BLK, ATT_GW), BF16),
            pltpu.VMEM((rows + ATT_BLK, 2 * ATT_GW), BF16),
            pltpu.VMEM((ATT_HG, ATT_BLK, 2 * ATT_BLK), F32),
        ],
        compiler_params=_params(("arbitrary", "arbitrary", "arbitrary"), 32),
        name=f"att_prompt_g{group}",
    )(qkv, qkv, qkv, qkv, qkv)


def _att_sample_kernel(q_ref, k_ref, v_ref, c_ref, shifted_ref, o_ref, lse_ref, nc_ref, *, group, dil, wb, T, nres):
    del shifted_ref
    slots = 2 * ATT_HG
    nk = c_ref.shape[0] * nres if nres else wb
    t = lax.broadcasted_iota(jnp.int32, (T, nk), 0)
    c = lax.broadcasted_iota(jnp.int32, (T, nk), 1)
    pos = (c // nres) * dil + c % nres if nres else c
    dist_c = wb + t - pos
    valid_c = ((dist_c & (dil - 1)) == 0) & (dist_c <= ATT_SPAN * dil)
    tn = lax.broadcasted_iota(jnp.int32, (T, ATT_BLK), 0)
    cn = lax.broadcasted_iota(jnp.int32, (T, ATT_BLK), 1)
    dist_n = tn - cn
    valid_n = (dist_n >= 0) & ((dist_n & (dil - 1)) == 0) & (cn < T)
    scale = ATT_HD ** -0.5
    heads = range(ATT_HG)
    sls = [slice(h * ATT_HD, (h + 1) * ATT_HD) for h in heads]
    for h, sl in zip(heads, sls):
        nc_ref[pl.ds(h, T, stride=slots), :] = k_ref[:, sl]
        nc_ref[pl.ds(ATT_HG + h, T, stride=slots), :] = v_ref[:, sl]
    qs = [q_ref[:, sl].astype(BF16) for sl in sls]
    if nres:
        slot_rows = lambda s: c_ref[:, pl.ds(s, nres, stride=slots), :].reshape(nk, ATT_HD).astype(BF16)
    else:
        slot_rows = lambda s: c_ref[pl.ds(s, wb, stride=slots), :].astype(BF16)
    kcs = [slot_rows(h) for h in heads]
    vcs = [slot_rows(ATT_HG + h) for h in heads]
    kns = [_pad_rows(k_ref[:, sl], ATT_BLK).astype(BF16) for sl in sls]
    vns = [_pad_rows(v_ref[:, sl], ATT_BLK).astype(BF16) for sl in sls]
    bias_c = jnp.where(valid_c, dist_c.astype(F32), -NEG)
    bias_n = jnp.where(valid_n, dist_n.astype(F32), -NEG)
    scs = [_dot_nt(q, kc) * scale - _alibi_slope(group, h) * bias_c for h, q, kc in zip(heads, qs, kcs)]
    sns = [_dot_nt(q, kn) * scale - _alibi_slope(group, h) * bias_n for h, q, kn in zip(heads, qs, kns)]
    ms = [jnp.maximum(sc.max(axis=-1, keepdims=True), sn.max(axis=-1, keepdims=True)) for sc, sn in zip(scs, sns)]
    pcs = [jnp.exp(sc - m) for sc, m in zip(scs, ms)]
    pns = [jnp.exp(sn - m) for sn, m in zip(sns, ms)]
    ls = [pc.sum(axis=-1, keepdims=True) + pn.sum(axis=-1, keepdims=True) for pc, pn in zip(pcs, pns)]
    accs = [_dot(pc.astype(BF16), vc) + _dot(pn.astype(BF16), vn) for pc, pn, vc, vn in zip(pcs, pns, vcs, vns)]
    for sl, acc, l in zip(sls, accs, ls):
        o_ref[:, sl] = (acc / l).astype(o_ref.dtype)
    lse_ref[...] = _pack_heads([jnp.broadcast_to(m + jnp.log(l), (T, LANES)) for l, m in zip(ls, ms)])


def _cache_rows(cache):
    return cache.reshape(cache.shape[0], cache.shape[1] * 2 * ATT_HG, ATT_HD)


def _att_sample(proj_s, cache, shifted, group, T):
    nb, wb = cache.shape[0], cache.shape[1]
    assert wb > T and wb % T == 0
    _, dil = ATT_GROUPS[group]
    c0 = 3 * group
    slots = 2 * ATT_HG
    col = lambda k: pl.BlockSpec((T, ATT_GW), lambda b, k=k: (b, c0 + k))
    out_spec = pl.BlockSpec((T, ATT_GW), lambda b: (b, 0))
    rows = _cache_rows(cache)
    if dil > T and wb % dil == 0 and T % 8 == 0:
        nres = T
        rows = rows.reshape(nb, wb // dil, dil * slots, ATT_HD)
        cache_spec = pl.BlockSpec((None, wb // dil, nres * slots, ATT_HD), lambda b: (b, 0, 0, 0))
    else:
        nres = 0
        cache_spec = pl.BlockSpec((None, wb * slots, ATT_HD), lambda b: (b, 0, 0))
    o, lse, new_cache = pl.pallas_call(
        functools.partial(_att_sample_kernel, group=group, dil=dil, wb=wb, T=T, nres=nres),
        grid=(nb,),
        in_specs=[col(0), col(1), col(2), cache_spec, pl.BlockSpec(memory_space=pl.ANY)],
        out_specs=[out_spec, pl.BlockSpec((T, LANES), lambda b: (b, 0)),
                   pl.BlockSpec((None, T * slots, ATT_HD), lambda b: (b, wb // T - 1, 0))],
        out_shape=[
            jax.ShapeDtypeStruct((nb * T, ATT_GW), F32),
            jax.ShapeDtypeStruct((nb * T, LANES), F32),
            jax.ShapeDtypeStruct(shifted.shape, F32),
        ],
        input_output_aliases={4: 2},
        compiler_params=_params(("arbitrary",), 48),
        name=f"att_sample_g{group}",
    )(proj_s, proj_s, proj_s, rows, shifted)
    return o, lse, new_cache.reshape(cache.shape)


def _epilogue_kernel(x_ref, gr_ref, ga_ref, yret_ref, o0_ref, o1_ref, o2_ref, l0_ref, l1_ref, l2_ref,
                     wr_ref, wa_ref, wo_ref, gf_ref, x1_ref, h2_ref, *scratch, by_residue):
    def position_order(ref, scr):
        dil = ref.shape[0]
        if dil == 1:
            return ref[0].astype(F32)
        for r in range(dil):
            slab = ref[r].astype(F32)
            for cb in range(scr.shape[0]):
                scr[cb, pl.ds(r, ref.shape[1], stride=dil), :] = slab[:, cb * LANES:(cb + 1) * LANES]
        return jnp.concatenate([scr[cb] for cb in range(scr.shape[0])], axis=-1)

    if by_residue:
        o0, o1, o2 = (position_order(ref, scr) for ref, scr in zip((o0_ref, o1_ref, o2_ref), scratch[0:3]))
        l0, l1, l2 = (position_order(ref, scr) for ref, scr in zip((l0_ref, l1_ref, l2_ref), scratch[3:6]))
    else:
        o0, o1, o2 = o0_ref[...], o1_ref[...], o2_ref[...]
        l0, l1, l2 = l0_ref[...], l1_ref[...], l2_ref[...]
    m = jnp.maximum(jnp.maximum(l0, l1), l2)
    e0, e1, e2 = jnp.exp(l0 - m), jnp.exp(l1 - m), jnp.exp(l2 - m)
    inv = 1.0 / (e0 + e1 + e2)
    o = _unpack_heads(e0 * inv) * o0 + _unpack_heads(e1 * inv) * o1 + _unpack_heads(e2 * inv) * o2
    att = _dot(o.astype(BF16), wa_ref[...])
    ret = _dot(yret_ref[...].astype(BF16), wr_ref[...])
    merged = (jax.nn.sigmoid(gr_ref[...].astype(F32)) * ret + jax.nn.sigmoid(ga_ref[...].astype(F32)) * att)
    x1 = x_ref[...] + _dot(merged.astype(BF16), wo_ref[...])
    x1_ref[...] = x1
    h2_ref[...] = _rmsnorm_rows(x1, gf_ref[...]).astype(BF16)


def _epilogue(x2d, proj2, yret, os_, lses, w_ret_o, w_att_o, w_o, g_ffn, *, tm, seq_len=None):
    M = x2d.shape[0]
    row = lambda w: pl.BlockSpec((tm, w), lambda m: (m, 0))
    const = lambda a: pl.BlockSpec(a.shape, lambda m: (0, 0), pipeline_mode=pl.Buffered(1))
    gf = g_ffn.reshape(1, D_MODEL)
    by_residue = seq_len is not None
    scratch = []
    if by_residue:
        tps = seq_len // tm
        att_specs = [pl.BlockSpec((None, a.shape[1], tm // a.shape[1], a.shape[3]),
                                  lambda m: (m // tps, 0, m % tps, 0)) for a in (*os_, *lses)]
        scratch = [pltpu.VMEM((a.shape[3] // LANES, tm, LANES), F32) for a in (*os_, *lses)]
    else:
        att_specs = [row(a.shape[1]) for a in (*os_, *lses)]
    return pl.pallas_call(
        functools.partial(_epilogue_kernel, by_residue=by_residue),
        grid=(M // tm,),
        in_specs=[
            row(D_MODEL),
            pl.BlockSpec((tm, D_MODEL), lambda m: (m, 0)),
            pl.BlockSpec((tm, D_MODEL), lambda m: (m, 1)),
            row(RET_WIDTH), *att_specs,
            const(w_ret_o), const(w_att_o), const(w_o), const(gf),
        ],
        out_specs=[row(D_MODEL), row(D_MODEL)],
        out_shape=[jax.ShapeDtypeStruct((M, D_MODEL), F32), jax.ShapeDtypeStruct((M, D_MODEL), BF16)],
        scratch_shapes=scratch,
        compiler_params=_params(("arbitrary",), 56),
        name="epilogue",
    )(x2d, proj2, proj2, yret, *os_, *lses, w_ret_o, w_att_o, w_o, gf)


def _gelu_gate(a, half_b):
    return (a * (1.0 + lax.erf(a * math.sqrt(0.5)))) * half_b


def _halve_b_half(x):
    return jnp.concatenate([x[..., :D_FF], 0.5 * x[..., D_FF:]], axis=-1)


def _conv3(u, r1, r2, w_ref, b_ref):
    return b_ref[...] + (r2 * w_ref[0:1, :] + r1 * w_ref[1:2, :] + u * w_ref[2:3, :])


UP_CHUNK = 256


SHIFT_CHUNK_ROWS = 4096


def _shift_plan(cache_rows, T, steps):
    nb, n, _ = cache_rows.shape
    shift = T * 2 * ATT_HG
    body = n - shift
    per_seq = -(-body // SHIFT_CHUNK_ROWS)
    while body % per_seq or (body // per_seq) % 8:
        per_seq += 1
    plan = dict(shift=shift, chunk=body // per_seq, per_seq=per_seq, count=nb * per_seq)
    assert plan["count"] < steps, "the background copy needs one grid step per chunk plus one to drain"
    return plan


def _shift_copy_step(s, src_ref, dst_ref, buf, sems, i, *, shift, chunk, per_seq, count):
    n = src_ref.shape[1]
    tail = pltpu.make_async_copy(src_ref.at[:, pl.ds(n - shift, shift)], dst_ref.at[:, pl.ds(n - shift, shift)],
                                 sems.at[i, 4])

    @pl.when(s == 0)
    def _():
        tail.start()

    @pl.when(s == count)
    def _():
        tail.wait()

    def read(c, slot):
        start = pl.multiple_of(shift + (c % per_seq) * chunk, 8)
        return pltpu.make_async_copy(src_ref.at[c // per_seq, pl.ds(start, chunk)], buf.at[slot], sems.at[i, slot])

    def write(c, slot):
        start = pl.multiple_of((c % per_seq) * chunk, 8)
        return pltpu.make_async_copy(buf.at[slot], dst_ref.at[c // per_seq, pl.ds(start, chunk)],
                                     sems.at[i, 2 + slot])

    slot = s % 2

    @pl.when(s == 0)
    def _():
        read(0, 0).start()

    @pl.when(s < count)
    def _():
        read(s, slot).wait()
        write(s, slot).start()

    @pl.when((s >= 1) & (s <= count))
    def _():
        write(s - 1, 1 - slot).wait()

    @pl.when(s + 1 < count)
    def _():
        read(s + 1, 1 - slot).start()


def _up_gate_kernel(h_ref, w_ref, cwa_ref, cwb_ref, cba_ref, cbb_ref, *rest, tiles_per_seq, shift_plans):
    ng = len(shift_plans)
    caches, rest = rest[:ng], rest[ng:]
    g_ref, ta_ref, tb_ref = rest[:3]
    shifted, rest = rest[3:3 + ng], rest[3 + ng:]
    u_scr, carry_scr = rest[:2]
    bufs, (sems,) = rest[2:2 + ng], rest[2 + ng:]
    m = pl.program_id(0)
    f = pl.program_id(1)
    tm = h_ref.shape[0]

    step = m * pl.num_programs(1) + f
    for i, plan in enumerate(shift_plans):
        _shift_copy_step(step, caches[i], shifted[i], bufs[i], sems, i, **plan)

    @pl.when(m % tiles_per_seq == 0)
    def _():
        carry_scr[f] = jnp.zeros(carry_scr.shape[1:], F32)

    u_scr[0:8, :] = carry_scr[f]
    u_scr[8:, :] = _dot(h_ref[...], w_ref[...])
    carry_scr[f] = u_scr[tm:, :]
    for c in range(g_ref.shape[1] // UP_CHUNK):
        cols = slice(c * UP_CHUNK, (c + 1) * UP_CHUNK)

        def conv(half, cw_ref, cb_ref, tail_ref):
            pcols = slice((2 * c + half) * UP_CHUNK, (2 * c + half + 1) * UP_CHUNK)
            tail_ref[:, cols] = u_scr[tm:, pcols]
            u, r1, r2 = u_scr[8:, pcols], u_scr[7:tm + 7, pcols], u_scr[6:tm + 6, pcols]
            return cb_ref[:, cols] + (r2 * cw_ref[0:1, cols] + r1 * cw_ref[1:2, cols] + u * cw_ref[2:3, cols])

        a = conv(0, cwa_ref, cba_ref, ta_ref)
        b = conv(1, cwb_ref, cbb_ref, tb_ref)
        g_ref[:, cols] = _gelu_gate(a, b).astype(BF16)


def _up_gate(h2, w_up_ab, conv_w, conv_b, caches, T, *, tm, tiles_per_seq):
    M = h2.shape[0]
    tf = 512
    nf = D_FF // tf
    cb = conv_b.reshape(1, 2 * D_FF)
    grid = (M // tm, nf)
    plans = tuple(_shift_plan(c, T, grid[0] * grid[1]) for c in caches)
    any_spec = pl.BlockSpec(memory_space=pl.ANY)
    return pl.pallas_call(
        functools.partial(_up_gate_kernel, tiles_per_seq=tiles_per_seq, shift_plans=plans),
        grid=grid,
        in_specs=[
            pl.BlockSpec((tm, D_MODEL), lambda m, f: (m, 0)),
            pl.BlockSpec((D_MODEL, 2 * tf), lambda m, f: (0, f)),
            pl.BlockSpec((CONV_W, tf), lambda m, f: (0, f)),
            pl.BlockSpec((CONV_W, tf), lambda m, f: (0, f + nf)),
            pl.BlockSpec((1, tf), lambda m, f: (0, f)),
            pl.BlockSpec((1, tf), lambda m, f: (0, f + nf)),
            *[any_spec for _ in caches],
        ],
        out_specs=[
            pl.BlockSpec((tm, tf), lambda m, f: (m, f)),
            pl.BlockSpec((None, 8, tf), lambda m, f: (m, 0, f)),
            pl.BlockSpec((None, 8, tf), lambda m, f: (m, 0, f)),
            *[any_spec for _ in caches],
        ],
        out_shape=[
            jax.ShapeDtypeStruct((M, D_FF), BF16),
            jax.ShapeDtypeStruct((M // tm, 8, D_FF), F32),
            jax.ShapeDtypeStruct((M // tm, 8, D_FF), F32),
            *[jax.ShapeDtypeStruct(c.shape, F32) for c in caches],
        ],
        scratch_shapes=[
            pltpu.VMEM((tm + 8, 2 * tf), F32),
            pltpu.VMEM((nf, 8, 2 * tf), F32),
            *[pltpu.VMEM((2, p["chunk"], ATT_HD), F32) for p in plans],
            pltpu.SemaphoreType.DMA((len(caches), 5)),
        ],
        compiler_params=_params(("arbitrary", "arbitrary"), 56),
        name="up_gate",
    )(h2, w_up_ab, conv_w, conv_w, cb, cb, *caches)


def _down_kernel(g_ref, wd_ref, x1_ref, gfin_ref, y_ref, *, final_norm):
    y = x1_ref[...] + _dot(g_ref[...], wd_ref[...])
    y_ref[...] = _rmsnorm_rows(y, gfin_ref[...]) if final_norm else y


def _down(g, x1, w_down, g_final, *, tm, final_norm):
    M = g.shape[0]
    return pl.pallas_call(
        functools.partial(_down_kernel, final_norm=final_norm),
        grid=(M // tm,),
        in_specs=[
            pl.BlockSpec((tm, D_FF), lambda m: (m, 0)),
            pl.BlockSpec((D_FF, D_MODEL), lambda m: (0, 0), pipeline_mode=pl.Buffered(1)),
            pl.BlockSpec((tm, D_MODEL), lambda m: (m, 0)),
            pl.BlockSpec((1, D_MODEL), lambda m: (0, 0)),
        ],
        out_specs=pl.BlockSpec((tm, D_MODEL), lambda m: (m, 0)),
        out_shape=jax.ShapeDtypeStruct((M, D_MODEL), F32),
        compiler_params=_params(("arbitrary",), 48),
        name="down",
    )(g, w_down, x1, g_final.reshape(1, D_MODEL))


def _ffn_sample_kernel(h_ref, wu_ref, s0a_ref, s0b_ref, s1a_ref, s1b_ref, cwa_ref, cwb_ref,
                       cba_ref, cbb_ref, wd_ref, x1_ref, gfin_ref, y_ref, ua_ref, ub_ref, *, T, final_norm):
    f = pl.program_id(0)
    M, tf = ua_ref.shape
    t = lax.broadcasted_iota(jnp.int32, (M, tf), 0) & (T - 1)
    u2 = _dot(h_ref[...], wu_ref[...])
    nchunk = tf // UP_CHUNK
    half = lambda k: jnp.concatenate(
        [u2[:, (2 * c + k) * UP_CHUNK:(2 * c + k + 1) * UP_CHUNK] for c in range(nchunk)], axis=-1)
    per_row = lambda s_ref: jnp.broadcast_to(s_ref[...][:, None, :], (M // T, T, tf)).reshape(M, tf)

    def conv(u, s0_ref, s1_ref, w_ref, b_ref, u_out_ref):
        u_out_ref[...] = u
        s0, s1 = per_row(s0_ref), per_row(s1_ref)
        r1 = jnp.where(t >= 1, pltpu.roll(u, 1, axis=0), s1)
        r2 = jnp.where(t >= 2, pltpu.roll(u, 2, axis=0), jnp.where(t == 0, s0, s1))
        return _conv3(u, r1, r2, w_ref, b_ref)

    a = conv(half(0), s0a_ref, s1a_ref, cwa_ref, cba_ref, ua_ref)
    b = conv(half(1), s0b_ref, s1b_ref, cwb_ref, cbb_ref, ub_ref)
    part = _dot(_gelu_gate(a, b).astype(BF16), wd_ref[...])

    @pl.when(f == 0)
    def _():
        y_ref[...] = x1_ref[...] + part

    @pl.when(f > 0)
    def _():
        y_ref[...] += part

    if final_norm:
        @pl.when(f == pl.num_programs(0) - 1)
        def _():
            y_ref[...] = _rmsnorm_rows(y_ref[...], gfin_ref[...])


def _ffn_sample(h2, x1, state_conv, w_up_ab, conv_w, conv_b, w_down, g_final, T, final_norm):
    M = h2.shape[0]
    nb = M // T
    tf = 512
    nf = D_FF // tf
    F2 = 2 * D_FF
    assert state_conv.shape[1] == CONV_W - 1 == 2
    s0, s1 = state_conv[:, 0].astype(F32), state_conv[:, 1].astype(F32)
    cb = conv_b.reshape(1, F2)
    gfin = g_final.reshape(1, D_MODEL)
    full = lambda w: pl.BlockSpec((M, w), lambda f: (0, 0))
    ca = lambda r, w=tf: pl.BlockSpec((r, w), lambda f: (0, f))
    cbk = lambda r, w=tf: pl.BlockSpec((r, w), lambda f: (0, f + nf))
    y, ua, ub = pl.pallas_call(
        functools.partial(_ffn_sample_kernel, T=T, final_norm=final_norm),
        grid=(nf,),
        in_specs=[
            full(D_MODEL), ca(D_MODEL, 2 * tf), ca(nb), cbk(nb), ca(nb), cbk(nb),
            ca(CONV_W), cbk(CONV_W), ca(1), cbk(1),
            pl.BlockSpec((tf, D_MODEL), lambda f: (f, 0)),
            full(D_MODEL), pl.BlockSpec((1, D_MODEL), lambda f: (0, 0)),
        ],
        out_specs=[full(D_MODEL), ca(M), ca(M)],
        out_shape=[
            jax.ShapeDtypeStruct((M, D_MODEL), F32),
            jax.ShapeDtypeStruct((M, D_FF), F32),
            jax.ShapeDtypeStruct((M, D_FF), F32),
        ],
        compiler_params=_params(("arbitrary",), 48),
        name="ffn_sample",
    )(h2, w_up_ab, s0, s0, s1, s1, conv_w, conv_w, cb, cb, w_down, x1, gfin)
    u = jnp.concatenate([ua, ub], axis=-1).reshape(nb, T, F2)
    return y, u[:, T - (CONV_W - 1):]


def kernel(x_prompt, x_sample, state_ret, cache_kv_w128, cache_kv_w512, cache_kv_w2048, state_conv, g_mix, w_in,
           gn_ret, w_ret_o, w_att_o, w_o, g_ffn, w_up, conv_w, conv_b, w_down, g_final):
    B, S, _ = x_prompt.shape
    NB, T, _ = x_sample.shape
    depth = w_in.shape[0]
    caches = (cache_kv_w128, cache_kv_w512, cache_kv_w2048)
    F2 = 2 * D_FF
    TM = 1024
    tiles_per_seq = S // TM

    xp = x_prompt.reshape(B * S, D_MODEL)
    xs = x_sample.reshape(NB * T, D_MODEL)
    ret_p, ret_s, conv_p, conv_s = [], [], [], []
    kv_p = [[] for _ in range(N_GROUPS)]
    kv_s = [[] for _ in range(N_GROUPS)]
    for l in range(depth):
        ret_tiles = 4 * RET_WIDTH // CAST_TN
        att_tiles = ATT_COLS // CAST_TN
        in_tiles = IN_COLS // CAST_TN
        w_main_l = _cast_cols(w_in[l], lambda n: (n + ret_tiles + att_tiles) % in_tiles, MAIN_COLS // CAST_TN,
                              CAST_TN, "cast_w_main")
        w_att_l = _cast_cols(w_in[l], lambda n: n + ret_tiles, att_tiles, CAST_TN, "cast_w_att")
        half_chunks = D_FF // UP_CHUNK
        w_up_l = _cast_cols(w_up[l], lambda n: n // 2 + (n % 2) * half_chunks, 2 * half_chunks, UP_CHUNK,
                            "cast_w_up")
        w_ret_o_l = w_ret_o[l].astype(BF16)
        w_att_o_l = w_att_o[l].astype(BF16)
        w_o_l = w_o[l].astype(BF16)
        w_down_l = w_down[l].astype(BF16)

        proj, h = _inproj_main(xp, g_mix[l], w_main_l, tm=TM)
        yret, st = _ret_prompt(proj.reshape(B, S, MAIN_COLS), gn_ret[l])
        ret_p.append(st)
        os_, lses = [], []
        for g, (win, _) in enumerate(ATT_GROUPS):
            qkv, kvf = _inproj_att(h, w_att_l, g, B=B, S=S, tm=512)
            o, lse = _att_prompt(qkv, g)
            os_.append(o)
            lses.append(lse)
            kv_p[g].append(kvf.reshape(B, min(win, S), 2, ATT_HG, ATT_HD))
        x1, h2 = _epilogue(xp, proj, yret.reshape(B * S, RET_WIDTH), os_, lses,
                           w_ret_o_l, w_att_o_l, w_o_l, g_ffn[l], tm=256, seq_len=S)
        conv_w_l, conv_b_l = _halve_b_half(conv_w[l]), _halve_b_half(conv_b[l])
        gated, tail_a, tail_b, *shifted = _up_gate(h2, w_up_l, conv_w_l, conv_b_l,
                                                   [_cache_rows(c[l]) for c in caches], T,
                                                   tm=TM, tiles_per_seq=tiles_per_seq)
        utail = jnp.concatenate([tail_a, tail_b], axis=-1)
        conv_p.append(utail[tiles_per_seq - 1::tiles_per_seq, 8 - (CONV_W - 1):])
        xp = _down(gated, x1, w_down_l, g_final, tm=256, final_norm=l == depth - 1)

        proj_s = _inproj_sample(xs, g_mix[l], w_main_l, "inproj_sample_main")
        att_s = _inproj_sample(xs, g_mix[l], w_att_l, "inproj_sample_att")
        yret_s, st_s = _ret_sample(proj_s, state_ret[l], gn_ret[l], T)
        ret_s.append(st_s)
        os_, lses = [], []
        for g in range(N_GROUPS):
            o, lse, new_cache = _att_sample(att_s, caches[g][l], shifted[g], g, T)
            os_.append(o)
            lses.append(lse)
            kv_s[g].append(new_cache)
        x1s, h2s = _epilogue(xs, proj_s, yret_s, os_, lses, w_ret_o_l, w_att_o_l, w_o_l, g_ffn[l], tm=NB * T)
        xs, cv = _ffn_sample(h2s, x1s, state_conv[l], w_up_l, conv_w_l, conv_b_l, w_down_l,
                             g_final, T, final_norm=l == depth - 1)
        conv_s.append(cv)

    return (xp.reshape(B, S, D_MODEL), xs.reshape(NB, T, D_MODEL),
            jnp.stack(ret_p), jnp.stack(ret_s),
            jnp.stack(kv_p[0]), jnp.stack(kv_s[0]),
            jnp.stack(kv_p[1]), jnp.stack(kv_s[1]),
            jnp.stack(kv_p[2]), jnp.stack(kv_s[2]),
            jnp.stack(conv_p), jnp.stack(conv_s))
```

```python
import functools
import math

import jax
import jax.numpy as jnp
from jax import lax
from jax.experimental import pallas as pl
from jax.experimental.pallas import tpu as pltpu

D_MODEL = 2048
RET_HEADS = 8
RET_DK = 128
RET_DV = 128
RET_WIDTH = RET_HEADS * RET_DV
RET_CHUNK = 128
ATT_GROUPS = ((128, 1), (512, 4), (2048, 16))
N_GROUPS = 3
ATT_HG = 4
ATT_HD = 128
ATT_GW = ATT_HG * ATT_HD
ATT_SPAN = 128
ATT_BLK = 128
ALIBI_MAX = 8.0
D_FF = 5632
CONV_W = 3
EPS = 1e-6
NEG = -1e30
IN_COLS = 4 * RET_WIDTH + 3 * N_GROUPS * ATT_GW + 2 * D_MODEL

GATE_COLS = 2 * D_MODEL
RET_COL0 = GATE_COLS
MAIN_COLS = GATE_COLS + 4 * RET_WIDTH
ATT_COLS = 3 * N_GROUPS * ATT_GW
CAST_TN = 512

LANES = 128
MIB = 1024 * 1024
BF16 = jnp.bfloat16
F32 = jnp.float32


def _params(semantics, vmem_mib, flags=None):
    return pltpu.CompilerParams(dimension_semantics=semantics, vmem_limit_bytes=vmem_mib * MIB, flags=flags)


def _dot(a, b):
    return jnp.dot(a, b, preferred_element_type=F32)


def _dot_nt(a, b):
    return lax.dot_general(a, b, (((1,), (1,)), ((), ())), preferred_element_type=F32)


def _rmsnorm_rows(x, g):
    ms = jnp.mean(x * x, axis=-1, keepdims=True)
    return x * lax.rsqrt(ms + EPS) * g


def _inproj_sample_kernel(x_ref, g_ref, w_ref, proj_ref, wcast_ref, h_scr):
    @pl.when(pl.program_id(0) == 0)
    def _():
        h_scr[...] = _rmsnorm_rows(x_ref[...], g_ref[...]).astype(BF16)

    w = w_ref[...].astype(BF16)
    wcast_ref[...] = w
    proj_ref[...] = _dot(h_scr[...], w)


def _inproj_sample(x2d, g, w, src_tile, n_tiles, name):
    M = x2d.shape[0]
    tn = CAST_TN
    return pl.pallas_call(
        _inproj_sample_kernel,
        grid=(n_tiles,),
        in_specs=[
            pl.BlockSpec((M, D_MODEL), lambda n: (0, 0)),
            pl.BlockSpec((1, D_MODEL), lambda n: (0, 0)),
            pl.BlockSpec((D_MODEL, tn), lambda n: (0, src_tile(n))),
        ],
        out_specs=[pl.BlockSpec((M, tn), lambda n: (0, n)), pl.BlockSpec((D_MODEL, tn), lambda n: (0, n))],
        out_shape=[jax.ShapeDtypeStruct((M, n_tiles * tn), F32), jax.ShapeDtypeStruct((D_MODEL, n_tiles * tn), BF16)],
        scratch_shapes=[pltpu.VMEM((M, D_MODEL), BF16)],
        compiler_params=_params(("arbitrary",), 32),
        name=name,
    )(x2d, g.reshape(1, D_MODEL), w)


def _inproj_main_kernel(x_ref, g_ref, w_ref, main_ref, h_ref):
    @pl.when(pl.program_id(1) == 0)
    def _():
        h_ref[...] = _rmsnorm_rows(x_ref[...], g_ref[...]).astype(BF16)

    main_ref[...] = _dot(h_ref[...], w_ref[...]).astype(BF16)


def _inproj_main(x2d, g, w_main, *, tm):
    M = x2d.shape[0]
    N = w_main.shape[1]
    tn = 1024
    return pl.pallas_call(
        _inproj_main_kernel,
        grid=(M // tm, N // tn),
        in_specs=[
            pl.BlockSpec((tm, D_MODEL), lambda m, n: (m, 0)),
            pl.BlockSpec((1, D_MODEL), lambda m, n: (0, 0)),
            pl.BlockSpec((D_MODEL, tn), lambda m, n: (0, n)),
        ],
        out_specs=[pl.BlockSpec((tm, tn), lambda m, n: (m, n)), pl.BlockSpec((tm, D_MODEL), lambda m, n: (m, 0))],
        out_shape=[jax.ShapeDtypeStruct((M, N), BF16), jax.ShapeDtypeStruct((M, D_MODEL), BF16)],
        compiler_params=_params(("arbitrary", "arbitrary"), 48),
        name="inproj_main",
    )(x2d, g.reshape(1, D_MODEL), w_main)


def _inproj_att_kernel(h_ref, w_ref, a_ref, kv_ref, *scratch, dil):
    acc = _dot(h_ref[...], w_ref[...])
    slots = 2 * ATT_HG
    rows = kv_ref.shape[0] // slots
    for slot in range(slots):
        cols = slice(ATT_GW + slot * ATT_HD, ATT_GW + (slot + 1) * ATT_HD)
        kv_ref[pl.ds(slot, rows, stride=slots), :] = acc[acc.shape[0] - rows:, cols]
    if dil == 1:
        a_ref[0] = acc.astype(BF16)
    else:
        acc_scr, = scratch
        rows = acc_scr.shape[1] // dil
        for cb in range(acc_scr.shape[0]):
            lanes = slice(cb * LANES, (cb + 1) * LANES)
            acc_scr[cb] = acc[:, lanes]
            for r in range(dil):
                a_ref[r, :, lanes] = acc_scr[cb, pl.ds(r, rows, stride=dil), :].astype(BF16)


def _inproj_att(h, w_att, group, *, B, S, tm):
    M = h.shape[0]
    tn = 3 * ATT_GW
    tps = S // tm
    win, dil = ATT_GROUPS[group]
    win = min(win, S)
    rows = min(win, tm)
    first = tps - win // rows
    slots = 2 * ATT_HG
    return pl.pallas_call(
        functools.partial(_inproj_att_kernel, dil=dil),
        grid=(M // tm,),
        in_specs=[
            pl.BlockSpec((tm, D_MODEL), lambda m: (m, 0)),
            pl.BlockSpec((D_MODEL, tn), lambda m: (0, group)),
        ],
        out_specs=[
            pl.BlockSpec((None, dil, tm // dil, tn), lambda m: (m // tps, 0, m % tps, 0)),
            pl.BlockSpec((None, rows * slots, ATT_HD), lambda m: (m // tps, jnp.maximum(m % tps - first, 0), 0)),
        ],
        out_shape=[
            jax.ShapeDtypeStruct((B, dil, S // dil, tn), BF16),
            jax.ShapeDtypeStruct((B, win * slots, ATT_HD), F32),
        ],
        scratch_shapes=[] if dil == 1 else [pltpu.VMEM((tn // LANES, tm, LANES), F32)],
        compiler_params=_params(("arbitrary",), 48),
        name=f"inproj_att_g{group}",
    )(h, w_att)


def _ret_tables(lq):
    lg = jnp.log(1.0 - jnp.exp2(-5.0 - jnp.arange(RET_HEADS, dtype=F32)))
    scale = RET_DK ** -0.5
    i = jnp.arange(lq, dtype=F32)
    j = jnp.arange(RET_CHUNK, dtype=F32)
    diff = i[:, None] - j[None, :]
    live = (diff >= 0) & (j[None, :] < lq)
    dm = jnp.where(live[None], jnp.exp(jnp.maximum(diff, 0.0)[None] * lg[:, None, None]), 0.0) * scale
    qd = jnp.exp((i[None, :] + 1.0) * lg[:, None])
    qd = jnp.broadcast_to(qd[:, :, None], (RET_HEADS, lq, RET_DV))
    kd = jnp.where(j[None, :] < lq, jnp.exp(jnp.maximum(lq - 1.0 - j, 0.0)[None, :] * lg[:, None]), 0.0) * scale
    kd = jnp.broadcast_to(kd[:, :, None], (RET_HEADS, RET_CHUNK, RET_DK))
    gl = jnp.exp(lq * lg)
    return dm.astype(F32), qd.astype(F32), kd.astype(F32), gl.astype(F32)


def _ret_heads(qs, ks, vs, sts, dms, qds, kds, gls):
    ss = [(_dot_nt(q, k) * dm).astype(BF16) for q, k, dm in zip(qs, ks, dms)]
    crosses = [_dot(q, st.astype(BF16)) for q, st in zip(qs, sts)]
    kts = [(k.astype(F32) * kd).T.astype(BF16) for k, kd in zip(ks, kds)]
    inners = [_dot(s, v) for s, v in zip(ss, vs)]
    updates = [_dot(kt, v) for kt, v in zip(kts, vs)]
    ys = [inner + cross * qd for inner, cross, qd in zip(inners, crosses, qds)]
    new_sts = [st * gl + upd for st, gl, upd in zip(sts, gls, updates)]
    return ys, new_sts


def _groupnorm_gates(ys, gns, rgs):
    mus = [jnp.mean(y, axis=-1, keepdims=True) for y in ys]
    ycs = [y - mu for y, mu in zip(ys, mus)]
    rstds = [lax.rsqrt(jnp.mean(yc * yc, axis=-1, keepdims=True) + EPS) for yc in ycs]
    return [(rg * jax.nn.sigmoid(rg)) * ((yc * rstd) * gn) for yc, rstd, gn, rg in zip(ycs, rstds, gns, rgs)]


def _head_slices():
    return [slice(h * RET_DK, (h + 1) * RET_DK) for h in range(RET_HEADS)]


def _ret_prompt_kernel(gl_ref, q_ref, k_ref, v_ref, rg_ref, dm_ref, qd_ref, kd_ref, gn_ref, y_ref, st_ref):
    c = pl.program_id(1)

    @pl.when(c == 0)
    def _():
        st_ref[...] = jnp.zeros_like(st_ref)

    sls = _head_slices()
    heads = range(RET_HEADS)
    sts = [st_ref[h] for h in heads]
    for i in range(q_ref.shape[0] // RET_CHUNK):
        rows = slice(i * RET_CHUNK, (i + 1) * RET_CHUNK)
        ys, sts = _ret_heads([q_ref[rows, sl] for sl in sls], [k_ref[rows, sl] for sl in sls],
                             [v_ref[rows, sl] for sl in sls], sts, [dm_ref[h] for h in heads],
                             [qd_ref[h] for h in heads], [kd_ref[h] for h in heads], [gl_ref[h] for h in heads])
        outs = _groupnorm_gates(ys, [gn_ref[:, sl] for sl in sls], [rg_ref[rows, sl].astype(F32) for sl in sls])
        for sl, out in zip(sls, outs):
            y_ref[rows, sl] = out.astype(y_ref.dtype)
    for h in heads:
        st_ref[h] = sts[h]


RET_STEP_CHUNKS = 2


def _ret_prompt(proj3, gn_ret):
    B, S, _ = proj3.shape
    L = RET_CHUNK
    dm, qd, kd, gl = _ret_tables(L)
    cb = RET_COL0 // RET_WIDTH
    rows = L * math.gcd(RET_STEP_CHUNKS, S // L)
    col = lambda k: pl.BlockSpec((None, rows, RET_WIDTH), lambda b, c, k=k: (b, c, cb + k))
    tab = pl.BlockSpec((RET_HEADS, L, RET_DK), lambda b, c: (0, 0, 0))
    return pl.pallas_call(
        _ret_prompt_kernel,
        grid=(B, S // rows),
        in_specs=[
            pl.BlockSpec(memory_space=pltpu.SMEM),
            col(0), col(1), col(2), col(3), tab, tab, tab,
            pl.BlockSpec((1, RET_WIDTH), lambda b, c: (0, 0)),
        ],
        out_specs=[
            pl.BlockSpec((None, rows, RET_WIDTH), lambda b, c: (b, c, 0)),
            pl.BlockSpec((None, RET_HEADS, RET_DK, RET_DV), lambda b, c: (b, 0, 0, 0)),
        ],
        out_shape=[
            jax.ShapeDtypeStruct((B, S, RET_WIDTH), BF16),
            jax.ShapeDtypeStruct((B, RET_HEADS, RET_DK, RET_DV), F32),
        ],
        compiler_params=_params(("arbitrary", "arbitrary"), 32),
        name="ret_prompt",
    )(gl, proj3, proj3, proj3, proj3, dm, qd, kd, gn_ret.reshape(1, RET_WIDTH))


def _pad_rows(x, rows):
    x = x.astype(F32)
    return jnp.concatenate([x, jnp.zeros((rows - x.shape[0], x.shape[1]), F32)], axis=0)


def _ret_sample_kernel(gl_ref, q_ref, k_ref, v_ref, rg_ref, st_in_ref, dm_ref, qd_ref, kd_ref, gn_ref,
                       y_ref, st_ref):
    sls = _head_slices()
    heads = range(RET_HEADS)
    ys, new_sts = _ret_heads([q_ref[:, sl].astype(BF16) for sl in sls],
                             [_pad_rows(k_ref[:, sl], RET_CHUNK).astype(BF16) for sl in sls],
                             [_pad_rows(v_ref[:, sl], RET_CHUNK).astype(BF16) for sl in sls],
                             [st_in_ref[h] for h in heads], [dm_ref[h] for h in heads], [qd_ref[h] for h in heads],
                             [kd_ref[h] for h in heads], [gl_ref[h] for h in heads])
    for h in heads:
        st_ref[h] = new_sts[h]
    outs = _groupnorm_gates(ys, [gn_ref[:, sl] for sl in sls], [rg_ref[:, sl].astype(F32) for sl in sls])
    for sl, out in zip(sls, outs):
        y_ref[:, sl] = out.astype(y_ref.dtype)


def _ret_sample(proj_s, state, gn_ret, T):
    nb = state.shape[0]
    dm, qd, kd, gl = _ret_tables(T)
    cb = RET_COL0 // RET_WIDTH
    col = lambda k: pl.BlockSpec((T, RET_WIDTH), lambda b, k=k: (b, cb + k))
    st_spec = pl.BlockSpec((None, RET_HEADS, RET_DK, RET_DV), lambda b: (b, 0, 0, 0))
    return pl.pallas_call(
        _ret_sample_kernel,
        grid=(nb,),
        in_specs=[
            pl.BlockSpec(memory_space=pltpu.SMEM),
            col(0), col(1), col(2), col(3), st_spec,
            pl.BlockSpec((RET_HEADS, T, RET_CHUNK), lambda b: (0, 0, 0)),
            pl.BlockSpec((RET_HEADS, T, RET_DV), lambda b: (0, 0, 0)),
            pl.BlockSpec((RET_HEADS, RET_CHUNK, RET_DK), lambda b: (0, 0, 0)),
            pl.BlockSpec((1, RET_WIDTH), lambda b: (0, 0)),
        ],
        out_specs=[pl.BlockSpec((T, RET_WIDTH), lambda b: (b, 0)), st_spec],
        out_shape=[
            jax.ShapeDtypeStruct((nb * T, RET_WIDTH), F32),
            jax.ShapeDtypeStruct(state.shape, F32),
        ],
        compiler_params=_params(("arbitrary",), 32),
        name="ret_sample",
    )(gl, proj_s, proj_s, proj_s, proj_s, state, dm, qd, kd, gn_ret.reshape(1, RET_WIDTH))


def _alibi_slope(g, h):
    n = N_GROUPS * ATT_HG
    return 2.0 ** (-ALIBI_MAX * (g * ATT_HG + h + 1) / n)


ATT_SUB = 8


def _pack_heads(xs):
    lane = lax.broadcasted_iota(jnp.int32, xs[0].shape, 1)
    out = xs[-1]
    for h in range(len(xs) - 2, -1, -1):
        out = jnp.where(lane == h, xs[h], out)
    return out


def _unpack_heads(x):
    return jnp.concatenate([jnp.broadcast_to(x[:, h:h + 1], (x.shape[0], ATT_HD)) for h in range(ATT_HG)], axis=-1)


def _att_prompt_kernel(q_ref, kp_ref, kc_ref, vp_ref, vc_ref, o_ref, lse_ref, k_scr, v_scr, bias_scr,
                       *, group, dil, nsub):
    j = pl.program_id(2)
    blk = ATT_BLK
    hd = ATT_HD
    k_scr[0:blk] = kp_ref[...]
    k_scr[blk:] = kc_ref[...]
    ones = jnp.ones((blk * (nsub + 1), hd), BF16)
    for h in range(ATT_HG):
        v_scr[0:blk, 2 * h * hd:(2 * h + 1) * hd] = vp_ref[:, h * hd:(h + 1) * hd]
        v_scr[blk:, 2 * h * hd:(2 * h + 1) * hd] = vc_ref[:, h * hd:(h + 1) * hd]
        v_scr[:, (2 * h + 1) * hd:(2 * h + 2) * hd] = ones

    r = lax.broadcasted_iota(jnp.int32, (blk, 2 * blk), 0)
    c = lax.broadcasted_iota(jnp.int32, (blk, 2 * blk), 1)
    dist = blk + r - c
    in_band = (dist >= 0) & (dist <= ATT_SPAN)
    for h in range(ATT_HG):
        bias_scr[h] = jnp.where(in_band, (-_alibi_slope(group, h) * dil) * dist.astype(F32), NEG)
    no_prev = jnp.where((c < blk) & (j == 0), NEG, 0.0)

    scale = hd ** -0.5
    for i in range(nsub):
        rows = slice(i * blk, (i + 1) * blk)
        keys = slice(i * blk, (i + 2) * blk)
        heads = range(ATT_HG)
        ss = [_dot_nt(q_ref[rows, h * hd:(h + 1) * hd], k_scr[keys, h * hd:(h + 1) * hd]) * scale + bias_scr[h]
              for h in heads]
        if i == 0:
            ss = [s + no_prev for s in ss]
        ms = [s.max(axis=-1, keepdims=True) for s in ss]
        ps = [jnp.exp(s - m).astype(BF16) for s, m in zip(ss, ms)]
        rs = [_dot(p, v_scr[keys, 2 * h * hd:(2 * h + 2) * hd]) for h, p in zip(heads, ps)]
        lses = []
        for h, res, m in zip(heads, rs, ms):
            l = res[:, hd:]
            o_ref[rows, h * hd:(h + 1) * hd] = (res[:, :hd] / l).astype(o_ref.dtype)
            lses.append(m + jnp.log(l))
        lse_ref[rows, :] = _pack_heads(lses)


def _att_prompt(qkv, group):
    B, dil, L, _ = qkv.shape
    nsub = math.gcd(ATT_SUB, L // ATT_BLK)
    rows = nsub * ATT_BLK
    cur = lambda which: pl.BlockSpec((None, None, rows, ATT_GW), lambda b, r, j: (b, r, j, which))
    prev = lambda which: pl.BlockSpec(
        (None, None, ATT_BLK, ATT_GW), lambda b, r, j: (b, r, jnp.maximum(j * nsub - 1, 0), which))
    out_spec = lambda w: pl.BlockSpec((None, None, rows, w), lambda b, r, j: (b, r, j, 0))
    return pl.pallas_call(
        functools.partial(_att_prompt_kernel, group=group, dil=dil, nsub=nsub),
        grid=(B, dil, L // rows),
        in_specs=[cur(0), prev(1), cur(1), prev(2), cur(2)],
        out_specs=[out_spec(ATT_GW), out_spec(LANES)],
        out_shape=[
            jax.ShapeDtypeStruct((B, dil, L, ATT_GW), BF16),
            jax.ShapeDtypeStruct((B, dil, L, LANES), F32),
        ],
        scratch_shapes=[
            pltpu.VMEM((rows + ATT_BLK, ATT_GW), BF16),
            pltpu.VMEM((rows + ATT_BLK, 2 * ATT_GW), BF16),
            pltpu.VMEM((ATT_HG, ATT_BLK, 2 * ATT_BLK), F32),
        ],
        compiler_params=_params(("arbitrary", "arbitrary", "arbitrary"), 32),
        name=f"att_prompt_g{group}",
    )(qkv, qkv, qkv, qkv, qkv)


def _att_sample_kernel(q_ref, k_ref, v_ref, c_ref, o_ref, lse_ref, nc_ref, *, group, dil, wb, T, nres):
    slots = 2 * ATT_HG
    nk = c_ref.shape[0] * nres if nres else wb
    t = lax.broadcasted_iota(jnp.int32, (T, nk), 0)
    c = lax.broadcasted_iota(jnp.int32, (T, nk), 1)
    pos = (c // nres) * dil + c % nres if nres else c
    dist_c = wb + t - pos
    valid_c = ((dist_c & (dil - 1)) == 0) & (dist_c <= ATT_SPAN * dil)
    tn = lax.broadcasted_iota(jnp.int32, (T, ATT_BLK), 0)
    cn = lax.broadcasted_iota(jnp.int32, (T, ATT_BLK), 1)
    dist_n = tn - cn
    valid_n = (dist_n >= 0) & ((dist_n & (dil - 1)) == 0) & (cn < T)
    scale = ATT_HD ** -0.5
    heads = range(ATT_HG)
    sls = [slice(h * ATT_HD, (h + 1) * ATT_HD) for h in heads]
    for h, sl in zip(heads, sls):
        nc_ref[pl.ds(h, T, stride=slots), :] = k_ref[:, sl]
        nc_ref[pl.ds(ATT_HG + h, T, stride=slots), :] = v_ref[:, sl]
    qs = [q_ref[:, sl].astype(BF16) for sl in sls]
    if nres:
        slot_rows = lambda s: c_ref[:, pl.ds(s, nres, stride=slots), :].reshape(nk, ATT_HD).astype(BF16)
    else:
        slot_rows = lambda s: c_ref[pl.ds(s, wb, stride=slots), :].astype(BF16)
    kcs = [slot_rows(h) for h in heads]
    vcs = [slot_rows(ATT_HG + h) for h in heads]
    kns = [_pad_rows(k_ref[:, sl], ATT_BLK).astype(BF16) for sl in sls]
    vns = [_pad_rows(v_ref[:, sl], ATT_BLK).astype(BF16) for sl in sls]
    bias_c = jnp.where(valid_c, dist_c.astype(F32), -NEG)
    bias_n = jnp.where(valid_n, dist_n.astype(F32), -NEG)
    scs = [_dot_nt(q, kc) * scale - _alibi_slope(group, h) * bias_c for h, q, kc in zip(heads, qs, kcs)]
    sns = [_dot_nt(q, kn) * scale - _alibi_slope(group, h) * bias_n for h, q, kn in zip(heads, qs, kns)]
    ms = [jnp.maximum(sc.max(axis=-1, keepdims=True), sn.max(axis=-1, keepdims=True)) for sc, sn in zip(scs, sns)]
    pcs = [jnp.exp(sc - m) for sc, m in zip(scs, ms)]
    pns = [jnp.exp(sn - m) for sn, m in zip(sns, ms)]
    ls = [pc.sum(axis=-1, keepdims=True) + pn.sum(axis=-1, keepdims=True) for pc, pn in zip(pcs, pns)]
    accs = [_dot(pc.astype(BF16), vc) + _dot(pn.astype(BF16), vn) for pc, pn, vc, vn in zip(pcs, pns, vcs, vns)]
    for sl, acc, l in zip(sls, accs, ls):
        o_ref[:, sl] = (acc / l).astype(o_ref.dtype)
    lse_ref[...] = _pack_heads([jnp.broadcast_to(m + jnp.log(l), (T, LANES)) for l, m in zip(ls, ms)])


def _cache_rows(cache):
    return cache.reshape(cache.shape[0], cache.shape[1] * 2 * ATT_HG, ATT_HD)


def _att_sample(proj_s, cache, group, T):
    nb, wb = cache.shape[0], cache.shape[1]
    assert wb > T
    _, dil = ATT_GROUPS[group]
    c0 = 3 * group
    slots = 2 * ATT_HG
    col = lambda k: pl.BlockSpec((T, ATT_GW), lambda b, k=k: (b, c0 + k))
    out_spec = pl.BlockSpec((T, ATT_GW), lambda b: (b, 0))
    rows = _cache_rows(cache)
    if dil > T and wb % dil == 0 and T % 8 == 0:
        nres = T
        rows = rows.reshape(nb, wb // dil, dil * slots, ATT_HD)
        cache_spec = pl.BlockSpec((None, wb // dil, nres * slots, ATT_HD), lambda b: (b, 0, 0, 0))
    else:
        nres = 0
        cache_spec = pl.BlockSpec((None, wb * slots, ATT_HD), lambda b: (b, 0, 0))
    return pl.pallas_call(
        functools.partial(_att_sample_kernel, group=group, dil=dil, wb=wb, T=T, nres=nres),
        grid=(nb,),
        in_specs=[col(0), col(1), col(2), cache_spec],
        out_specs=[out_spec, pl.BlockSpec((T, LANES), lambda b: (b, 0)),
                   pl.BlockSpec((None, T * slots, ATT_HD), lambda b: (b, 0, 0))],
        out_shape=[
            jax.ShapeDtypeStruct((nb * T, ATT_GW), F32),
            jax.ShapeDtypeStruct((nb * T, LANES), F32),
            jax.ShapeDtypeStruct((nb, T * slots, ATT_HD), F32),
        ],
        compiler_params=_params(("arbitrary",), 48),
        name=f"att_sample_g{group}",
    )(proj_s, proj_s, proj_s, rows)


def _epilogue_kernel(x_ref, gr_ref, ga_ref, yret_ref, o0_ref, o1_ref, o2_ref, l0_ref, l1_ref, l2_ref,
                     wr_ref, wa_ref, wo_ref, gf_ref, x1_ref, h2_ref, *scratch, by_residue):
    def position_order(ref, scr):
        dil = ref.shape[0]
        if dil == 1:
            return ref[0].astype(F32)
        for r in range(dil):
            slab = ref[r].astype(F32)
            for cb in range(scr.shape[0]):
                scr[cb, pl.ds(r, ref.shape[1], stride=dil), :] = slab[:, cb * LANES:(cb + 1) * LANES]
        return jnp.concatenate([scr[cb] for cb in range(scr.shape[0])], axis=-1)

    if by_residue:
        o0, o1, o2 = (position_order(ref, scr) for ref, scr in zip((o0_ref, o1_ref, o2_ref), scratch[0:3]))
        l0, l1, l2 = (position_order(ref, scr) for ref, scr in zip((l0_ref, l1_ref, l2_ref), scratch[3:6]))
    else:
        o0, o1, o2 = o0_ref[...], o1_ref[...], o2_ref[...]
        l0, l1, l2 = l0_ref[...], l1_ref[...], l2_ref[...]
    m = jnp.maximum(jnp.maximum(l0, l1), l2)
    e0, e1, e2 = jnp.exp(l0 - m), jnp.exp(l1 - m), jnp.exp(l2 - m)
    inv = 1.0 / (e0 + e1 + e2)
    o = _unpack_heads(e0 * inv) * o0 + _unpack_heads(e1 * inv) * o1 + _unpack_heads(e2 * inv) * o2
    att = _dot(o.astype(BF16), wa_ref[...])
    ret = _dot(yret_ref[...].astype(BF16), wr_ref[...])
    merged = (jax.nn.sigmoid(gr_ref[...].astype(F32)) * ret + jax.nn.sigmoid(ga_ref[...].astype(F32)) * att)
    x1 = x_ref[...] + _dot(merged.astype(BF16), wo_ref[...])
    x1_ref[...] = x1
    h2_ref[...] = _rmsnorm_rows(x1, gf_ref[...]).astype(BF16)


def _epilogue(x2d, proj2, yret, os_, lses, w_ret_o, w_att_o, w_o, g_ffn, *, tm, seq_len=None):
    M = x2d.shape[0]
    row = lambda w: pl.BlockSpec((tm, w), lambda m: (m, 0))
    const = lambda a: pl.BlockSpec(a.shape, lambda m: (0, 0), pipeline_mode=pl.Buffered(1))
    gf = g_ffn.reshape(1, D_MODEL)
    by_residue = seq_len is not None
    scratch = []
    if by_residue:
        tps = seq_len // tm
        att_specs = [pl.BlockSpec((None, a.shape[1], tm // a.shape[1], a.shape[3]),
                                  lambda m: (m // tps, 0, m % tps, 0)) for a in (*os_, *lses)]
        scratch = [pltpu.VMEM((a.shape[3] // LANES, tm, LANES), F32) for a in (*os_, *lses)]
    else:
        att_specs = [row(a.shape[1]) for a in (*os_, *lses)]
    return pl.pallas_call(
        functools.partial(_epilogue_kernel, by_residue=by_residue),
        grid=(M // tm,),
        in_specs=[
            row(D_MODEL),
            pl.BlockSpec((tm, D_MODEL), lambda m: (m, 0)),
            pl.BlockSpec((tm, D_MODEL), lambda m: (m, 1)),
            row(RET_WIDTH), *att_specs,
            const(w_ret_o), const(w_att_o), const(w_o), const(gf),
        ],
        out_specs=[row(D_MODEL), row(D_MODEL)],
        out_shape=[jax.ShapeDtypeStruct((M, D_MODEL), F32), jax.ShapeDtypeStruct((M, D_MODEL), BF16)],
        scratch_shapes=scratch,
        compiler_params=_params(("arbitrary",), 56),
        name="epilogue",
    )(x2d, proj2, proj2, yret, *os_, *lses, w_ret_o, w_att_o, w_o, gf)


def _gelu_gate(a, half_b):
    return (a * (1.0 + lax.erf(a * math.sqrt(0.5)))) * half_b


def _halve_b_half(x):
    return jnp.concatenate([x[..., :D_FF], 0.5 * x[..., D_FF:]], axis=-1)


def _conv3(u, r1, r2, w_ref, b_ref):
    return b_ref[...] + (r2 * w_ref[0:1, :] + r1 * w_ref[1:2, :] + u * w_ref[2:3, :])


UP_CHUNK = 256


SHIFT_CHUNK_ROWS = 4096


def _shift_plan(cache_rows, T, steps):
    nb, n, _ = cache_rows.shape
    shift = T * 2 * ATT_HG
    body = n - shift
    per_seq = -(-body // SHIFT_CHUNK_ROWS)
    while body % per_seq or (body // per_seq) % 8:
        per_seq += 1
    plan = dict(shift=shift, chunk=body // per_seq, per_seq=per_seq, count=nb * per_seq)
    assert plan["count"] < steps, "the background copy needs one grid step per chunk plus one to drain"
    return plan


def _shift_copy_step(s, src_ref, new_ref, dst_ref, buf, sems, i, *, shift, chunk, per_seq, count):
    n = src_ref.shape[1]
    tail = pltpu.make_async_copy(new_ref, dst_ref.at[:, pl.ds(n - shift, shift)], sems.at[i, 4])

    @pl.when(s == 0)
    def _():
        tail.start()

    @pl.when(s == count)
    def _():
        tail.wait()

    def read(c, slot):
        start = pl.multiple_of(shift + (c % per_seq) * chunk, 8)
        return pltpu.make_async_copy(src_ref.at[c // per_seq, pl.ds(start, chunk)], buf.at[slot], sems.at[i, slot])

    def write(c, slot):
        start = pl.multiple_of((c % per_seq) * chunk, 8)
        return pltpu.make_async_copy(buf.at[slot], dst_ref.at[c // per_seq, pl.ds(start, chunk)],
                                     sems.at[i, 2 + slot])

    slot = s % 2

    @pl.when(s == 0)
    def _():
        read(0, 0).start()

    @pl.when(s < count)
    def _():
        read(s, slot).wait()
        write(s, slot).start()

    @pl.when((s >= 1) & (s <= count))
    def _():
        write(s - 1, 1 - slot).wait()

    @pl.when(s + 1 < count)
    def _():
        read(s + 1, 1 - slot).start()


def _up_gate_kernel(h_ref, w_ref, cwa_ref, cwb_ref, cba_ref, cbb_ref, *rest, tiles_per_seq, shift_plans):
    ng = len(shift_plans)
    caches, new_rows, rest = rest[:ng], rest[ng:2 * ng], rest[2 * ng:]
    g_ref, ta_ref, tb_ref = rest[:3]
    advanced, rest = rest[3:3 + ng], rest[3 + ng:]
    u_scr, carry_scr = rest[:2]
    bufs, (sems,) = rest[2:2 + ng], rest[2 + ng:]
    m = pl.program_id(0)
    f = pl.program_id(1)
    tm = h_ref.shape[0]

    step = m * pl.num_programs(1) + f
    for i, plan in enumerate(shift_plans):
        _shift_copy_step(step, caches[i], new_rows[i], advanced[i], bufs[i], sems, i, **plan)

    @pl.when(m % tiles_per_seq == 0)
    def _():
        carry_scr[f] = jnp.zeros(carry_scr.shape[1:], F32)

    u_scr[0:8, :] = carry_scr[f]
    u_scr[8:, :] = _dot(h_ref[...], w_ref[...])
    carry_scr[f] = u_scr[tm:, :]
    for c in range(g_ref.shape[1] // UP_CHUNK):
        cols = slice(c * UP_CHUNK, (c + 1) * UP_CHUNK)

        def conv(half, cw_ref, cb_ref, tail_ref):
            pcols = slice((2 * c + half) * UP_CHUNK, (2 * c + half + 1) * UP_CHUNK)
            tail_ref[:, cols] = u_scr[tm:, pcols]
            u, r1, r2 = u_scr[8:, pcols], u_scr[7:tm + 7, pcols], u_scr[6:tm + 6, pcols]
            return cb_ref[:, cols] + (r2 * cw_ref[0:1, cols] + r1 * cw_ref[1:2, cols] + u * cw_ref[2:3, cols])

        a = conv(0, cwa_ref, cba_ref, ta_ref)
        b = conv(1, cwb_ref, cbb_ref, tb_ref)
        g_ref[:, cols] = _gelu_gate(a, b).astype(BF16)


def _up_gate(h2, w_up_ab, conv_w, conv_b, caches, new_rows, T, *, tm, tiles_per_seq):
    M = h2.shape[0]
    tf = 512
    nf = D_FF // tf
    cb = conv_b.reshape(1, 2 * D_FF)
    grid = (M // tm, nf)
    plans = tuple(_shift_plan(c, T, grid[0] * grid[1]) for c in caches)
    any_spec = pl.BlockSpec(memory_space=pl.ANY)
    return pl.pallas_call(
        functools.partial(_up_gate_kernel, tiles_per_seq=tiles_per_seq, shift_plans=plans),
        grid=grid,
        in_specs=[
            pl.BlockSpec((tm, D_MODEL), lambda m, f: (m, 0)),
            pl.BlockSpec((D_MODEL, 2 * tf), lambda m, f: (0, f)),
            pl.BlockSpec((CONV_W, tf), lambda m, f: (0, f)),
            pl.BlockSpec((CONV_W, tf), lambda m, f: (0, f + nf)),
            pl.BlockSpec((1, tf), lambda m, f: (0, f)),
            pl.BlockSpec((1, tf), lambda m, f: (0, f + nf)),
            *[any_spec for _ in (*caches, *new_rows)],
        ],
        out_specs=[
            pl.BlockSpec((tm, tf), lambda m, f: (m, f)),
            pl.BlockSpec((None, 8, tf), lambda m, f: (m, 0, f)),
            pl.BlockSpec((None, 8, tf), lambda m, f: (m, 0, f)),
            *[any_spec for _ in caches],
        ],
        out_shape=[
            jax.ShapeDtypeStruct((M, D_FF), BF16),
            jax.ShapeDtypeStruct((M // tm, 8, D_FF), F32),
            jax.ShapeDtypeStruct((M // tm, 8, D_FF), F32),
            *[jax.ShapeDtypeStruct(c.shape, F32) for c in caches],
        ],
        scratch_shapes=[
            pltpu.VMEM((tm + 8, 2 * tf), F32),
            pltpu.VMEM((nf, 8, 2 * tf), F32),
            *[pltpu.VMEM((2, p["chunk"], ATT_HD), F32) for p in plans],
            pltpu.SemaphoreType.DMA((len(caches), 5)),
        ],
        compiler_params=_params(("arbitrary", "arbitrary"), 56),
        name="up_gate",
    )(h2, w_up_ab, conv_w, conv_w, cb, cb, *caches, *new_rows)


def _down_kernel(g_ref, wd_ref, x1_ref, gfin_ref, y_ref, *, final_norm):
    y = x1_ref[...] + _dot(g_ref[...], wd_ref[...])
    y_ref[...] = _rmsnorm_rows(y, gfin_ref[...]) if final_norm else y


def _down(g, x1, w_down, g_final, *, tm, final_norm):
    M = g.shape[0]
    return pl.pallas_call(
        functools.partial(_down_kernel, final_norm=final_norm),
        grid=(M // tm,),
        in_specs=[
            pl.BlockSpec((tm, D_FF), lambda m: (m, 0)),
            pl.BlockSpec((D_FF, D_MODEL), lambda m: (0, 0), pipeline_mode=pl.Buffered(1)),
            pl.BlockSpec((tm, D_MODEL), lambda m: (m, 0)),
            pl.BlockSpec((1, D_MODEL), lambda m: (0, 0)),
        ],
        out_specs=pl.BlockSpec((tm, D_MODEL), lambda m: (m, 0)),
        out_shape=jax.ShapeDtypeStruct((M, D_MODEL), F32),
        compiler_params=_params(("arbitrary",), 48),
        name="down",
    )(g, w_down, x1, g_final.reshape(1, D_MODEL))


def _ffn_sample_kernel(h_ref, wua_ref, wub_ref, s0a_ref, s0b_ref, s1a_ref, s1b_ref, cwa_ref, cwb_ref,
                       cba_ref, cbb_ref, wd_ref, x1_ref, gfin_ref, y_ref, ua_ref, ub_ref, wu_cast_ref, wd_cast_ref,
                       *, T, final_norm):
    f = pl.program_id(0)
    M, tf = ua_ref.shape
    t = lax.broadcasted_iota(jnp.int32, (M, tf), 0) & (T - 1)
    wua, wub, wd = wua_ref[...].astype(BF16), wub_ref[...].astype(BF16), wd_ref[...].astype(BF16)
    for c in range(tf // UP_CHUNK):
        cols = slice(c * UP_CHUNK, (c + 1) * UP_CHUNK)
        wu_cast_ref[:, 2 * c * UP_CHUNK:(2 * c + 1) * UP_CHUNK] = wua[:, cols]
        wu_cast_ref[:, (2 * c + 1) * UP_CHUNK:(2 * c + 2) * UP_CHUNK] = wub[:, cols]
    wd_cast_ref[...] = wd
    h = h_ref[...]
    per_row = lambda s_ref: jnp.broadcast_to(s_ref[...][:, None, :], (M // T, T, tf)).reshape(M, tf)

    def conv(u, s0_ref, s1_ref, w_ref, b_ref, u_out_ref):
        u_out_ref[...] = u
        s0, s1 = per_row(s0_ref), per_row(s1_ref)
        r1 = jnp.where(t >= 1, pltpu.roll(u, 1, axis=0), s1)
        r2 = jnp.where(t >= 2, pltpu.roll(u, 2, axis=0), jnp.where(t == 0, s0, s1))
        return _conv3(u, r1, r2, w_ref, b_ref)

    a = conv(_dot(h, wua), s0a_ref, s1a_ref, cwa_ref, cba_ref, ua_ref)
    b = conv(_dot(h, wub), s0b_ref, s1b_ref, cwb_ref, cbb_ref, ub_ref)
    part = _dot(_gelu_gate(a, b).astype(BF16), wd)

    @pl.when(f == 0)
    def _():
        y_ref[...] = x1_ref[...] + part

    @pl.when(f > 0)
    def _():
        y_ref[...] += part

    if final_norm:
        @pl.when(f == pl.num_programs(0) - 1)
        def _():
            y_ref[...] = _rmsnorm_rows(y_ref[...], gfin_ref[...])


def _ffn_sample(h2, x1, state_conv, w_up, conv_w, conv_b, w_down, g_final, T, final_norm):
    M = h2.shape[0]
    nb = M // T
    tf = 512
    nf = D_FF // tf
    F2 = 2 * D_FF
    assert state_conv.shape[1] == CONV_W - 1 == 2
    s0, s1 = state_conv[:, 0].astype(F32), state_conv[:, 1].astype(F32)
    cb = conv_b.reshape(1, F2)
    gfin = g_final.reshape(1, D_MODEL)
    full = lambda w: pl.BlockSpec((M, w), lambda f: (0, 0))
    ca = lambda r, w=tf: pl.BlockSpec((r, w), lambda f: (0, f))
    cbk = lambda r, w=tf: pl.BlockSpec((r, w), lambda f: (0, f + nf))
    down_rows = pl.BlockSpec((tf, D_MODEL), lambda f: (f, 0))
    y, ua, ub, w_up_ab, w_down_bf16 = pl.pallas_call(
        functools.partial(_ffn_sample_kernel, T=T, final_norm=final_norm),
        grid=(nf,),
        in_specs=[
            full(D_MODEL), ca(D_MODEL), cbk(D_MODEL), ca(nb), cbk(nb), ca(nb), cbk(nb),
            ca(CONV_W), cbk(CONV_W), ca(1), cbk(1),
            down_rows,
            full(D_MODEL), pl.BlockSpec((1, D_MODEL), lambda f: (0, 0)),
        ],
        out_specs=[full(D_MODEL), ca(M), ca(M), ca(D_MODEL, 2 * tf), down_rows],
        out_shape=[
            jax.ShapeDtypeStruct((M, D_MODEL), F32),
            jax.ShapeDtypeStruct((M, D_FF), F32),
            jax.ShapeDtypeStruct((M, D_FF), F32),
            jax.ShapeDtypeStruct((D_MODEL, F2), BF16),
            jax.ShapeDtypeStruct((D_FF, D_MODEL), BF16),
        ],
        compiler_params=_params(("arbitrary",), 56),
        name="ffn_sample",
    )(h2, w_up, w_up, s0, s0, s1, s1, conv_w, conv_w, cb, cb, w_down, x1, gfin)
    u = jnp.concatenate([ua, ub], axis=-1).reshape(nb, T, F2)
    return y, u[:, T - (CONV_W - 1):], w_up_ab, w_down_bf16


def kernel(x_prompt, x_sample, state_ret, cache_kv_w128, cache_kv_w512, cache_kv_w2048, state_conv, g_mix, w_in,
           gn_ret, w_ret_o, w_att_o, w_o, g_ffn, w_up, conv_w, conv_b, w_down, g_final):
    B, S, _ = x_prompt.shape
    NB, T, _ = x_sample.shape
    depth = w_in.shape[0]
    caches = (cache_kv_w128, cache_kv_w512, cache_kv_w2048)
    TM = 1024
    tiles_per_seq = S // TM

    xp = x_prompt.reshape(B * S, D_MODEL)
    xs = x_sample.reshape(NB * T, D_MODEL)
    ret_p, ret_s, conv_p, conv_s = [], [], [], []
    kv_p = [[] for _ in range(N_GROUPS)]
    kv_s = [[] for _ in range(N_GROUPS)]
    for l in range(depth):
        w_ret_o_l = w_ret_o[l].astype(BF16)
        w_att_o_l = w_att_o[l].astype(BF16)
        w_o_l = w_o[l].astype(BF16)
        conv_w_l, conv_b_l = _halve_b_half(conv_w[l]), _halve_b_half(conv_b[l])

        ret_tiles = 4 * RET_WIDTH // CAST_TN
        att_tiles = ATT_COLS // CAST_TN
        in_tiles = IN_COLS // CAST_TN
        proj_s, w_main_l = _inproj_sample(xs, g_mix[l], w_in[l], lambda n: (n + ret_tiles + att_tiles) % in_tiles,
                                          MAIN_COLS // CAST_TN, "inproj_sample_main")
        att_s, w_att_l = _inproj_sample(xs, g_mix[l], w_in[l], lambda n: n + ret_tiles, att_tiles,
                                        "inproj_sample_att")
        yret_s, st_s = _ret_sample(proj_s, state_ret[l], gn_ret[l], T)
        ret_s.append(st_s)
        os_, lses, new_rows = [], [], []
        for g in range(N_GROUPS):
            o, lse, rows = _att_sample(att_s, caches[g][l], g, T)
            os_.append(o)
            lses.append(lse)
            new_rows.append(rows)
        x1s, h2s = _epilogue(xs, proj_s, yret_s, os_, lses, w_ret_o_l, w_att_o_l, w_o_l, g_ffn[l], tm=NB * T)
        xs, cv, w_up_l, w_down_l = _ffn_sample(h2s, x1s, state_conv[l], w_up[l], conv_w_l, conv_b_l, w_down[l],
                                               g_final, T, final_norm=l == depth - 1)
        conv_s.append(cv)

        proj, h = _inproj_main(xp, g_mix[l], w_main_l, tm=TM)
        yret, st = _ret_prompt(proj.reshape(B, S, MAIN_COLS), gn_ret[l])
        ret_p.append(st)
        os_, lses = [], []
        for g, (win, _) in enumerate(ATT_GROUPS):
            qkv, kvf = _inproj_att(h, w_att_l, g, B=B, S=S, tm=512)
            o, lse = _att_prompt(qkv, g)
            os_.append(o)
            lses.append(lse)
            kv_p[g].append(kvf.reshape(B, min(win, S), 2, ATT_HG, ATT_HD))
        x1, h2 = _epilogue(xp, proj, yret.reshape(B * S, RET_WIDTH), os_, lses,
                           w_ret_o_l, w_att_o_l, w_o_l, g_ffn[l], tm=256, seq_len=S)
        gated, tail_a, tail_b, *advanced = _up_gate(h2, w_up_l, conv_w_l, conv_b_l,
                                                    [_cache_rows(c[l]) for c in caches], new_rows, T,
                                                    tm=TM, tiles_per_seq=tiles_per_seq)
        for g in range(N_GROUPS):
            kv_s[g].append(advanced[g].reshape(caches[g][l].shape))
        utail = jnp.concatenate([tail_a, tail_b], axis=-1)
        conv_p.append(utail[tiles_per_seq - 1::tiles_per_seq, 8 - (CONV_W - 1):])
        xp = _down(gated, x1, w_down_l, g_final, tm=256, final_norm=l == depth - 1)

    return (xp.reshape(B, S, D_MODEL), xs.reshape(NB, T, D_MODEL),
            jnp.stack(ret_p), jnp.stack(ret_s),
            jnp.stack(kv_p[0]), jnp.stack(kv_s[0]),
            jnp.stack(kv_p[1]), jnp.stack(kv_s[1]),
            jnp.stack(kv_p[2]), jnp.stack(kv_s[2]),
            jnp.stack(conv_p), jnp.stack(conv_s))
```

```python
import functools
import math

import jax
import jax.numpy as jnp
from jax import lax
from jax.experimental import pallas as pl
from jax.experimental.pallas import tpu as pltpu

D_MODEL = 2048
RET_HEADS = 8
RET_DK = 128
RET_DV = 128
RET_WIDTH = RET_HEADS * RET_DV
RET_CHUNK = 128
ATT_GROUPS = ((128, 1), (512, 4), (2048, 16))
N_GROUPS = 3
ATT_HG = 4
ATT_HD = 128
ATT_GW = ATT_HG * ATT_HD
ATT_SPAN = 128
ATT_BLK = 128
ALIBI_MAX = 8.0
D_FF = 5632
CONV_W = 3
EPS = 1e-6
NEG = -1e30
IN_COLS = 4 * RET_WIDTH + 3 * N_GROUPS * ATT_GW + 2 * D_MODEL

GATE_COLS = 2 * D_MODEL
RET_COL0 = GATE_COLS
MAIN_COLS = GATE_COLS + 4 * RET_WIDTH
ATT_COLS = 3 * N_GROUPS * ATT_GW
CAST_TN = 512

LANES = 128
MIB = 1024 * 1024
BF16 = jnp.bfloat16
F32 = jnp.float32


def _params(semantics, vmem_mib, flags=None):
    return pltpu.CompilerParams(dimension_semantics=semantics, vmem_limit_bytes=vmem_mib * MIB, flags=flags)


def _dot(a, b):
    return jnp.dot(a, b, preferred_element_type=F32)


def _dot_nt(a, b):
    return lax.dot_general(a, b, (((1,), (1,)), ((), ())), preferred_element_type=F32)


def _rmsnorm_rows(x, g):
    ms = jnp.mean(x * x, axis=-1, keepdims=True)
    return x * lax.rsqrt(ms + EPS) * g


def _inproj_sample_kernel(x_ref, g_ref, w_ref, proj_ref, wcast_ref, h_scr):
    @pl.when(pl.program_id(0) == 0)
    def _():
        h_scr[...] = _rmsnorm_rows(x_ref[...], g_ref[...]).astype(BF16)

    w = w_ref[...].astype(BF16)
    wcast_ref[...] = w
    proj_ref[...] = _dot(h_scr[...], w)


def _inproj_sample(x2d, g, w, src_tile, n_tiles, name):
    M = x2d.shape[0]
    tn = CAST_TN
    return pl.pallas_call(
        _inproj_sample_kernel,
        grid=(n_tiles,),
        in_specs=[
            pl.BlockSpec((M, D_MODEL), lambda n: (0, 0)),
            pl.BlockSpec((1, D_MODEL), lambda n: (0, 0)),
            pl.BlockSpec((D_MODEL, tn), lambda n: (0, src_tile(n))),
        ],
        out_specs=[pl.BlockSpec((M, tn), lambda n: (0, n)), pl.BlockSpec((D_MODEL, tn), lambda n: (0, n))],
        out_shape=[jax.ShapeDtypeStruct((M, n_tiles * tn), F32), jax.ShapeDtypeStruct((D_MODEL, n_tiles * tn), BF16)],
        scratch_shapes=[pltpu.VMEM((M, D_MODEL), BF16)],
        compiler_params=_params(("arbitrary",), 32),
        name=name,
    )(x2d, g.reshape(1, D_MODEL), w)


def _inproj_main_kernel(x_ref, g_ref, w_ref, main_ref, h_ref):
    @pl.when(pl.program_id(1) == 0)
    def _():
        h_ref[...] = _rmsnorm_rows(x_ref[...], g_ref[...]).astype(BF16)

    main_ref[...] = _dot(h_ref[...], w_ref[...]).astype(BF16)


def _inproj_main(x2d, g, w_main, *, tm):
    M = x2d.shape[0]
    N = w_main.shape[1]
    tn = 1024
    return pl.pallas_call(
        _inproj_main_kernel,
        grid=(M // tm, N // tn),
        in_specs=[
            pl.BlockSpec((tm, D_MODEL), lambda m, n: (m, 0)),
            pl.BlockSpec((1, D_MODEL), lambda m, n: (0, 0)),
            pl.BlockSpec((D_MODEL, tn), lambda m, n: (0, n)),
        ],
        out_specs=[pl.BlockSpec((tm, tn), lambda m, n: (m, n)), pl.BlockSpec((tm, D_MODEL), lambda m, n: (m, 0))],
        out_shape=[jax.ShapeDtypeStruct((M, N), BF16), jax.ShapeDtypeStruct((M, D_MODEL), BF16)],
        compiler_params=_params(("arbitrary", "arbitrary"), 48),
        name="inproj_main",
    )(x2d, g.reshape(1, D_MODEL), w_main)


def _inproj_att_kernel(h_ref, w_ref, a_ref, kv_ref, *scratch, dil):
    acc = _dot(h_ref[...], w_ref[...])
    slots = 2 * ATT_HG
    rows = kv_ref.shape[0] // slots
    for slot in range(slots):
        cols = slice(ATT_GW + slot * ATT_HD, ATT_GW + (slot + 1) * ATT_HD)
        kv_ref[pl.ds(slot, rows, stride=slots), :] = acc[acc.shape[0] - rows:, cols]
    if dil == 1:
        a_ref[0] = acc.astype(BF16)
    else:
        acc_scr, = scratch
        rows = acc_scr.shape[1] // dil
        for cb in range(acc_scr.shape[0]):
            lanes = slice(cb * LANES, (cb + 1) * LANES)
            acc_scr[cb] = acc[:, lanes]
            for r in range(dil):
                a_ref[r, :, lanes] = acc_scr[cb, pl.ds(r, rows, stride=dil), :].astype(BF16)


def _inproj_att(h, w_att, group, *, B, S, tm):
    M = h.shape[0]
    tn = 3 * ATT_GW
    tps = S // tm
    win, dil = ATT_GROUPS[group]
    win = min(win, S)
    rows = min(win, tm)
    first = tps - win // rows
    slots = 2 * ATT_HG
    return pl.pallas_call(
        functools.partial(_inproj_att_kernel, dil=dil),
        grid=(M // tm,),
        in_specs=[
            pl.BlockSpec((tm, D_MODEL), lambda m: (m, 0)),
            pl.BlockSpec((D_MODEL, tn), lambda m: (0, group)),
        ],
        out_specs=[
            pl.BlockSpec((None, dil, tm // dil, tn), lambda m: (m // tps, 0, m % tps, 0)),
            pl.BlockSpec((None, rows * slots, ATT_HD), lambda m: (m // tps, jnp.maximum(m % tps - first, 0), 0)),
        ],
        out_shape=[
            jax.ShapeDtypeStruct((B, dil, S // dil, tn), BF16),
            jax.ShapeDtypeStruct((B, win * slots, ATT_HD), F32),
        ],
        scratch_shapes=[] if dil == 1 else [pltpu.VMEM((tn // LANES, tm, LANES), F32)],
        compiler_params=_params(("arbitrary",), 48),
        name=f"inproj_att_g{group}",
    )(h, w_att)


def _ret_tables(lq):
    lg = jnp.log(1.0 - jnp.exp2(-5.0 - jnp.arange(RET_HEADS, dtype=F32)))
    scale = RET_DK ** -0.5
    i = jnp.arange(lq, dtype=F32)
    j = jnp.arange(RET_CHUNK, dtype=F32)
    diff = i[:, None] - j[None, :]
    live = (diff >= 0) & (j[None, :] < lq)
    dm = jnp.where(live[None], jnp.exp(jnp.maximum(diff, 0.0)[None] * lg[:, None, None]), 0.0) * scale
    qd = jnp.exp((i[None, :] + 1.0) * lg[:, None])
    qd = jnp.broadcast_to(qd[:, :, None], (RET_HEADS, lq, RET_DV))
    kd = jnp.where(j[None, :] < lq, jnp.exp(jnp.maximum(lq - 1.0 - j, 0.0)[None, :] * lg[:, None]), 0.0) * scale
    kd = jnp.broadcast_to(kd[:, :, None], (RET_HEADS, RET_CHUNK, RET_DK))
    gl = jnp.exp(lq * lg)
    return dm.astype(F32), qd.astype(F32), kd.astype(F32), gl.astype(F32)


def _ret_heads(qs, ks, vs, sts, dms, qds, kds, gls):
    ss = [(_dot_nt(q, k) * dm).astype(BF16) for q, k, dm in zip(qs, ks, dms)]
    crosses = [_dot(q, st.astype(BF16)) for q, st in zip(qs, sts)]
    kts = [(k.astype(F32) * kd).T.astype(BF16) for k, kd in zip(ks, kds)]
    inners = [_dot(s, v) for s, v in zip(ss, vs)]
    updates = [_dot(kt, v) for kt, v in zip(kts, vs)]
    ys = [inner + cross * qd for inner, cross, qd in zip(inners, crosses, qds)]
    new_sts = [st * gl + upd for st, gl, upd in zip(sts, gls, updates)]
    return ys, new_sts


def _groupnorm_gates(ys, gns, rgs):
    mus = [jnp.mean(y, axis=-1, keepdims=True) for y in ys]
    ycs = [y - mu for y, mu in zip(ys, mus)]
    rstds = [lax.rsqrt(jnp.mean(yc * yc, axis=-1, keepdims=True) + EPS) for yc in ycs]
    return [(rg * jax.nn.sigmoid(rg)) * ((yc * rstd) * gn) for yc, rstd, gn, rg in zip(ycs, rstds, gns, rgs)]


def _head_slices():
    return [slice(h * RET_DK, (h + 1) * RET_DK) for h in range(RET_HEADS)]


def _ret_prompt_kernel(gl_ref, q_ref, k_ref, v_ref, rg_ref, dm_ref, qd_ref, kd_ref, gn_ref, y_ref, st_ref):
    c = pl.program_id(1)

    @pl.when(c == 0)
    def _():
        st_ref[...] = jnp.zeros_like(st_ref)

    sls = _head_slices()
    heads = range(RET_HEADS)
    sts = [st_ref[h] for h in heads]
    for i in range(q_ref.shape[0] // RET_CHUNK):
        rows = slice(i * RET_CHUNK, (i + 1) * RET_CHUNK)
        ys, sts = _ret_heads([q_ref[rows, sl] for sl in sls], [k_ref[rows, sl] for sl in sls],
                             [v_ref[rows, sl] for sl in sls], sts, [dm_ref[h] for h in heads],
                             [qd_ref[h] for h in heads], [kd_ref[h] for h in heads], [gl_ref[h] for h in heads])
        outs = _groupnorm_gates(ys, [gn_ref[:, sl] for sl in sls], [rg_ref[rows, sl].astype(F32) for sl in sls])
        for sl, out in zip(sls, outs):
            y_ref[rows, sl] = out.astype(y_ref.dtype)
    for h in heads:
        st_ref[h] = sts[h]


RET_STEP_CHUNKS = 2


def _ret_prompt(proj3, gn_ret):
    B, S, _ = proj3.shape
    L = RET_CHUNK
    dm, qd, kd, gl = _ret_tables(L)
    cb = RET_COL0 // RET_WIDTH
    rows = L * math.gcd(RET_STEP_CHUNKS, S // L)
    col = lambda k: pl.BlockSpec((None, rows, RET_WIDTH), lambda b, c, k=k: (b, c, cb + k))
    tab = pl.BlockSpec((RET_HEADS, L, RET_DK), lambda b, c: (0, 0, 0))
    return pl.pallas_call(
        _ret_prompt_kernel,
        grid=(B, S // rows),
        in_specs=[
            pl.BlockSpec(memory_space=pltpu.SMEM),
            col(0), col(1), col(2), col(3), tab, tab, tab,
            pl.BlockSpec((1, RET_WIDTH), lambda b, c: (0, 0)),
        ],
        out_specs=[
            pl.BlockSpec((None, rows, RET_WIDTH), lambda b, c: (b, c, 0)),
            pl.BlockSpec((None, RET_HEADS, RET_DK, RET_DV), lambda b, c: (b, 0, 0, 0)),
        ],
        out_shape=[
            jax.ShapeDtypeStruct((B, S, RET_WIDTH), BF16),
            jax.ShapeDtypeStruct((B, RET_HEADS, RET_DK, RET_DV), F32),
        ],
        compiler_params=_params(("arbitrary", "arbitrary"), 32),
        name="ret_prompt",
    )(gl, proj3, proj3, proj3, proj3, dm, qd, kd, gn_ret.reshape(1, RET_WIDTH))


def _pad_rows(x, rows):
    x = x.astype(F32)
    return jnp.concatenate([x, jnp.zeros((rows - x.shape[0], x.shape[1]), F32)], axis=0)


def _ret_sample_kernel(gl_ref, q_ref, k_ref, v_ref, rg_ref, st_in_ref, dm_ref, qd_ref, kd_ref, gn_ref,
                       y_ref, st_ref, *, T):
    sls = _head_slices()
    heads = range(RET_HEADS)
    for j in range(st_ref.shape[0]):
        rows = slice(j * T, (j + 1) * T)
        ys, new_sts = _ret_heads([q_ref[rows, sl].astype(BF16) for sl in sls],
                                 [_pad_rows(k_ref[rows, sl], RET_CHUNK).astype(BF16) for sl in sls],
                                 [_pad_rows(v_ref[rows, sl], RET_CHUNK).astype(BF16) for sl in sls],
                                 [st_in_ref[j, h] for h in heads], [dm_ref[h] for h in heads],
                                 [qd_ref[h] for h in heads], [kd_ref[h] for h in heads], [gl_ref[h] for h in heads])
        for h in heads:
            st_ref[j, h] = new_sts[h]
        outs = _groupnorm_gates(ys, [gn_ref[:, sl] for sl in sls], [rg_ref[rows, sl].astype(F32) for sl in sls])
        for sl, out in zip(sls, outs):
            y_ref[rows, sl] = out.astype(y_ref.dtype)


SAMPLE_SEQS_PER_STEP = 4


def _ret_sample(proj_s, state, gn_ret, T):
    nb = state.shape[0]
    per = math.gcd(SAMPLE_SEQS_PER_STEP, nb)
    dm, qd, kd, gl = _ret_tables(T)
    cb = RET_COL0 // RET_WIDTH
    col = lambda k: pl.BlockSpec((per * T, RET_WIDTH), lambda b, k=k: (b, cb + k))
    st_spec = pl.BlockSpec((per, RET_HEADS, RET_DK, RET_DV), lambda b: (b, 0, 0, 0))
    return pl.pallas_call(
        functools.partial(_ret_sample_kernel, T=T),
        grid=(nb // per,),
        in_specs=[
            pl.BlockSpec(memory_space=pltpu.SMEM),
            col(0), col(1), col(2), col(3), st_spec,
            pl.BlockSpec((RET_HEADS, T, RET_CHUNK), lambda b: (0, 0, 0)),
            pl.BlockSpec((RET_HEADS, T, RET_DV), lambda b: (0, 0, 0)),
            pl.BlockSpec((RET_HEADS, RET_CHUNK, RET_DK), lambda b: (0, 0, 0)),
            pl.BlockSpec((1, RET_WIDTH), lambda b: (0, 0)),
        ],
        out_specs=[pl.BlockSpec((per * T, RET_WIDTH), lambda b: (b, 0)), st_spec],
        out_shape=[
            jax.ShapeDtypeStruct((nb * T, RET_WIDTH), F32),
            jax.ShapeDtypeStruct(state.shape, F32),
        ],
        compiler_params=_params(("arbitrary",), 32),
        name="ret_sample",
    )(gl, proj_s, proj_s, proj_s, proj_s, state, dm, qd, kd, gn_ret.reshape(1, RET_WIDTH))


def _alibi_slope(g, h):
    n = N_GROUPS * ATT_HG
    return 2.0 ** (-ALIBI_MAX * (g * ATT_HG + h + 1) / n)


ATT_SUB = 8


def _pack_heads(xs):
    lane = lax.broadcasted_iota(jnp.int32, xs[0].shape, 1)
    out = xs[-1]
    for h in range(len(xs) - 2, -1, -1):
        out = jnp.where(lane == h, xs[h], out)
    return out


def _unpack_heads(x):
    return jnp.concatenate([jnp.broadcast_to(x[:, h:h + 1], (x.shape[0], ATT_HD)) for h in range(ATT_HG)], axis=-1)


def _att_prompt_kernel(q_ref, kp_ref, kc_ref, vp_ref, vc_ref, o_ref, lse_ref, k_scr, v_scr, bias_scr,
                       *, group, dil, nsub):
    j = pl.program_id(2)
    blk = ATT_BLK
    hd = ATT_HD
    k_scr[0:blk] = kp_ref[...]
    k_scr[blk:] = kc_ref[...]
    ones = jnp.ones((blk * (nsub + 1), hd), BF16)
    for h in range(ATT_HG):
        v_scr[0:blk, 2 * h * hd:(2 * h + 1) * hd] = vp_ref[:, h * hd:(h + 1) * hd]
        v_scr[blk:, 2 * h * hd:(2 * h + 1) * hd] = vc_ref[:, h * hd:(h + 1) * hd]
        v_scr[:, (2 * h + 1) * hd:(2 * h + 2) * hd] = ones

    r = lax.broadcasted_iota(jnp.int32, (blk, 2 * blk), 0)
    c = lax.broadcasted_iota(jnp.int32, (blk, 2 * blk), 1)
    dist = blk + r - c
    in_band = (dist >= 0) & (dist <= ATT_SPAN)
    for h in range(ATT_HG):
        bias_scr[h] = jnp.where(in_band, (-_alibi_slope(group, h) * dil) * dist.astype(F32), NEG)
    no_prev = jnp.where((c < blk) & (j == 0), NEG, 0.0)

    scale = hd ** -0.5
    for i in range(nsub):
        rows = slice(i * blk, (i + 1) * blk)
        keys = slice(i * blk, (i + 2) * blk)
        heads = range(ATT_HG)
        ss = [_dot_nt(q_ref[rows, h * hd:(h + 1) * hd], k_scr[keys, h * hd:(h + 1) * hd]) * scale + bias_scr[h]
              for h in heads]
        if i == 0:
            ss = [s + no_prev for s in ss]
        ms = [s.max(axis=-1, keepdims=True) for s in ss]
        ps = [jnp.exp(s - m).astype(BF16) for s, m in zip(ss, ms)]
        rs = [_dot(p, v_scr[keys, 2 * h * hd:(2 * h + 2) * hd]) for h, p in zip(heads, ps)]
        lses = []
        for h, res, m in zip(heads, rs, ms):
            l = res[:, hd:]
            o_ref[rows, h * hd:(h + 1) * hd] = (res[:, :hd] / l).astype(o_ref.dtype)
            lses.append(m + jnp.log(l))
        lse_ref[rows, :] = _pack_heads(lses)


def _att_prompt(qkv, group):
    B, dil, L, _ = qkv.shape
    nsub = math.gcd(ATT_SUB, L // ATT_BLK)
    rows = nsub * ATT_BLK
    cur = lambda which: pl.BlockSpec((None, None, rows, ATT_GW), lambda b, r, j: (b, r, j, which))
    prev = lambda which: pl.BlockSpec(
        (None, None, ATT_BLK, ATT_GW), lambda b, r, j: (b, r, jnp.maximum(j * nsub - 1, 0), which))
    out_spec = lambda w: pl.BlockSpec((None, None, rows, w), lambda b, r, j: (b, r, j, 0))
    return pl.pallas_call(
        functools.partial(_att_prompt_kernel, group=group, dil=dil, nsub=nsub),
        grid=(B, dil, L // rows),
        in_specs=[cur(0), prev(1), cur(1), prev(2), cur(2)],
        out_specs=[out_spec(ATT_GW), out_spec(LANES)],
        out_shape=[
            jax.ShapeDtypeStruct((B, dil, L, ATT_GW), BF16),
            jax.ShapeDtypeStruct((B, dil, L, LANES), F32),
        ],
        scratch_shapes=[
            pltpu.VMEM((rows + ATT_BLK, ATT_GW), BF16),
            pltpu.VMEM((rows + ATT_BLK, 2 * ATT_GW), BF16),
            pltpu.VMEM((ATT_HG, ATT_BLK, 2 * ATT_BLK), F32),
        ],
        compiler_params=_params(("arbitrary", "arbitrary", "arbitrary"), 32),
        name=f"att_prompt_g{group}",
    )(qkv, qkv, qkv, qkv, qkv)


def _att_sample_kernel(q_ref, k_ref, v_ref, c_ref, o_ref, lse_ref, nc_ref, *, group, dil, wb, T, nres):
    slots = 2 * ATT_HG
    nk = c_ref.shape[1] * nres if nres else wb
    t = lax.broadcasted_iota(jnp.int32, (T, nk), 0)
    c = lax.broadcasted_iota(jnp.int32, (T, nk), 1)
    pos = (c // nres) * dil + c % nres if nres else c
    dist_c = wb + t - pos
    valid_c = ((dist_c & (dil - 1)) == 0) & (dist_c <= ATT_SPAN * dil)
    tn = lax.broadcasted_iota(jnp.int32, (T, ATT_BLK), 0)
    cn = lax.broadcasted_iota(jnp.int32, (T, ATT_BLK), 1)
    dist_n = tn - cn
    valid_n = (dist_n >= 0) & ((dist_n & (dil - 1)) == 0) & (cn < T)
    scale = ATT_HD ** -0.5
    heads = range(ATT_HG)
    sls = [slice(h * ATT_HD, (h + 1) * ATT_HD) for h in heads]
    bias_c = jnp.where(valid_c, dist_c.astype(F32), -NEG)
    bias_n = jnp.where(valid_n, dist_n.astype(F32), -NEG)
    for j in range(c_ref.shape[0]):
        rows = slice(j * T, (j + 1) * T)
        for h, sl in zip(heads, sls):
            nc_ref[j, pl.ds(h, T, stride=slots), :] = k_ref[rows, sl]
            nc_ref[j, pl.ds(ATT_HG + h, T, stride=slots), :] = v_ref[rows, sl]
        qs = [q_ref[rows, sl].astype(BF16) for sl in sls]
        if nres:
            slot_rows = lambda s: c_ref[j, :, pl.ds(s, nres, stride=slots), :].reshape(nk, ATT_HD).astype(BF16)
        else:
            slot_rows = lambda s: c_ref[j, pl.ds(s, wb, stride=slots), :].astype(BF16)
        kcs = [slot_rows(h) for h in heads]
        vcs = [slot_rows(ATT_HG + h) for h in heads]
        kns = [_pad_rows(k_ref[rows, sl], ATT_BLK).astype(BF16) for sl in sls]
        vns = [_pad_rows(v_ref[rows, sl], ATT_BLK).astype(BF16) for sl in sls]
        scs = [_dot_nt(q, kc) * scale - _alibi_slope(group, h) * bias_c for h, q, kc in zip(heads, qs, kcs)]
        sns = [_dot_nt(q, kn) * scale - _alibi_slope(group, h) * bias_n for h, q, kn in zip(heads, qs, kns)]
        ms = [jnp.maximum(sc.max(axis=-1, keepdims=True), sn.max(axis=-1, keepdims=True))
              for sc, sn in zip(scs, sns)]
        pcs = [jnp.exp(sc - m) for sc, m in zip(scs, ms)]
        pns = [jnp.exp(sn - m) for sn, m in zip(sns, ms)]
        ls = [pc.sum(axis=-1, keepdims=True) + pn.sum(axis=-1, keepdims=True) for pc, pn in zip(pcs, pns)]
        accs = [_dot(pc.astype(BF16), vc) + _dot(pn.astype(BF16), vn)
                for pc, pn, vc, vn in zip(pcs, pns, vcs, vns)]
        for sl, acc, l in zip(sls, accs, ls):
            o_ref[rows, sl] = (acc / l).astype(o_ref.dtype)
        lse_ref[rows, :] = _pack_heads([jnp.broadcast_to(m + jnp.log(l), (T, LANES)) for l, m in zip(ls, ms)])


def _cache_rows(cache):
    return cache.reshape(cache.shape[0], cache.shape[1] * 2 * ATT_HG, ATT_HD)


def _att_sample(proj_s, cache, group, T):
    nb, wb = cache.shape[0], cache.shape[1]
    assert wb > T
    _, dil = ATT_GROUPS[group]
    c0 = 3 * group
    slots = 2 * ATT_HG
    per = math.gcd(nb, max(1, SAMPLE_SEQS_PER_STEP * ATT_GROUPS[0][0] // wb))
    col = lambda k: pl.BlockSpec((per * T, ATT_GW), lambda b, k=k: (b, c0 + k))
    out_spec = pl.BlockSpec((per * T, ATT_GW), lambda b: (b, 0))
    rows = _cache_rows(cache)
    if dil > T and wb % dil == 0 and T % 8 == 0:
        nres = T
        rows = rows.reshape(nb, wb // dil, dil * slots, ATT_HD)
        cache_spec = pl.BlockSpec((per, wb // dil, nres * slots, ATT_HD), lambda b: (b, 0, 0, 0))
    else:
        nres = 0
        cache_spec = pl.BlockSpec((per, wb * slots, ATT_HD), lambda b: (b, 0, 0))
    return pl.pallas_call(
        functools.partial(_att_sample_kernel, group=group, dil=dil, wb=wb, T=T, nres=nres),
        grid=(nb // per,),
        in_specs=[col(0), col(1), col(2), cache_spec],
        out_specs=[out_spec, pl.BlockSpec((per * T, LANES), lambda b: (b, 0)),
                   pl.BlockSpec((per, T * slots, ATT_HD), lambda b: (b, 0, 0))],
        out_shape=[
            jax.ShapeDtypeStruct((nb * T, ATT_GW), F32),
            jax.ShapeDtypeStruct((nb * T, LANES), F32),
            jax.ShapeDtypeStruct((nb, T * slots, ATT_HD), F32),
        ],
        compiler_params=_params(("arbitrary",), 48),
        name=f"att_sample_g{group}",
    )(proj_s, proj_s, proj_s, rows)


def _epilogue_kernel(x_ref, gr_ref, ga_ref, yret_ref, o0_ref, o1_ref, o2_ref, l0_ref, l1_ref, l2_ref,
                     wr_ref, wa_ref, wo_ref, gf_ref, x1_ref, h2_ref, *scratch, by_residue):
    def position_order(ref, scr):
        dil = ref.shape[0]
        if dil == 1:
            return ref[0].astype(F32)
        for r in range(dil):
            slab = ref[r].astype(F32)
            for cb in range(scr.shape[0]):
                scr[cb, pl.ds(r, ref.shape[1], stride=dil), :] = slab[:, cb * LANES:(cb + 1) * LANES]
        return jnp.concatenate([scr[cb] for cb in range(scr.shape[0])], axis=-1)

    if by_residue:
        o0, o1, o2 = (position_order(ref, scr) for ref, scr in zip((o0_ref, o1_ref, o2_ref), scratch[0:3]))
        l0, l1, l2 = (position_order(ref, scr) for ref, scr in zip((l0_ref, l1_ref, l2_ref), scratch[3:6]))
    else:
        o0, o1, o2 = o0_ref[...], o1_ref[...], o2_ref[...]
        l0, l1, l2 = l0_ref[...], l1_ref[...], l2_ref[...]
    m = jnp.maximum(jnp.maximum(l0, l1), l2)
    e0, e1, e2 = jnp.exp(l0 - m), jnp.exp(l1 - m), jnp.exp(l2 - m)
    inv = 1.0 / (e0 + e1 + e2)
    o = _unpack_heads(e0 * inv) * o0 + _unpack_heads(e1 * inv) * o1 + _unpack_heads(e2 * inv) * o2
    att = _dot(o.astype(BF16), wa_ref[...])
    ret = _dot(yret_ref[...].astype(BF16), wr_ref[...])
    merged = (jax.nn.sigmoid(gr_ref[...].astype(F32)) * ret + jax.nn.sigmoid(ga_ref[...].astype(F32)) * att)
    x1 = x_ref[...] + _dot(merged.astype(BF16), wo_ref[...])
    x1_ref[...] = x1
    h2_ref[...] = _rmsnorm_rows(x1, gf_ref[...]).astype(BF16)


def _epilogue(x2d, proj2, yret, os_, lses, w_ret_o, w_att_o, w_o, g_ffn, *, tm, seq_len=None):
    M = x2d.shape[0]
    row = lambda w: pl.BlockSpec((tm, w), lambda m: (m, 0))
    const = lambda a: pl.BlockSpec(a.shape, lambda m: (0, 0), pipeline_mode=pl.Buffered(1))
    gf = g_ffn.reshape(1, D_MODEL)
    by_residue = seq_len is not None
    scratch = []
    if by_residue:
        tps = seq_len // tm
        att_specs = [pl.BlockSpec((None, a.shape[1], tm // a.shape[1], a.shape[3]),
                                  lambda m: (m // tps, 0, m % tps, 0)) for a in (*os_, *lses)]
        scratch = [pltpu.VMEM((a.shape[3] // LANES, tm, LANES), F32) for a in (*os_, *lses)]
    else:
        att_specs = [row(a.shape[1]) for a in (*os_, *lses)]
    return pl.pallas_call(
        functools.partial(_epilogue_kernel, by_residue=by_residue),
        grid=(M // tm,),
        in_specs=[
            row(D_MODEL),
            pl.BlockSpec((tm, D_MODEL), lambda m: (m, 0)),
            pl.BlockSpec((tm, D_MODEL), lambda m: (m, 1)),
            row(RET_WIDTH), *att_specs,
            const(w_ret_o), const(w_att_o), const(w_o), const(gf),
        ],
        out_specs=[row(D_MODEL), row(D_MODEL)],
        out_shape=[jax.ShapeDtypeStruct((M, D_MODEL), F32), jax.ShapeDtypeStruct((M, D_MODEL), BF16)],
        scratch_shapes=scratch,
        compiler_params=_params(("arbitrary",), 56),
        name="epilogue",
    )(x2d, proj2, proj2, yret, *os_, *lses, w_ret_o, w_att_o, w_o, gf)


def _gelu_gate(a, half_b):
    return (a * (1.0 + lax.erf(a * math.sqrt(0.5)))) * half_b


def _halve_b_half(x):
    return jnp.concatenate([x[..., :D_FF], 0.5 * x[..., D_FF:]], axis=-1)


def _conv3(u, r1, r2, w_ref, b_ref):
    return b_ref[...] + (r2 * w_ref[0:1, :] + r1 * w_ref[1:2, :] + u * w_ref[2:3, :])


UP_CHUNK = 256


SHIFT_CHUNK_ROWS = 4096


def _shift_plan(cache_rows, T, steps):
    nb, n, _ = cache_rows.shape
    shift = T * 2 * ATT_HG
    body = n - shift
    per_seq = -(-body // SHIFT_CHUNK_ROWS)
    while body % per_seq or (body // per_seq) % 8:
        per_seq += 1
    plan = dict(shift=shift, chunk=body // per_seq, per_seq=per_seq, count=nb * per_seq)
    assert plan["count"] < steps, "the background copy needs one grid step per chunk plus one to drain"
    return plan


def _shift_copy_step(s, src_ref, new_ref, dst_ref, buf, sems, i, *, shift, chunk, per_seq, count):
    n = src_ref.shape[1]
    tail = pltpu.make_async_copy(new_ref, dst_ref.at[:, pl.ds(n - shift, shift)], sems.at[i, 4])

    @pl.when(s == 0)
    def _():
        tail.start()

    @pl.when(s == count)
    def _():
        tail.wait()

    def read(c, slot):
        start = pl.multiple_of(shift + (c % per_seq) * chunk, 8)
        return pltpu.make_async_copy(src_ref.at[c // per_seq, pl.ds(start, chunk)], buf.at[slot], sems.at[i, slot])

    def write(c, slot):
        start = pl.multiple_of((c % per_seq) * chunk, 8)
        return pltpu.make_async_copy(buf.at[slot], dst_ref.at[c // per_seq, pl.ds(start, chunk)],
                                     sems.at[i, 2 + slot])

    slot = s % 2

    @pl.when(s == 0)
    def _():
        read(0, 0).start()

    @pl.when(s < count)
    def _():
        read(s, slot).wait()
        write(s, slot).start()

    @pl.when((s >= 1) & (s <= count))
    def _():
        write(s - 1, 1 - slot).wait()

    @pl.when(s + 1 < count)
    def _():
        read(s + 1, 1 - slot).start()


def _up_gate_kernel(h_ref, w_ref, cwa_ref, cwb_ref, cba_ref, cbb_ref, *rest, tiles_per_seq, shift_plans):
    ng = len(shift_plans)
    caches, new_rows, rest = rest[:ng], rest[ng:2 * ng], rest[2 * ng:]
    g_ref, ta_ref, tb_ref = rest[:3]
    advanced, rest = rest[3:3 + ng], rest[3 + ng:]
    u_scr, carry_scr = rest[:2]
    bufs, (sems,) = rest[2:2 + ng], rest[2 + ng:]
    m = pl.program_id(0)
    f = pl.program_id(1)
    tm = h_ref.shape[0]

    step = m * pl.num_programs(1) + f
    for i, plan in enumerate(shift_plans):
        _shift_copy_step(step, caches[i], new_rows[i], advanced[i], bufs[i], sems, i, **plan)

    @pl.when(m % tiles_per_seq == 0)
    def _():
        carry_scr[f] = jnp.zeros(carry_scr.shape[1:], F32)

    u_scr[0:8, :] = carry_scr[f]
    u_scr[8:, :] = _dot(h_ref[...], w_ref[...])
    carry_scr[f] = u_scr[tm:, :]
    for c in range(g_ref.shape[1] // UP_CHUNK):
        cols = slice(c * UP_CHUNK, (c + 1) * UP_CHUNK)

        def conv(half, cw_ref, cb_ref, tail_ref):
            pcols = slice((2 * c + half) * UP_CHUNK, (2 * c + half + 1) * UP_CHUNK)
            tail_ref[:, cols] = u_scr[tm:, pcols]
            u, r1, r2 = u_scr[8:, pcols], u_scr[7:tm + 7, pcols], u_scr[6:tm + 6, pcols]
            return cb_ref[:, cols] + (r2 * cw_ref[0:1, cols] + r1 * cw_ref[1:2, cols] + u * cw_ref[2:3, cols])

        a = conv(0, cwa_ref, cba_ref, ta_ref)
        b = conv(1, cwb_ref, cbb_ref, tb_ref)
        g_ref[:, cols] = _gelu_gate(a, b).astype(BF16)


def _up_gate(h2, w_up_ab, conv_w, conv_b, caches, new_rows, T, *, tm, tiles_per_seq):
    M = h2.shape[0]
    tf = 512
    nf = D_FF // tf
    cb = conv_b.reshape(1, 2 * D_FF)
    grid = (M // tm, nf)
    plans = tuple(_shift_plan(c, T, grid[0] * grid[1]) for c in caches)
    any_spec = pl.BlockSpec(memory_space=pl.ANY)
    return pl.pallas_call(
        functools.partial(_up_gate_kernel, tiles_per_seq=tiles_per_seq, shift_plans=plans),
        grid=grid,
        in_specs=[
            pl.BlockSpec((tm, D_MODEL), lambda m, f: (m, 0)),
            pl.BlockSpec((D_MODEL, 2 * tf), lambda m, f: (0, f)),
            pl.BlockSpec((CONV_W, tf), lambda m, f: (0, f)),
            pl.BlockSpec((CONV_W, tf), lambda m, f: (0, f + nf)),
            pl.BlockSpec((1, tf), lambda m, f: (0, f)),
            pl.BlockSpec((1, tf), lambda m, f: (0, f + nf)),
            *[any_spec for _ in (*caches, *new_rows)],
        ],
        out_specs=[
            pl.BlockSpec((tm, tf), lambda m, f: (m, f)),
            pl.BlockSpec((None, 8, tf), lambda m, f: (m, 0, f)),
            pl.BlockSpec((None, 8, tf), lambda m, f: (m, 0, f)),
            *[any_spec for _ in caches],
        ],
        out_shape=[
            jax.ShapeDtypeStruct((M, D_FF), BF16),
            jax.ShapeDtypeStruct((M // tm, 8, D_FF), F32),
            jax.ShapeDtypeStruct((M // tm, 8, D_FF), F32),
            *[jax.ShapeDtypeStruct(c.shape, F32) for c in caches],
        ],
        scratch_shapes=[
            pltpu.VMEM((tm + 8, 2 * tf), F32),
            pltpu.VMEM((nf, 8, 2 * tf), F32),
            *[pltpu.VMEM((2, p["chunk"], ATT_HD), F32) for p in plans],
            pltpu.SemaphoreType.DMA((len(caches), 5)),
        ],
        compiler_params=_params(("arbitrary", "arbitrary"), 56),
        name="up_gate",
    )(h2, w_up_ab, conv_w, conv_w, cb, cb, *caches, *new_rows)


def _down_kernel(g_ref, wd_ref, x1_ref, gfin_ref, y_ref, *, final_norm):
    y = x1_ref[...] + _dot(g_ref[...], wd_ref[...])
    y_ref[...] = _rmsnorm_rows(y, gfin_ref[...]) if final_norm else y


def _down(g, x1, w_down, g_final, *, tm, final_norm):
    M = g.shape[0]
    return pl.pallas_call(
        functools.partial(_down_kernel, final_norm=final_norm),
        grid=(M // tm,),
        in_specs=[
            pl.BlockSpec((tm, D_FF), lambda m: (m, 0)),
            pl.BlockSpec((D_FF, D_MODEL), lambda m: (0, 0), pipeline_mode=pl.Buffered(1)),
            pl.BlockSpec((tm, D_MODEL), lambda m: (m, 0)),
            pl.BlockSpec((1, D_MODEL), lambda m: (0, 0)),
        ],
        out_specs=pl.BlockSpec((tm, D_MODEL), lambda m: (m, 0)),
        out_shape=jax.ShapeDtypeStruct((M, D_MODEL), F32),
        compiler_params=_params(("arbitrary",), 48),
        name="down",
    )(g, w_down, x1, g_final.reshape(1, D_MODEL))


def _ffn_sample_kernel(h_ref, wua_ref, wub_ref, s0a_ref, s0b_ref, s1a_ref, s1b_ref, cwa_ref, cwb_ref,
                       cba_ref, cbb_ref, wd_ref, x1_ref, gfin_ref, y_ref, ua_ref, ub_ref, wu_cast_ref, wd_cast_ref,
                       *, T, final_norm):
    f = pl.program_id(0)
    M, tf = ua_ref.shape
    t = lax.broadcasted_iota(jnp.int32, (M, tf), 0) & (T - 1)
    wua, wub, wd = wua_ref[...].astype(BF16), wub_ref[...].astype(BF16), wd_ref[...].astype(BF16)
    for c in range(tf // UP_CHUNK):
        cols = slice(c * UP_CHUNK, (c + 1) * UP_CHUNK)
        wu_cast_ref[:, 2 * c * UP_CHUNK:(2 * c + 1) * UP_CHUNK] = wua[:, cols]
        wu_cast_ref[:, (2 * c + 1) * UP_CHUNK:(2 * c + 2) * UP_CHUNK] = wub[:, cols]
    wd_cast_ref[...] = wd
    h = h_ref[...]
    per_row = lambda s_ref: jnp.broadcast_to(s_ref[...][:, None, :], (M // T, T, tf)).reshape(M, tf)

    def conv(u, s0_ref, s1_ref, w_ref, b_ref, u_out_ref):
        u_out_ref[...] = u
        s0, s1 = per_row(s0_ref), per_row(s1_ref)
        r1 = jnp.where(t >= 1, pltpu.roll(u, 1, axis=0), s1)
        r2 = jnp.where(t >= 2, pltpu.roll(u, 2, axis=0), jnp.where(t == 0, s0, s1))
        return _conv3(u, r1, r2, w_ref, b_ref)

    a = conv(_dot(h, wua), s0a_ref, s1a_ref, cwa_ref, cba_ref, ua_ref)
    b = conv(_dot(h, wub), s0b_ref, s1b_ref, cwb_ref, cbb_ref, ub_ref)
    part = _dot(_gelu_gate(a, b).astype(BF16), wd)

    @pl.when(f == 0)
    def _():
        y_ref[...] = x1_ref[...] + part

    @pl.when(f > 0)
    def _():
        y_ref[...] += part

    if final_norm:
        @pl.when(f == pl.num_programs(0) - 1)
        def _():
            y_ref[...] = _rmsnorm_rows(y_ref[...], gfin_ref[...])


def _ffn_sample(h2, x1, state_conv, w_up, conv_w, conv_b, w_down, g_final, T, final_norm):
    M = h2.shape[0]
    nb = M // T
    tf = 512
    nf = D_FF // tf
    F2 = 2 * D_FF
    assert state_conv.shape[1] == CONV_W - 1 == 2
    s0, s1 = state_conv[:, 0].astype(F32), state_conv[:, 1].astype(F32)
    cb = conv_b.reshape(1, F2)
    gfin = g_final.reshape(1, D_MODEL)
    full = lambda w: pl.BlockSpec((M, w), lambda f: (0, 0))
    ca = lambda r, w=tf: pl.BlockSpec((r, w), lambda f: (0, f))
    cbk = lambda r, w=tf: pl.BlockSpec((r, w), lambda f: (0, f + nf))
    down_rows = pl.BlockSpec((tf, D_MODEL), lambda f: (f, 0))
    y, ua, ub, w_up_ab, w_down_bf16 = pl.pallas_call(
        functools.partial(_ffn_sample_kernel, T=T, final_norm=final_norm),
        grid=(nf,),
        in_specs=[
            full(D_MODEL), ca(D_MODEL), cbk(D_MODEL), ca(nb), cbk(nb), ca(nb), cbk(nb),
            ca(CONV_W), cbk(CONV_W), ca(1), cbk(1),
            down_rows,
            full(D_MODEL), pl.BlockSpec((1, D_MODEL), lambda f: (0, 0)),
        ],
        out_specs=[full(D_MODEL), ca(M), ca(M), ca(D_MODEL, 2 * tf), down_rows],
        out_shape=[
            jax.ShapeDtypeStruct((M, D_MODEL), F32),
            jax.ShapeDtypeStruct((M, D_FF), F32),
            jax.ShapeDtypeStruct((M, D_FF), F32),
            jax.ShapeDtypeStruct((D_MODEL, F2), BF16),
            jax.ShapeDtypeStruct((D_FF, D_MODEL), BF16),
        ],
        compiler_params=_params(("arbitrary",), 56),
        name="ffn_sample",
    )(h2, w_up, w_up, s0, s0, s1, s1, conv_w, conv_w, cb, cb, w_down, x1, gfin)
    u = jnp.concatenate([ua, ub], axis=-1).reshape(nb, T, F2)
    return y, u[:, T - (CONV_W - 1):], w_up_ab, w_down_bf16


def kernel(x_prompt, x_sample, state_ret, cache_kv_w128, cache_kv_w512, cache_kv_w2048, state_conv, g_mix, w_in,
           gn_ret, w_ret_o, w_att_o, w_o, g_ffn, w_up, conv_w, conv_b, w_down, g_final):
    B, S, _ = x_prompt.shape
    NB, T, _ = x_sample.shape
    depth = w_in.shape[0]
    caches = (cache_kv_w128, cache_kv_w512, cache_kv_w2048)
    TM = 1024
    tiles_per_seq = S // TM

    xp = x_prompt.reshape(B * S, D_MODEL)
    xs = x_sample.reshape(NB * T, D_MODEL)
    ret_p, ret_s, conv_p, conv_s = [], [], [], []
    kv_p = [[] for _ in range(N_GROUPS)]
    kv_s = [[] for _ in range(N_GROUPS)]
    for l in range(depth):
        w_ret_o_l = w_ret_o[l].astype(BF16)
        w_att_o_l = w_att_o[l].astype(BF16)
        w_o_l = w_o[l].astype(BF16)
        conv_w_l, conv_b_l = _halve_b_half(conv_w[l]), _halve_b_half(conv_b[l])

        ret_tiles = 4 * RET_WIDTH // CAST_TN
        att_tiles = ATT_COLS // CAST_TN
        in_tiles = IN_COLS // CAST_TN
        proj_s, w_main_l = _inproj_sample(xs, g_mix[l], w_in[l], lambda n: (n + ret_tiles + att_tiles) % in_tiles,
                                          MAIN_COLS // CAST_TN, "inproj_sample_main")
        att_s, w_att_l = _inproj_sample(xs, g_mix[l], w_in[l], lambda n: n + ret_tiles, att_tiles,
                                        "inproj_sample_att")
        yret_s, st_s = _ret_sample(proj_s, state_ret[l], gn_ret[l], T)
        ret_s.append(st_s)
        os_, lses, new_rows = [], [], []
        for g in range(N_GROUPS):
            o, lse, rows = _att_sample(att_s, caches[g][l], g, T)
            os_.append(o)
            lses.append(lse)
            new_rows.append(rows)
        x1s, h2s = _epilogue(xs, proj_s, yret_s, os_, lses, w_ret_o_l, w_att_o_l, w_o_l, g_ffn[l], tm=NB * T)
        xs, cv, w_up_l, w_down_l = _ffn_sample(h2s, x1s, state_conv[l], w_up[l], conv_w_l, conv_b_l, w_down[l],
                                               g_final, T, final_norm=l == depth - 1)
        conv_s.append(cv)

        proj, h = _inproj_main(xp, g_mix[l], w_main_l, tm=TM)
        yret, st = _ret_prompt(proj.reshape(B, S, MAIN_COLS), gn_ret[l])
        ret_p.append(st)
        os_, lses = [], []
        for g, (win, _) in enumerate(ATT_GROUPS):
            qkv, kvf = _inproj_att(h, w_att_l, g, B=B, S=S, tm=512)
            o, lse = _att_prompt(qkv, g)
            os_.append(o)
            lses.append(lse)
            kv_p[g].append(kvf.reshape(B, min(win, S), 2, ATT_HG, ATT_HD))
        x1, h2 = _epilogue(xp, proj, yret.reshape(B * S, RET_WIDTH), os_, lses,
                           w_ret_o_l, w_att_o_l, w_o_l, g_ffn[l], tm=256, seq_len=S)
        gated, tail_a, tail_b, *advanced = _up_gate(h2, w_up_l, conv_w_l, conv_b_l,
                                                    [_cache_rows(c[l]) for c in caches], new_rows, T,
                                                    tm=TM, tiles_per_seq=tiles_per_seq)
        for g in range(N_GROUPS):
            kv_s[g].append(advanced[g].reshape(caches[g][l].shape))
        utail = jnp.concatenate([tail_a, tail_b], axis=-1)
        conv_p.append(utail[tiles_per_seq - 1::tiles_per_seq, 8 - (CONV_W - 1):])
        xp = _down(gated, x1, w_down_l, g_final, tm=256, final_norm=l == depth - 1)

    return (xp.reshape(B, S, D_MODEL), xs.reshape(NB, T, D_MODEL),
            jnp.stack(ret_p), jnp.stack(ret_s),
            jnp.stack(kv_p[0]), jnp.stack(kv_s[0]),
            jnp.stack(kv_p[1]), jnp.stack(kv_s[1]),
            jnp.stack(kv_p[2]), jnp.stack(kv_s[2]),
            jnp.stack(conv_p), jnp.stack(conv_s))
```

```python
import functools
import math

import jax
import jax.numpy as jnp
from jax import lax
from jax.experimental import pallas as pl
from jax.experimental.pallas import tpu as pltpu

D_MODEL = 2048
RET_HEADS = 8
RET_DK = 128
RET_DV = 128
RET_WIDTH = RET_HEADS * RET_DV
RET_CHUNK = 128
ATT_GROUPS = ((128, 1), (512, 4), (2048, 16))
N_GROUPS = 3
ATT_HG = 4
ATT_HD = 128
ATT_GW = ATT_HG * ATT_HD
ATT_SPAN = 128
ATT_BLK = 128
ALIBI_MAX = 8.0
D_FF = 5632
CONV_W = 3
EPS = 1e-6
NEG = -1e30
IN_COLS = 4 * RET_WIDTH + 3 * N_GROUPS * ATT_GW + 2 * D_MODEL

GATE_COLS = 2 * D_MODEL
RET_COL0 = GATE_COLS
MAIN_COLS = GATE_COLS + 4 * RET_WIDTH
ATT_COLS = 3 * N_GROUPS * ATT_GW
CAST_TN = 512

LANES = 128
MIB = 1024 * 1024
BF16 = jnp.bfloat16
F32 = jnp.float32


def _params(semantics, vmem_mib, flags=None):
    return pltpu.CompilerParams(dimension_semantics=semantics, vmem_limit_bytes=vmem_mib * MIB, flags=flags)


def _dot(a, b):
    return jnp.dot(a, b, preferred_element_type=F32)


def _dot_nt(a, b):
    return lax.dot_general(a, b, (((1,), (1,)), ((), ())), preferred_element_type=F32)


def _rmsnorm_rows(x, g):
    ms = jnp.mean(x * x, axis=-1, keepdims=True)
    return x * lax.rsqrt(ms + EPS) * g


def _inproj_sample_kernel(x_ref, g_ref, w_ref, proj_ref, wcast_ref, h_scr):
    @pl.when(pl.program_id(0) == 0)
    def _():
        h_scr[...] = _rmsnorm_rows(x_ref[...], g_ref[...]).astype(BF16)

    w = w_ref[...].astype(BF16)
    wcast_ref[...] = w
    proj_ref[...] = _dot(h_scr[...], w)


def _inproj_sample(x2d, g, w, src_tile, n_tiles, name):
    M = x2d.shape[0]
    tn = CAST_TN
    return pl.pallas_call(
        _inproj_sample_kernel,
        grid=(n_tiles,),
        in_specs=[
            pl.BlockSpec((M, D_MODEL), lambda n: (0, 0)),
            pl.BlockSpec((1, D_MODEL), lambda n: (0, 0)),
            pl.BlockSpec((D_MODEL, tn), lambda n: (0, src_tile(n))),
        ],
        out_specs=[pl.BlockSpec((M, tn), lambda n: (0, n)), pl.BlockSpec((D_MODEL, tn), lambda n: (0, n))],
        out_shape=[jax.ShapeDtypeStruct((M, n_tiles * tn), F32), jax.ShapeDtypeStruct((D_MODEL, n_tiles * tn), BF16)],
        scratch_shapes=[pltpu.VMEM((M, D_MODEL), BF16)],
        compiler_params=_params(("arbitrary",), 32),
        name=name,
    )(x2d, g.reshape(1, D_MODEL), w)


def _inproj_main_kernel(x_ref, g_ref, w_ref, main_ref, h_ref):
    @pl.when(pl.program_id(1) == 0)
    def _():
        h_ref[...] = _rmsnorm_rows(x_ref[...], g_ref[...]).astype(BF16)

    main_ref[...] = _dot(h_ref[...], w_ref[...]).astype(BF16)


def _inproj_main(x2d, g, w_main, *, tm):
    M = x2d.shape[0]
    N = w_main.shape[1]
    tn = 1024
    return pl.pallas_call(
        _inproj_main_kernel,
        grid=(M // tm, N // tn),
        in_specs=[
            pl.BlockSpec((tm, D_MODEL), lambda m, n: (m, 0)),
            pl.BlockSpec((1, D_MODEL), lambda m, n: (0, 0)),
            pl.BlockSpec((D_MODEL, tn), lambda m, n: (0, n)),
        ],
        out_specs=[pl.BlockSpec((tm, tn), lambda m, n: (m, n)), pl.BlockSpec((tm, D_MODEL), lambda m, n: (m, 0))],
        out_shape=[jax.ShapeDtypeStruct((M, N), BF16), jax.ShapeDtypeStruct((M, D_MODEL), BF16)],
        compiler_params=_params(("arbitrary", "arbitrary"), 48),
        name="inproj_main",
    )(x2d, g.reshape(1, D_MODEL), w_main)


def _inproj_att_kernel(h_ref, w_ref, a_ref, kv_ref, *scratch, dil):
    acc = _dot(h_ref[...], w_ref[...])
    slots = 2 * ATT_HG
    rows = kv_ref.shape[0] // slots
    for slot in range(slots):
        cols = slice(ATT_GW + slot * ATT_HD, ATT_GW + (slot + 1) * ATT_HD)
        kv_ref[pl.ds(slot, rows, stride=slots), :] = acc[acc.shape[0] - rows:, cols]
    if dil == 1:
        a_ref[0] = acc.astype(BF16)
    else:
        acc_scr, = scratch
        rows = acc_scr.shape[1] // dil
        for cb in range(acc_scr.shape[0]):
            lanes = slice(cb * LANES, (cb + 1) * LANES)
            acc_scr[cb] = acc[:, lanes]
            for r in range(dil):
                a_ref[r, :, lanes] = acc_scr[cb, pl.ds(r, rows, stride=dil), :].astype(BF16)


def _inproj_att(h, w_att, group, *, B, S, tm):
    M = h.shape[0]
    tn = 3 * ATT_GW
    tps = S // tm
    win, dil = ATT_GROUPS[group]
    win = min(win, S)
    rows = min(win, tm)
    first = tps - win // rows
    slots = 2 * ATT_HG
    return pl.pallas_call(
        functools.partial(_inproj_att_kernel, dil=dil),
        grid=(M // tm,),
        in_specs=[
            pl.BlockSpec((tm, D_MODEL), lambda m: (m, 0)),
            pl.BlockSpec((D_MODEL, tn), lambda m: (0, group)),
        ],
        out_specs=[
            pl.BlockSpec((None, dil, tm // dil, tn), lambda m: (m // tps, 0, m % tps, 0)),
            pl.BlockSpec((None, rows * slots, ATT_HD), lambda m: (m // tps, jnp.maximum(m % tps - first, 0), 0)),
        ],
        out_shape=[
            jax.ShapeDtypeStruct((B, dil, S // dil, tn), BF16),
            jax.ShapeDtypeStruct((B, win * slots, ATT_HD), F32),
        ],
        scratch_shapes=[] if dil == 1 else [pltpu.VMEM((tn // LANES, tm, LANES), F32)],
        compiler_params=_params(("arbitrary",), 48),
        name=f"inproj_att_g{group}",
    )(h, w_att)


def _ret_tables(lq):
    lg = jnp.log(1.0 - jnp.exp2(-5.0 - jnp.arange(RET_HEADS, dtype=F32)))
    scale = RET_DK ** -0.5
    i = jnp.arange(lq, dtype=F32)
    j = jnp.arange(RET_CHUNK, dtype=F32)
    diff = i[:, None] - j[None, :]
    live = (diff >= 0) & (j[None, :] < lq)
    dm = jnp.where(live[None], jnp.exp(jnp.maximum(diff, 0.0)[None] * lg[:, None, None]), 0.0) * scale
    qd = jnp.exp((i[None, :] + 1.0) * lg[:, None])
    qd = jnp.broadcast_to(qd[:, :, None], (RET_HEADS, lq, RET_DV))
    kd = jnp.where(j[None, :] < lq, jnp.exp(jnp.maximum(lq - 1.0 - j, 0.0)[None, :] * lg[:, None]), 0.0) * scale
    kd = jnp.broadcast_to(kd[:, :, None], (RET_HEADS, RET_CHUNK, RET_DK))
    gl = jnp.exp(lq * lg)
    return dm.astype(F32), qd.astype(F32), kd.astype(F32), gl.astype(F32)


def _ret_heads(qs, ks, vs, sts, dms, qds, kds, gls):
    ss = [(_dot_nt(q, k) * dm).astype(BF16) for q, k, dm in zip(qs, ks, dms)]
    crosses = [_dot(q, st.astype(BF16)) for q, st in zip(qs, sts)]
    kts = [(k.astype(F32) * kd).T.astype(BF16) for k, kd in zip(ks, kds)]
    inners = [_dot(s, v) for s, v in zip(ss, vs)]
    updates = [_dot(kt, v) for kt, v in zip(kts, vs)]
    ys = [inner + cross * qd for inner, cross, qd in zip(inners, crosses, qds)]
    new_sts = [st * gl + upd for st, gl, upd in zip(sts, gls, updates)]
    return ys, new_sts


def _groupnorm_gates(ys, gns, rgs):
    mus = [jnp.mean(y, axis=-1, keepdims=True) for y in ys]
    ycs = [y - mu for y, mu in zip(ys, mus)]
    rstds = [lax.rsqrt(jnp.mean(yc * yc, axis=-1, keepdims=True) + EPS) for yc in ycs]
    return [(rg * jax.nn.sigmoid(rg)) * ((yc * rstd) * gn) for yc, rstd, gn, rg in zip(ycs, rstds, gns, rgs)]


def _head_slices():
    return [slice(h * RET_DK, (h + 1) * RET_DK) for h in range(RET_HEADS)]


def _ret_prompt_kernel(gl_ref, q_ref, k_ref, v_ref, rg_ref, dm_ref, qd_ref, kd_ref, gn_ref, y_ref, st_ref):
    c = pl.program_id(1)

    @pl.when(c == 0)
    def _():
        st_ref[...] = jnp.zeros_like(st_ref)

    sls = _head_slices()
    heads = range(RET_HEADS)
    sts = [st_ref[h] for h in heads]
    for i in range(q_ref.shape[0] // RET_CHUNK):
        rows = slice(i * RET_CHUNK, (i + 1) * RET_CHUNK)
        ys, sts = _ret_heads([q_ref[rows, sl] for sl in sls], [k_ref[rows, sl] for sl in sls],
                             [v_ref[rows, sl] for sl in sls], sts, [dm_ref[h] for h in heads],
                             [qd_ref[h] for h in heads], [kd_ref[h] for h in heads], [gl_ref[h] for h in heads])
        outs = _groupnorm_gates(ys, [gn_ref[:, sl] for sl in sls], [rg_ref[rows, sl].astype(F32) for sl in sls])
        for sl, out in zip(sls, outs):
            y_ref[rows, sl] = out.astype(y_ref.dtype)
    for h in heads:
        st_ref[h] = sts[h]


RET_STEP_CHUNKS = 2


def _ret_prompt(proj3, gn_ret):
    B, S, _ = proj3.shape
    L = RET_CHUNK
    dm, qd, kd, gl = _ret_tables(L)
    cb = RET_COL0 // RET_WIDTH
    rows = L * math.gcd(RET_STEP_CHUNKS, S // L)
    col = lambda k: pl.BlockSpec((None, rows, RET_WIDTH), lambda b, c, k=k: (b, c, cb + k))
    tab = pl.BlockSpec((RET_HEADS, L, RET_DK), lambda b, c: (0, 0, 0))
    return pl.pallas_call(
        _ret_prompt_kernel,
        grid=(B, S // rows),
        in_specs=[
            pl.BlockSpec(memory_space=pltpu.SMEM),
            col(0), col(1), col(2), col(3), tab, tab, tab,
            pl.BlockSpec((1, RET_WIDTH), lambda b, c: (0, 0)),
        ],
        out_specs=[
            pl.BlockSpec((None, rows, RET_WIDTH), lambda b, c: (b, c, 0)),
            pl.BlockSpec((None, RET_HEADS, RET_DK, RET_DV), lambda b, c: (b, 0, 0, 0)),
        ],
        out_shape=[
            jax.ShapeDtypeStruct((B, S, RET_WIDTH), BF16),
            jax.ShapeDtypeStruct((B, RET_HEADS, RET_DK, RET_DV), F32),
        ],
        compiler_params=_params(("arbitrary", "arbitrary"), 32),
        name="ret_prompt",
    )(gl, proj3, proj3, proj3, proj3, dm, qd, kd, gn_ret.reshape(1, RET_WIDTH))


def _pad_rows(x, rows):
    x = x.astype(F32)
    return jnp.concatenate([x, jnp.zeros((rows - x.shape[0], x.shape[1]), F32)], axis=0)


def _ret_sample_kernel(gl_ref, q_ref, k_ref, v_ref, rg_ref, st_in_ref, dm_ref, qd_ref, kd_ref, gn_ref,
                       y_ref, st_ref, *, T):
    sls = _head_slices()
    heads = range(RET_HEADS)
    for j in range(st_ref.shape[0]):
        rows = slice(j * T, (j + 1) * T)
        ys, new_sts = _ret_heads([q_ref[rows, sl].astype(BF16) for sl in sls],
                                 [_pad_rows(k_ref[rows, sl], RET_CHUNK).astype(BF16) for sl in sls],
                                 [_pad_rows(v_ref[rows, sl], RET_CHUNK).astype(BF16) for sl in sls],
                                 [st_in_ref[j, h] for h in heads], [dm_ref[h] for h in heads],
                                 [qd_ref[h] for h in heads], [kd_ref[h] for h in heads], [gl_ref[h] for h in heads])
        for h in heads:
            st_ref[j, h] = new_sts[h]
        outs = _groupnorm_gates(ys, [gn_ref[:, sl] for sl in sls], [rg_ref[rows, sl].astype(F32) for sl in sls])
        for sl, out in zip(sls, outs):
            y_ref[rows, sl] = out.astype(y_ref.dtype)


SAMPLE_SEQS_PER_STEP = 4


def _ret_sample(proj_s, state, gn_ret, T):
    nb = state.shape[0]
    per = math.gcd(SAMPLE_SEQS_PER_STEP, nb)
    dm, qd, kd, gl = _ret_tables(T)
    cb = RET_COL0 // RET_WIDTH
    col = lambda k: pl.BlockSpec((per * T, RET_WIDTH), lambda b, k=k: (b, cb + k))
    st_spec = pl.BlockSpec((per, RET_HEADS, RET_DK, RET_DV), lambda b: (b, 0, 0, 0))
    return pl.pallas_call(
        functools.partial(_ret_sample_kernel, T=T),
        grid=(nb // per,),
        in_specs=[
            pl.BlockSpec(memory_space=pltpu.SMEM),
            col(0), col(1), col(2), col(3), st_spec,
            pl.BlockSpec((RET_HEADS, T, RET_CHUNK), lambda b: (0, 0, 0)),
            pl.BlockSpec((RET_HEADS, T, RET_DV), lambda b: (0, 0, 0)),
            pl.BlockSpec((RET_HEADS, RET_CHUNK, RET_DK), lambda b: (0, 0, 0)),
            pl.BlockSpec((1, RET_WIDTH), lambda b: (0, 0)),
        ],
        out_specs=[pl.BlockSpec((per * T, RET_WIDTH), lambda b: (b, 0)), st_spec],
        out_shape=[
            jax.ShapeDtypeStruct((nb * T, RET_WIDTH), F32),
            jax.ShapeDtypeStruct(state.shape, F32),
        ],
        compiler_params=_params(("arbitrary",), 32),
        name="ret_sample",
    )(gl, proj_s, proj_s, proj_s, proj_s, state, dm, qd, kd, gn_ret.reshape(1, RET_WIDTH))


def _alibi_slope(g, h):
    n = N_GROUPS * ATT_HG
    return 2.0 ** (-ALIBI_MAX * (g * ATT_HG + h + 1) / n)


ATT_SUB = 8


def _pack_heads(xs):
    lane = lax.broadcasted_iota(jnp.int32, xs[0].shape, 1)
    out = xs[-1]
    for h in range(len(xs) - 2, -1, -1):
        out = jnp.where(lane == h, xs[h], out)
    return out


def _unpack_heads(x):
    return jnp.concatenate([jnp.broadcast_to(x[:, h:h + 1], (x.shape[0], ATT_HD)) for h in range(ATT_HG)], axis=-1)


def _att_prompt_kernel(q_ref, kp_ref, kc_ref, vp_ref, vc_ref, o_ref, lse_ref, k_scr, v_scr, bias_scr,
                       *, group, dil, nsub):
    j = pl.program_id(2)
    blk = ATT_BLK
    hd = ATT_HD
    k_scr[0:blk] = kp_ref[...]
    k_scr[blk:] = kc_ref[...]
    ones = jnp.ones((blk * (nsub + 1), hd), BF16)
    for h in range(ATT_HG):
        v_scr[0:blk, 2 * h * hd:(2 * h + 1) * hd] = vp_ref[:, h * hd:(h + 1) * hd]
        v_scr[blk:, 2 * h * hd:(2 * h + 1) * hd] = vc_ref[:, h * hd:(h + 1) * hd]
        v_scr[:, (2 * h + 1) * hd:(2 * h + 2) * hd] = ones

    r = lax.broadcasted_iota(jnp.int32, (blk, 2 * blk), 0)
    c = lax.broadcasted_iota(jnp.int32, (blk, 2 * blk), 1)
    dist = blk + r - c
    in_band = (dist >= 0) & (dist <= ATT_SPAN)
    for h in range(ATT_HG):
        bias_scr[h] = jnp.where(in_band, (-_alibi_slope(group, h) * dil) * dist.astype(F32), NEG)
    no_prev = jnp.where((c < blk) & (j == 0), NEG, 0.0)

    scale = hd ** -0.5
    for i in range(nsub):
        rows = slice(i * blk, (i + 1) * blk)
        keys = slice(i * blk, (i + 2) * blk)
        heads = range(ATT_HG)
        ss = [_dot_nt(q_ref[rows, h * hd:(h + 1) * hd], k_scr[keys, h * hd:(h + 1) * hd]) * scale + bias_scr[h]
              for h in heads]
        if i == 0:
            ss = [s + no_prev for s in ss]
        ms = [s.max(axis=-1, keepdims=True) for s in ss]
        ps = [jnp.exp(s - m).astype(BF16) for s, m in zip(ss, ms)]
        rs = [_dot(p, v_scr[keys, 2 * h * hd:(2 * h + 2) * hd]) for h, p in zip(heads, ps)]
        lses = []
        for h, res, m in zip(heads, rs, ms):
            l = res[:, hd:]
            o_ref[rows, h * hd:(h + 1) * hd] = (res[:, :hd] / l).astype(o_ref.dtype)
            lses.append(m + jnp.log(l))
        lse_ref[rows, :] = _pack_heads(lses)


def _att_prompt(qkv, group):
    B, dil, L, _ = qkv.shape
    nsub = math.gcd(ATT_SUB, L // ATT_BLK)
    rows = nsub * ATT_BLK
    cur = lambda which: pl.BlockSpec((None, None, rows, ATT_GW), lambda b, r, j: (b, r, j, which))
    prev = lambda which: pl.BlockSpec(
        (None, None, ATT_BLK, ATT_GW), lambda b, r, j: (b, r, jnp.maximum(j * nsub - 1, 0), which))
    out_spec = lambda w: pl.BlockSpec((None, None, rows, w), lambda b, r, j: (b, r, j, 0))
    return pl.pallas_call(
        functools.partial(_att_prompt_kernel, group=group, dil=dil, nsub=nsub),
        grid=(B, dil, L // rows),
        in_specs=[cur(0), prev(1), cur(1), prev(2), cur(2)],
        out_specs=[out_spec(ATT_GW), out_spec(LANES)],
        out_shape=[
            jax.ShapeDtypeStruct((B, dil, L, ATT_GW), BF16),
            jax.ShapeDtypeStruct((B, dil, L, LANES), F32),
        ],
        scratch_shapes=[
            pltpu.VMEM((rows + ATT_BLK, ATT_GW), BF16),
            pltpu.VMEM((rows + ATT_BLK, 2 * ATT_GW), BF16),
            pltpu.VMEM((ATT_HG, ATT_BLK, 2 * ATT_BLK), F32),
        ],
        compiler_params=_params(("arbitrary", "arbitrary", "arbitrary"), 32),
        name=f"att_prompt_g{group}",
    )(qkv, qkv, qkv, qkv, qkv)


def _att_sample_kernel(q_ref, k_ref, v_ref, c_ref, o_ref, lse_ref, nc_ref, *, group, dil, wb, T, nres):
    slots = 2 * ATT_HG
    nk = c_ref.shape[1] * nres if nres else wb
    t = lax.broadcasted_iota(jnp.int32, (T, nk), 0)
    c = lax.broadcasted_iota(jnp.int32, (T, nk), 1)
    pos = (c // nres) * dil + c % nres if nres else c
    dist_c = wb + t - pos
    valid_c = ((dist_c & (dil - 1)) == 0) & (dist_c <= ATT_SPAN * dil)
    tn = lax.broadcasted_iota(jnp.int32, (T, ATT_BLK), 0)
    cn = lax.broadcasted_iota(jnp.int32, (T, ATT_BLK), 1)
    dist_n = tn - cn
    valid_n = (dist_n >= 0) & ((dist_n & (dil - 1)) == 0) & (cn < T)
    scale = ATT_HD ** -0.5
    heads = range(ATT_HG)
    sls = [slice(h * ATT_HD, (h + 1) * ATT_HD) for h in heads]
    bias_c = jnp.where(valid_c, dist_c.astype(F32), -NEG)
    bias_n = jnp.where(valid_n, dist_n.astype(F32), -NEG)
    for j in range(c_ref.shape[0]):
        rows = slice(j * T, (j + 1) * T)
        for h, sl in zip(heads, sls):
            nc_ref[j, pl.ds(h, T, stride=slots), :] = k_ref[rows, sl]
            nc_ref[j, pl.ds(ATT_HG + h, T, stride=slots), :] = v_ref[rows, sl]
        qs = [q_ref[rows, sl].astype(BF16) for sl in sls]
        if nres:
            slot_rows = lambda s: c_ref[j, :, pl.ds(s, nres, stride=slots), :].reshape(nk, ATT_HD).astype(BF16)
        else:
            slot_rows = lambda s: c_ref[j, pl.ds(s, wb, stride=slots), :].astype(BF16)
        kcs = [slot_rows(h) for h in heads]
        vcs = [slot_rows(ATT_HG + h) for h in heads]
        kns = [_pad_rows(k_ref[rows, sl], ATT_BLK).astype(BF16) for sl in sls]
        vns = [_pad_rows(v_ref[rows, sl], ATT_BLK).astype(BF16) for sl in sls]
        scs = [_dot_nt(q, kc) * scale - _alibi_slope(group, h) * bias_c for h, q, kc in zip(heads, qs, kcs)]
        sns = [_dot_nt(q, kn) * scale - _alibi_slope(group, h) * bias_n for h, q, kn in zip(heads, qs, kns)]
        ms = [jnp.maximum(sc.max(axis=-1, keepdims=True), sn.max(axis=-1, keepdims=True))
              for sc, sn in zip(scs, sns)]
        pcs = [jnp.exp(sc - m) for sc, m in zip(scs, ms)]
        pns = [jnp.exp(sn - m) for sn, m in zip(sns, ms)]
        ls = [pc.sum(axis=-1, keepdims=True) + pn.sum(axis=-1, keepdims=True) for pc, pn in zip(pcs, pns)]
        accs = [_dot(pc.astype(BF16), vc) + _dot(pn.astype(BF16), vn)
                for pc, pn, vc, vn in zip(pcs, pns, vcs, vns)]
        for sl, acc, l in zip(sls, accs, ls):
            o_ref[rows, sl] = (acc / l).astype(o_ref.dtype)
        lse_ref[rows, :] = _pack_heads([jnp.broadcast_to(m + jnp.log(l), (T, LANES)) for l, m in zip(ls, ms)])


def _cache_rows(cache):
    return cache.reshape(cache.shape[0], cache.shape[1] * 2 * ATT_HG, ATT_HD)


def _att_sample(proj_s, cache, group, T):
    nb, wb = cache.shape[0], cache.shape[1]
    assert wb > T
    _, dil = ATT_GROUPS[group]
    c0 = 3 * group
    slots = 2 * ATT_HG
    per = math.gcd(nb, max(1, 2 * SAMPLE_SEQS_PER_STEP * ATT_GROUPS[0][0] // wb))
    col = lambda k: pl.BlockSpec((per * T, ATT_GW), lambda b, k=k: (b, c0 + k))
    out_spec = pl.BlockSpec((per * T, ATT_GW), lambda b: (b, 0))
    rows = _cache_rows(cache)
    if dil > T and wb % dil == 0 and T % 8 == 0:
        nres = T
        rows = rows.reshape(nb, wb // dil, dil * slots, ATT_HD)
        cache_spec = pl.BlockSpec((per, wb // dil, nres * slots, ATT_HD), lambda b: (b, 0, 0, 0))
    else:
        nres = 0
        cache_spec = pl.BlockSpec((per, wb * slots, ATT_HD), lambda b: (b, 0, 0))
    return pl.pallas_call(
        functools.partial(_att_sample_kernel, group=group, dil=dil, wb=wb, T=T, nres=nres),
        grid=(nb // per,),
        in_specs=[col(0), col(1), col(2), cache_spec],
        out_specs=[out_spec, pl.BlockSpec((per * T, LANES), lambda b: (b, 0)),
                   pl.BlockSpec((per, T * slots, ATT_HD), lambda b: (b, 0, 0))],
        out_shape=[
            jax.ShapeDtypeStruct((nb * T, ATT_GW), F32),
            jax.ShapeDtypeStruct((nb * T, LANES), F32),
            jax.ShapeDtypeStruct((nb, T * slots, ATT_HD), F32),
        ],
        compiler_params=_params(("arbitrary",), 48),
        name=f"att_sample_g{group}",
    )(proj_s, proj_s, proj_s, rows)


def _epilogue_kernel(x_ref, gr_ref, ga_ref, yret_ref, o0_ref, o1_ref, o2_ref, l0_ref, l1_ref, l2_ref,
                     wr_ref, wa_ref, wo_ref, gf_ref, x1_ref, h2_ref, *scratch, by_residue):
    def position_order(ref, scr):
        dil = ref.shape[0]
        if dil == 1:
            return ref[0].astype(F32)
        for r in range(dil):
            slab = ref[r].astype(F32)
            for cb in range(scr.shape[0]):
                scr[cb, pl.ds(r, ref.shape[1], stride=dil), :] = slab[:, cb * LANES:(cb + 1) * LANES]
        return jnp.concatenate([scr[cb] for cb in range(scr.shape[0])], axis=-1)

    if by_residue:
        o0, o1, o2 = (position_order(ref, scr) for ref, scr in zip((o0_ref, o1_ref, o2_ref), scratch[0:3]))
        l0, l1, l2 = (position_order(ref, scr) for ref, scr in zip((l0_ref, l1_ref, l2_ref), scratch[3:6]))
    else:
        o0, o1, o2 = o0_ref[...], o1_ref[...], o2_ref[...]
        l0, l1, l2 = l0_ref[...], l1_ref[...], l2_ref[...]
    m = jnp.maximum(jnp.maximum(l0, l1), l2)
    e0, e1, e2 = jnp.exp(l0 - m), jnp.exp(l1 - m), jnp.exp(l2 - m)
    inv = 1.0 / (e0 + e1 + e2)
    o = _unpack_heads(e0 * inv) * o0 + _unpack_heads(e1 * inv) * o1 + _unpack_heads(e2 * inv) * o2
    att = _dot(o.astype(BF16), wa_ref[...])
    ret = _dot(yret_ref[...].astype(BF16), wr_ref[...])
    merged = (jax.nn.sigmoid(gr_ref[...].astype(F32)) * ret + jax.nn.sigmoid(ga_ref[...].astype(F32)) * att)
    x1 = x_ref[...] + _dot(merged.astype(BF16), wo_ref[...])
    x1_ref[...] = x1
    h2_ref[...] = _rmsnorm_rows(x1, gf_ref[...]).astype(BF16)


def _epilogue(x2d, proj2, yret, os_, lses, w_ret_o, w_att_o, w_o, g_ffn, *, tm, seq_len=None):
    M = x2d.shape[0]
    row = lambda w: pl.BlockSpec((tm, w), lambda m: (m, 0))
    const = lambda a: pl.BlockSpec(a.shape, lambda m: (0, 0), pipeline_mode=pl.Buffered(1))
    gf = g_ffn.reshape(1, D_MODEL)
    by_residue = seq_len is not None
    scratch = []
    if by_residue:
        tps = seq_len // tm
        att_specs = [pl.BlockSpec((None, a.shape[1], tm // a.shape[1], a.shape[3]),
                                  lambda m: (m // tps, 0, m % tps, 0)) for a in (*os_, *lses)]
        scratch = [pltpu.VMEM((a.shape[3] // LANES, tm, LANES), F32) for a in (*os_, *lses)]
    else:
        att_specs = [row(a.shape[1]) for a in (*os_, *lses)]
    return pl.pallas_call(
        functools.partial(_epilogue_kernel, by_residue=by_residue),
        grid=(M // tm,),
        in_specs=[
            row(D_MODEL),
            pl.BlockSpec((tm, D_MODEL), lambda m: (m, 0)),
            pl.BlockSpec((tm, D_MODEL), lambda m: (m, 1)),
            row(RET_WIDTH), *att_specs,
            const(w_ret_o), const(w_att_o), const(w_o), const(gf),
        ],
        out_specs=[row(D_MODEL), row(D_MODEL)],
        out_shape=[jax.ShapeDtypeStruct((M, D_MODEL), F32), jax.ShapeDtypeStruct((M, D_MODEL), BF16)],
        scratch_shapes=scratch,
        compiler_params=_params(("arbitrary",), 56),
        name="epilogue",
    )(x2d, proj2, proj2, yret, *os_, *lses, w_ret_o, w_att_o, w_o, gf)


def _gelu_gate(a, half_b):
    return (a * (1.0 + lax.erf(a * math.sqrt(0.5)))) * half_b


def _halve_b_half(x):
    return jnp.concatenate([x[..., :D_FF], 0.5 * x[..., D_FF:]], axis=-1)


def _conv3(u, r1, r2, w_ref, b_ref):
    return b_ref[...] + (r2 * w_ref[0:1, :] + r1 * w_ref[1:2, :] + u * w_ref[2:3, :])


UP_CHUNK = 256


SHIFT_CHUNK_ROWS = 4096


def _shift_plan(cache_rows, T, steps):
    nb, n, _ = cache_rows.shape
    shift = T * 2 * ATT_HG
    body = n - shift
    per_seq = -(-body // SHIFT_CHUNK_ROWS)
    while body % per_seq or (body // per_seq) % 8:
        per_seq += 1
    plan = dict(shift=shift, chunk=body // per_seq, per_seq=per_seq, count=nb * per_seq)
    assert plan["count"] < steps, "the background copy needs one grid step per chunk plus one to drain"
    return plan


def _shift_copy_step(s, src_ref, new_ref, dst_ref, buf, sems, i, *, shift, chunk, per_seq, count):
    n = src_ref.shape[1]
    tail = pltpu.make_async_copy(new_ref, dst_ref.at[:, pl.ds(n - shift, shift)], sems.at[i, 4])

    @pl.when(s == 0)
    def _():
        tail.start()

    @pl.when(s == count)
    def _():
        tail.wait()

    def read(c, slot):
        start = pl.multiple_of(shift + (c % per_seq) * chunk, 8)
        return pltpu.make_async_copy(src_ref.at[c // per_seq, pl.ds(start, chunk)], buf.at[slot], sems.at[i, slot])

    def write(c, slot):
        start = pl.multiple_of((c % per_seq) * chunk, 8)
        return pltpu.make_async_copy(buf.at[slot], dst_ref.at[c // per_seq, pl.ds(start, chunk)],
                                     sems.at[i, 2 + slot])

    slot = s % 2

    @pl.when(s == 0)
    def _():
        read(0, 0).start()

    @pl.when(s < count)
    def _():
        read(s, slot).wait()
        write(s, slot).start()

    @pl.when((s >= 1) & (s <= count))
    def _():
        write(s - 1, 1 - slot).wait()

    @pl.when(s + 1 < count)
    def _():
        read(s + 1, 1 - slot).start()


def _up_gate_kernel(h_ref, w_ref, cwa_ref, cwb_ref, cba_ref, cbb_ref, *rest, tiles_per_seq, shift_plans):
    ng = len(shift_plans)
    caches, new_rows, rest = rest[:ng], rest[ng:2 * ng], rest[2 * ng:]
    g_ref, ta_ref, tb_ref = rest[:3]
    advanced, rest = rest[3:3 + ng], rest[3 + ng:]
    u_scr, carry_scr = rest[:2]
    bufs, (sems,) = rest[2:2 + ng], rest[2 + ng:]
    m = pl.program_id(0)
    f = pl.program_id(1)
    tm = h_ref.shape[0]

    step = m * pl.num_programs(1) + f
    for i, plan in enumerate(shift_plans):
        _shift_copy_step(step, caches[i], new_rows[i], advanced[i], bufs[i], sems, i, **plan)

    @pl.when(m % tiles_per_seq == 0)
    def _():
        carry_scr[f] = jnp.zeros(carry_scr.shape[1:], F32)

    u_scr[0:8, :] = carry_scr[f]
    u_scr[8:, :] = _dot(h_ref[...], w_ref[...])
    carry_scr[f] = u_scr[tm:, :]
    for c in range(g_ref.shape[1] // UP_CHUNK):
        cols = slice(c * UP_CHUNK, (c + 1) * UP_CHUNK)

        def conv(half, cw_ref, cb_ref, tail_ref):
            pcols = slice((2 * c + half) * UP_CHUNK, (2 * c + half + 1) * UP_CHUNK)
            tail_ref[:, cols] = u_scr[tm:, pcols]
            u, r1, r2 = u_scr[8:, pcols], u_scr[7:tm + 7, pcols], u_scr[6:tm + 6, pcols]
            return cb_ref[:, cols] + (r2 * cw_ref[0:1, cols] + r1 * cw_ref[1:2, cols] + u * cw_ref[2:3, cols])

        a = conv(0, cwa_ref, cba_ref, ta_ref)
        b = conv(1, cwb_ref, cbb_ref, tb_ref)
        g_ref[:, cols] = _gelu_gate(a, b).astype(BF16)


def _up_gate(h2, w_up_ab, conv_w, conv_b, caches, new_rows, T, *, tm, tiles_per_seq):
    M = h2.shape[0]
    tf = 512
    nf = D_FF // tf
    cb = conv_b.reshape(1, 2 * D_FF)
    grid = (M // tm, nf)
    plans = tuple(_shift_plan(c, T, grid[0] * grid[1]) for c in caches)
    any_spec = pl.BlockSpec(memory_space=pl.ANY)
    return pl.pallas_call(
        functools.partial(_up_gate_kernel, tiles_per_seq=tiles_per_seq, shift_plans=plans),
        grid=grid,
        in_specs=[
            pl.BlockSpec((tm, D_MODEL), lambda m, f: (m, 0)),
            pl.BlockSpec((D_MODEL, 2 * tf), lambda m, f: (0, f)),
            pl.BlockSpec((CONV_W, tf), lambda m, f: (0, f)),
            pl.BlockSpec((CONV_W, tf), lambda m, f: (0, f + nf)),
            pl.BlockSpec((1, tf), lambda m, f: (0, f)),
            pl.BlockSpec((1, tf), lambda m, f: (0, f + nf)),
            *[any_spec for _ in (*caches, *new_rows)],
        ],
        out_specs=[
            pl.BlockSpec((tm, tf), lambda m, f: (m, f)),
            pl.BlockSpec((None, 8, tf), lambda m, f: (m, 0, f)),
            pl.BlockSpec((None, 8, tf), lambda m, f: (m, 0, f)),
            *[any_spec for _ in caches],
        ],
        out_shape=[
            jax.ShapeDtypeStruct((M, D_FF), BF16),
            jax.ShapeDtypeStruct((M // tm, 8, D_FF), F32),
            jax.ShapeDtypeStruct((M // tm, 8, D_FF), F32),
            *[jax.ShapeDtypeStruct(c.shape, F32) for c in caches],
        ],
        scratch_shapes=[
            pltpu.VMEM((tm + 8, 2 * tf), F32),
            pltpu.VMEM((nf, 8, 2 * tf), F32),
            *[pltpu.VMEM((2, p["chunk"], ATT_HD), F32) for p in plans],
            pltpu.SemaphoreType.DMA((len(caches), 5)),
        ],
        compiler_params=_params(("arbitrary", "arbitrary"), 56),
        name="up_gate",
    )(h2, w_up_ab, conv_w, conv_w, cb, cb, *caches, *new_rows)


def _down_kernel(g_ref, wd_ref, x1_ref, gfin_ref, y_ref, *, final_norm):
    y = x1_ref[...] + _dot(g_ref[...], wd_ref[...])
    y_ref[...] = _rmsnorm_rows(y, gfin_ref[...]) if final_norm else y


def _down(g, x1, w_down, g_final, *, tm, final_norm):
    M = g.shape[0]
    return pl.pallas_call(
        functools.partial(_down_kernel, final_norm=final_norm),
        grid=(M // tm,),
        in_specs=[
            pl.BlockSpec((tm, D_FF), lambda m: (m, 0)),
            pl.BlockSpec((D_FF, D_MODEL), lambda m: (0, 0), pipeline_mode=pl.Buffered(1)),
            pl.BlockSpec((tm, D_MODEL), lambda m: (m, 0)),
            pl.BlockSpec((1, D_MODEL), lambda m: (0, 0)),
        ],
        out_specs=pl.BlockSpec((tm, D_MODEL), lambda m: (m, 0)),
        out_shape=jax.ShapeDtypeStruct((M, D_MODEL), F32),
        compiler_params=_params(("arbitrary",), 56),
        name="down",
    )(g, w_down, x1, g_final.reshape(1, D_MODEL))


def _ffn_sample_kernel(h_ref, wua_ref, wub_ref, s0a_ref, s0b_ref, s1a_ref, s1b_ref, cwa_ref, cwb_ref,
                       cba_ref, cbb_ref, wd_ref, x1_ref, gfin_ref, y_ref, ua_ref, ub_ref, wu_cast_ref, wd_cast_ref,
                       *, T, final_norm):
    f = pl.program_id(0)
    M, tf = ua_ref.shape
    t = lax.broadcasted_iota(jnp.int32, (M, tf), 0) & (T - 1)
    wua, wub, wd = wua_ref[...].astype(BF16), wub_ref[...].astype(BF16), wd_ref[...].astype(BF16)
    for c in range(tf // UP_CHUNK):
        cols = slice(c * UP_CHUNK, (c + 1) * UP_CHUNK)
        wu_cast_ref[:, 2 * c * UP_CHUNK:(2 * c + 1) * UP_CHUNK] = wua[:, cols]
        wu_cast_ref[:, (2 * c + 1) * UP_CHUNK:(2 * c + 2) * UP_CHUNK] = wub[:, cols]
    wd_cast_ref[...] = wd
    h = h_ref[...]
    per_row = lambda s_ref: jnp.broadcast_to(s_ref[...][:, None, :], (M // T, T, tf)).reshape(M, tf)

    def conv(u, s0_ref, s1_ref, w_ref, b_ref, u_out_ref):
        u_out_ref[...] = u
        s0, s1 = per_row(s0_ref), per_row(s1_ref)
        r1 = jnp.where(t >= 1, pltpu.roll(u, 1, axis=0), s1)
        r2 = jnp.where(t >= 2, pltpu.roll(u, 2, axis=0), jnp.where(t == 0, s0, s1))
        return _conv3(u, r1, r2, w_ref, b_ref)

    a = conv(_dot(h, wua), s0a_ref, s1a_ref, cwa_ref, cba_ref, ua_ref)
    b = conv(_dot(h, wub), s0b_ref, s1b_ref, cwb_ref, cbb_ref, ub_ref)
    part = _dot(_gelu_gate(a, b).astype(BF16), wd)

    @pl.when(f == 0)
    def _():
        y_ref[...] = x1_ref[...] + part

    @pl.when(f > 0)
    def _():
        y_ref[...] += part

    if final_norm:
        @pl.when(f == pl.num_programs(0) - 1)
        def _():
            y_ref[...] = _rmsnorm_rows(y_ref[...], gfin_ref[...])


def _ffn_sample(h2, x1, state_conv, w_up, conv_w, conv_b, w_down, g_final, T, final_norm):
    M = h2.shape[0]
    nb = M // T
    tf = 512
    nf = D_FF // tf
    F2 = 2 * D_FF
    assert state_conv.shape[1] == CONV_W - 1 == 2
    s0, s1 = state_conv[:, 0].astype(F32), state_conv[:, 1].astype(F32)
    cb = conv_b.reshape(1, F2)
    gfin = g_final.reshape(1, D_MODEL)
    full = lambda w: pl.BlockSpec((M, w), lambda f: (0, 0))
    ca = lambda r, w=tf: pl.BlockSpec((r, w), lambda f: (0, f))
    cbk = lambda r, w=tf: pl.BlockSpec((r, w), lambda f: (0, f + nf))
    down_rows = pl.BlockSpec((tf, D_MODEL), lambda f: (f, 0))
    y, ua, ub, w_up_ab, w_down_bf16 = pl.pallas_call(
        functools.partial(_ffn_sample_kernel, T=T, final_norm=final_norm),
        grid=(nf,),
        in_specs=[
            full(D_MODEL), ca(D_MODEL), cbk(D_MODEL), ca(nb), cbk(nb), ca(nb), cbk(nb),
            ca(CONV_W), cbk(CONV_W), ca(1), cbk(1),
            down_rows,
            full(D_MODEL), pl.BlockSpec((1, D_MODEL), lambda f: (0, 0)),
        ],
        out_specs=[full(D_MODEL), ca(M), ca(M), ca(D_MODEL, 2 * tf), down_rows],
        out_shape=[
            jax.ShapeDtypeStruct((M, D_MODEL), F32),
            jax.ShapeDtypeStruct((M, D_FF), F32),
            jax.ShapeDtypeStruct((M, D_FF), F32),
            jax.ShapeDtypeStruct((D_MODEL, F2), BF16),
            jax.ShapeDtypeStruct((D_FF, D_MODEL), BF16),
        ],
        compiler_params=_params(("arbitrary",), 56),
        name="ffn_sample",
    )(h2, w_up, w_up, s0, s0, s1, s1, conv_w, conv_w, cb, cb, w_down, x1, gfin)
    u = jnp.concatenate([ua, ub], axis=-1).reshape(nb, T, F2)
    return y, u[:, T - (CONV_W - 1):], w_up_ab, w_down_bf16


def kernel(x_prompt, x_sample, state_ret, cache_kv_w128, cache_kv_w512, cache_kv_w2048, state_conv, g_mix, w_in,
           gn_ret, w_ret_o, w_att_o, w_o, g_ffn, w_up, conv_w, conv_b, w_down, g_final):
    B, S, _ = x_prompt.shape
    NB, T, _ = x_sample.shape
    depth = w_in.shape[0]
    caches = (cache_kv_w128, cache_kv_w512, cache_kv_w2048)
    TM = 1024
    tiles_per_seq = S // TM

    xp = x_prompt.reshape(B * S, D_MODEL)
    xs = x_sample.reshape(NB * T, D_MODEL)
    ret_p, ret_s, conv_p, conv_s = [], [], [], []
    kv_p = [[] for _ in range(N_GROUPS)]
    kv_s = [[] for _ in range(N_GROUPS)]
    for l in range(depth):
        w_ret_o_l = w_ret_o[l].astype(BF16)
        w_att_o_l = w_att_o[l].astype(BF16)
        w_o_l = w_o[l].astype(BF16)
        conv_w_l, conv_b_l = _halve_b_half(conv_w[l]), _halve_b_half(conv_b[l])

        ret_tiles = 4 * RET_WIDTH // CAST_TN
        att_tiles = ATT_COLS // CAST_TN
        in_tiles = IN_COLS // CAST_TN
        proj_s, w_main_l = _inproj_sample(xs, g_mix[l], w_in[l], lambda n: (n + ret_tiles + att_tiles) % in_tiles,
                                          MAIN_COLS // CAST_TN, "inproj_sample_main")
        att_s, w_att_l = _inproj_sample(xs, g_mix[l], w_in[l], lambda n: n + ret_tiles, att_tiles,
                                        "inproj_sample_att")
        yret_s, st_s = _ret_sample(proj_s, state_ret[l], gn_ret[l], T)
        ret_s.append(st_s)
        os_, lses, new_rows = [], [], []
        for g in range(N_GROUPS):
            o, lse, rows = _att_sample(att_s, caches[g][l], g, T)
            os_.append(o)
            lses.append(lse)
            new_rows.append(rows)
        x1s, h2s = _epilogue(xs, proj_s, yret_s, os_, lses, w_ret_o_l, w_att_o_l, w_o_l, g_ffn[l], tm=NB * T)
        xs, cv, w_up_l, w_down_l = _ffn_sample(h2s, x1s, state_conv[l], w_up[l], conv_w_l, conv_b_l, w_down[l],
                                               g_final, T, final_norm=l == depth - 1)
        conv_s.append(cv)

        proj, h = _inproj_main(xp, g_mix[l], w_main_l, tm=TM)
        yret, st = _ret_prompt(proj.reshape(B, S, MAIN_COLS), gn_ret[l])
        ret_p.append(st)
        os_, lses = [], []
        for g, (win, _) in enumerate(ATT_GROUPS):
            qkv, kvf = _inproj_att(h, w_att_l, g, B=B, S=S, tm=512)
            o, lse = _att_prompt(qkv, g)
            os_.append(o)
            lses.append(lse)
            kv_p[g].append(kvf.reshape(B, min(win, S), 2, ATT_HG, ATT_HD))
        x1, h2 = _epilogue(xp, proj, yret.reshape(B * S, RET_WIDTH), os_, lses,
                           w_ret_o_l, w_att_o_l, w_o_l, g_ffn[l], tm=256, seq_len=S)
        gated, tail_a, tail_b, *advanced = _up_gate(h2, w_up_l, conv_w_l, conv_b_l,
                                                    [_cache_rows(c[l]) for c in caches], new_rows, T,
                                                    tm=TM, tiles_per_seq=tiles_per_seq)
        for g in range(N_GROUPS):
            kv_s[g].append(advanced[g].reshape(caches[g][l].shape))
        utail = jnp.concatenate([tail_a, tail_b], axis=-1)
        conv_p.append(utail[tiles_per_seq - 1::tiles_per_seq, 8 - (CONV_W - 1):])
        xp = _down(gated, x1, w_down_l, g_final, tm=512, final_norm=l == depth - 1)

    return (xp.reshape(B, S, D_MODEL), xs.reshape(NB, T, D_MODEL),
            jnp.stack(ret_p), jnp.stack(ret_s),
            jnp.stack(kv_p[0]), jnp.stack(kv_s[0]),
            jnp.stack(kv_p[1]), jnp.stack(kv_s[1]),
            jnp.stack(kv_p[2]), jnp.stack(kv_s[2]),
            jnp.stack(conv_p), jnp.stack(conv_s))
```

```python
import functools
import math

import jax
import jax.numpy as jnp
from jax import lax
from jax.experimental import pallas as pl
from jax.experimental.pallas import tpu as pltpu

D_MODEL = 2048
RET_HEADS = 8
RET_DK = 128
RET_DV = 128
RET_WIDTH = RET_HEADS * RET_DV
RET_CHUNK = 128
ATT_GROUPS = ((128, 1), (512, 4), (2048, 16))
N_GROUPS = 3
ATT_HG = 4
ATT_HD = 128
ATT_GW = ATT_HG * ATT_HD
ATT_SPAN = 128
ATT_BLK = 128
ALIBI_MAX = 8.0
D_FF = 5632
CONV_W = 3
EPS = 1e-6
NEG = -1e30
IN_COLS = 4 * RET_WIDTH + 3 * N_GROUPS * ATT_GW + 2 * D_MODEL

GATE_COLS = 2 * D_MODEL
RET_COL0 = GATE_COLS
MAIN_COLS = GATE_COLS + 4 * RET_WIDTH
ATT_COLS = 3 * N_GROUPS * ATT_GW
CAST_TN = 512

LANES = 128
MIB = 1024 * 1024
BF16 = jnp.bfloat16
F32 = jnp.float32


def _params(semantics, vmem_mib, flags=None):
    return pltpu.CompilerParams(dimension_semantics=semantics, vmem_limit_bytes=vmem_mib * MIB, flags=flags)


def _dot(a, b):
    return jnp.dot(a, b, preferred_element_type=F32)


def _dot_nt(a, b):
    return lax.dot_general(a, b, (((1,), (1,)), ((), ())), preferred_element_type=F32)


def _rmsnorm_rows(x, g):
    ms = jnp.mean(x * x, axis=-1, keepdims=True)
    return x * lax.rsqrt(ms + EPS) * g


def _inproj_sample_kernel(x_ref, g_ref, w_ref, proj_ref, wcast_ref, h_scr):
    @pl.when(pl.program_id(0) == 0)
    def _():
        h_scr[...] = _rmsnorm_rows(x_ref[...], g_ref[...]).astype(BF16)

    w = w_ref[...].astype(BF16)
    wcast_ref[...] = w
    proj_ref[...] = _dot(h_scr[...], w)


def _inproj_sample(x2d, g, w, src_tile, n_tiles, name):
    M = x2d.shape[0]
    tn = CAST_TN
    return pl.pallas_call(
        _inproj_sample_kernel,
        grid=(n_tiles,),
        in_specs=[
            pl.BlockSpec((M, D_MODEL), lambda n: (0, 0)),
            pl.BlockSpec((1, D_MODEL), lambda n: (0, 0)),
            pl.BlockSpec((D_MODEL, tn), lambda n: (0, src_tile(n))),
        ],
        out_specs=[pl.BlockSpec((M, tn), lambda n: (0, n)), pl.BlockSpec((D_MODEL, tn), lambda n: (0, n))],
        out_shape=[jax.ShapeDtypeStruct((M, n_tiles * tn), F32), jax.ShapeDtypeStruct((D_MODEL, n_tiles * tn), BF16)],
        scratch_shapes=[pltpu.VMEM((M, D_MODEL), BF16)],
        compiler_params=_params(("arbitrary",), 32),
        name=name,
    )(x2d, g.reshape(1, D_MODEL), w)


def _inproj_main_kernel(x_ref, g_ref, w_ref, main_ref, h_ref):
    @pl.when(pl.program_id(1) == 0)
    def _():
        h_ref[...] = _rmsnorm_rows(x_ref[...], g_ref[...]).astype(BF16)

    main_ref[...] = _dot(h_ref[...], w_ref[...]).astype(BF16)


def _inproj_main(x2d, g, w_main, *, tm):
    M = x2d.shape[0]
    N = w_main.shape[1]
    tn = 1024
    return pl.pallas_call(
        _inproj_main_kernel,
        grid=(M // tm, N // tn),
        in_specs=[
            pl.BlockSpec((tm, D_MODEL), lambda m, n: (m, 0)),
            pl.BlockSpec((1, D_MODEL), lambda m, n: (0, 0)),
            pl.BlockSpec((D_MODEL, tn), lambda m, n: (0, n)),
        ],
        out_specs=[pl.BlockSpec((tm, tn), lambda m, n: (m, n)), pl.BlockSpec((tm, D_MODEL), lambda m, n: (m, 0))],
        out_shape=[jax.ShapeDtypeStruct((M, N), BF16), jax.ShapeDtypeStruct((M, D_MODEL), BF16)],
        compiler_params=_params(("arbitrary", "arbitrary"), 48),
        name="inproj_main",
    )(x2d, g.reshape(1, D_MODEL), w_main)


def _inproj_att_kernel(h_ref, w_ref, a_ref, kv_ref, *scratch, dil):
    acc = _dot(h_ref[...], w_ref[...])
    slots = 2 * ATT_HG
    rows = kv_ref.shape[0] // slots
    for slot in range(slots):
        cols = slice(ATT_GW + slot * ATT_HD, ATT_GW + (slot + 1) * ATT_HD)
        kv_ref[pl.ds(slot, rows, stride=slots), :] = acc[acc.shape[0] - rows:, cols]
    if dil == 1:
        a_ref[0] = acc.astype(BF16)
    else:
        acc_scr, = scratch
        rows = acc_scr.shape[1] // dil
        for cb in range(acc_scr.shape[0]):
            lanes = slice(cb * LANES, (cb + 1) * LANES)
            acc_scr[cb] = acc[:, lanes]
            for r in range(dil):
                a_ref[r, :, lanes] = acc_scr[cb, pl.ds(r, rows, stride=dil), :].astype(BF16)


def _inproj_att(h, w_att, group, *, B, S, tm):
    M = h.shape[0]
    tn = 3 * ATT_GW
    tps = S // tm
    win, dil = ATT_GROUPS[group]
    win = min(win, S)
    rows = min(win, tm)
    first = tps - win // rows
    slots = 2 * ATT_HG
    return pl.pallas_call(
        functools.partial(_inproj_att_kernel, dil=dil),
        grid=(M // tm,),
        in_specs=[
            pl.BlockSpec((tm, D_MODEL), lambda m: (m, 0)),
            pl.BlockSpec((D_MODEL, tn), lambda m: (0, group)),
        ],
        out_specs=[
            pl.BlockSpec((None, dil, tm // dil, tn), lambda m: (m // tps, 0, m % tps, 0)),
            pl.BlockSpec((None, rows * slots, ATT_HD), lambda m: (m // tps, jnp.maximum(m % tps - first, 0), 0)),
        ],
        out_shape=[
            jax.ShapeDtypeStruct((B, dil, S // dil, tn), BF16),
            jax.ShapeDtypeStruct((B, win * slots, ATT_HD), F32),
        ],
        scratch_shapes=[] if dil == 1 else [pltpu.VMEM((tn // LANES, tm, LANES), F32)],
        compiler_params=_params(("arbitrary",), 48),
        name=f"inproj_att_g{group}",
    )(h, w_att)


def _ret_tables(lq):
    lg = jnp.log(1.0 - jnp.exp2(-5.0 - jnp.arange(RET_HEADS, dtype=F32)))
    scale = RET_DK ** -0.5
    i = jnp.arange(lq, dtype=F32)
    j = jnp.arange(RET_CHUNK, dtype=F32)
    diff = i[:, None] - j[None, :]
    live = (diff >= 0) & (j[None, :] < lq)
    dm = jnp.where(live[None], jnp.exp(jnp.maximum(diff, 0.0)[None] * lg[:, None, None]), 0.0) * scale
    qd = jnp.exp((i[None, :] + 1.0) * lg[:, None])
    qd = jnp.broadcast_to(qd[:, :, None], (RET_HEADS, lq, RET_DV))
    kd = jnp.where(j[None, :] < lq, jnp.exp(jnp.maximum(lq - 1.0 - j, 0.0)[None, :] * lg[:, None]), 0.0) * scale
    kd = jnp.broadcast_to(kd[:, :, None], (RET_HEADS, RET_CHUNK, RET_DK))
    gl = jnp.exp(lq * lg)
    return dm.astype(F32), qd.astype(F32), kd.astype(F32), gl.astype(F32)


def _ret_heads(qs, ks, vs, sts, dms, qds, kds, gls):
    ss = [(_dot_nt(q, k) * dm).astype(BF16) for q, k, dm in zip(qs, ks, dms)]
    crosses = [_dot(q, st.astype(BF16)) for q, st in zip(qs, sts)]
    kts = [(k.astype(F32) * kd).T.astype(BF16) for k, kd in zip(ks, kds)]
    inners = [_dot(s, v) for s, v in zip(ss, vs)]
    updates = [_dot(kt, v) for kt, v in zip(kts, vs)]
    ys = [inner + cross * qd for inner, cross, qd in zip(inners, crosses, qds)]
    new_sts = [st * gl + upd for st, gl, upd in zip(sts, gls, updates)]
    return ys, new_sts


def _groupnorm_gates(ys, gns, rgs):
    mus = [jnp.mean(y, axis=-1, keepdims=True) for y in ys]
    ycs = [y - mu for y, mu in zip(ys, mus)]
    rstds = [lax.rsqrt(jnp.mean(yc * yc, axis=-1, keepdims=True) + EPS) for yc in ycs]
    return [(rg * jax.nn.sigmoid(rg)) * ((yc * rstd) * gn) for yc, rstd, gn, rg in zip(ycs, rstds, gns, rgs)]


def _head_slices():
    return [slice(h * RET_DK, (h + 1) * RET_DK) for h in range(RET_HEADS)]


def _ret_prompt_kernel(gl_ref, q_ref, k_ref, v_ref, rg_ref, dm_ref, qd_ref, kd_ref, gn_ref, y_ref, st_ref):
    c = pl.program_id(1)

    @pl.when(c == 0)
    def _():
        st_ref[...] = jnp.zeros_like(st_ref)

    sls = _head_slices()
    heads = range(RET_HEADS)
    sts = [st_ref[h] for h in heads]
    for i in range(q_ref.shape[0] // RET_CHUNK):
        rows = slice(i * RET_CHUNK, (i + 1) * RET_CHUNK)
        ys, sts = _ret_heads([q_ref[rows, sl] for sl in sls], [k_ref[rows, sl] for sl in sls],
                             [v_ref[rows, sl] for sl in sls], sts, [dm_ref[h] for h in heads],
                             [qd_ref[h] for h in heads], [kd_ref[h] for h in heads], [gl_ref[h] for h in heads])
        outs = _groupnorm_gates(ys, [gn_ref[:, sl] for sl in sls], [rg_ref[rows, sl].astype(F32) for sl in sls])
        for sl, out in zip(sls, outs):
            y_ref[rows, sl] = out.astype(y_ref.dtype)
    for h in heads:
        st_ref[h] = sts[h]


RET_STEP_CHUNKS = 2


def _ret_prompt(proj3, gn_ret):
    B, S, _ = proj3.shape
    L = RET_CHUNK
    dm, qd, kd, gl = _ret_tables(L)
    cb = RET_COL0 // RET_WIDTH
    rows = L * math.gcd(RET_STEP_CHUNKS, S // L)
    col = lambda k: pl.BlockSpec((None, rows, RET_WIDTH), lambda b, c, k=k: (b, c, cb + k))
    tab = pl.BlockSpec((RET_HEADS, L, RET_DK), lambda b, c: (0, 0, 0))
    return pl.pallas_call(
        _ret_prompt_kernel,
        grid=(B, S // rows),
        in_specs=[
            pl.BlockSpec(memory_space=pltpu.SMEM),
            col(0), col(1), col(2), col(3), tab, tab, tab,
            pl.BlockSpec((1, RET_WIDTH), lambda b, c: (0, 0)),
        ],
        out_specs=[
            pl.BlockSpec((None, rows, RET_WIDTH), lambda b, c: (b, c, 0)),
            pl.BlockSpec((None, RET_HEADS, RET_DK, RET_DV), lambda b, c: (b, 0, 0, 0)),
        ],
        out_shape=[
            jax.ShapeDtypeStruct((B, S, RET_WIDTH), BF16),
            jax.ShapeDtypeStruct((B, RET_HEADS, RET_DK, RET_DV), F32),
        ],
        compiler_params=_params(("arbitrary", "arbitrary"), 32),
        name="ret_prompt",
    )(gl, proj3, proj3, proj3, proj3, dm, qd, kd, gn_ret.reshape(1, RET_WIDTH))


def _pad_rows(x, rows):
    x = x.astype(F32)
    return jnp.concatenate([x, jnp.zeros((rows - x.shape[0], x.shape[1]), F32)], axis=0)


def _ret_sample_kernel(gl_ref, q_ref, k_ref, v_ref, rg_ref, st_in_ref, dm_ref, qd_ref, kd_ref, gn_ref,
                       y_ref, st_ref, *, T):
    sls = _head_slices()
    heads = range(RET_HEADS)
    for j in range(st_ref.shape[0]):
        rows = slice(j * T, (j + 1) * T)
        ys, new_sts = _ret_heads([q_ref[rows, sl].astype(BF16) for sl in sls],
                                 [_pad_rows(k_ref[rows, sl], RET_CHUNK).astype(BF16) for sl in sls],
                                 [_pad_rows(v_ref[rows, sl], RET_CHUNK).astype(BF16) for sl in sls],
                                 [st_in_ref[j, h] for h in heads], [dm_ref[h] for h in heads],
                                 [qd_ref[h] for h in heads], [kd_ref[h] for h in heads], [gl_ref[h] for h in heads])
        for h in heads:
            st_ref[j, h] = new_sts[h]
        outs = _groupnorm_gates(ys, [gn_ref[:, sl] for sl in sls], [rg_ref[rows, sl].astype(F32) for sl in sls])
        for sl, out in zip(sls, outs):
            y_ref[rows, sl] = out.astype(y_ref.dtype)


SAMPLE_SEQS_PER_STEP = 4


def _ret_sample(proj_s, state, gn_ret, T):
    nb = state.shape[0]
    per = math.gcd(SAMPLE_SEQS_PER_STEP, nb)
    dm, qd, kd, gl = _ret_tables(T)
    cb = RET_COL0 // RET_WIDTH
    col = lambda k: pl.BlockSpec((per * T, RET_WIDTH), lambda b, k=k: (b, cb + k))
    st_spec = pl.BlockSpec((per, RET_HEADS, RET_DK, RET_DV), lambda b: (b, 0, 0, 0))
    return pl.pallas_call(
        functools.partial(_ret_sample_kernel, T=T),
        grid=(nb // per,),
        in_specs=[
            pl.BlockSpec(memory_space=pltpu.SMEM),
            col(0), col(1), col(2), col(3), st_spec,
            pl.BlockSpec((RET_HEADS, T, RET_CHUNK), lambda b: (0, 0, 0)),
            pl.BlockSpec((RET_HEADS, T, RET_DV), lambda b: (0, 0, 0)),
            pl.BlockSpec((RET_HEADS, RET_CHUNK, RET_DK), lambda b: (0, 0, 0)),
            pl.BlockSpec((1, RET_WIDTH), lambda b: (0, 0)),
        ],
        out_specs=[pl.BlockSpec((per * T, RET_WIDTH), lambda b: (b, 0)), st_spec],
        out_shape=[
            jax.ShapeDtypeStruct((nb * T, RET_WIDTH), F32),
            jax.ShapeDtypeStruct(state.shape, F32),
        ],
        compiler_params=_params(("arbitrary",), 32),
        name="ret_sample",
    )(gl, proj_s, proj_s, proj_s, proj_s, state, dm, qd, kd, gn_ret.reshape(1, RET_WIDTH))


def _alibi_slope(g, h):
    n = N_GROUPS * ATT_HG
    return 2.0 ** (-ALIBI_MAX * (g * ATT_HG + h + 1) / n)


ATT_SUB = 8


def _pack_heads(xs):
    lane = lax.broadcasted_iota(jnp.int32, xs[0].shape, 1)
    out = xs[-1]
    for h in range(len(xs) - 2, -1, -1):
        out = jnp.where(lane == h, xs[h], out)
    return out


def _unpack_heads(x):
    return jnp.concatenate([jnp.broadcast_to(x[:, h:h + 1], (x.shape[0], ATT_HD)) for h in range(ATT_HG)], axis=-1)


def _att_prompt_kernel(q_ref, kp_ref, kc_ref, vp_ref, vc_ref, o_ref, lse_ref, k_scr, v_scr, bias_scr,
                       *, group, dil, nsub):
    j = pl.program_id(2)
    blk = ATT_BLK
    hd = ATT_HD
    k_scr[0:blk] = kp_ref[...]
    k_scr[blk:] = kc_ref[...]
    ones = jnp.ones((blk * (nsub + 1), hd), BF16)
    for h in range(ATT_HG):
        v_scr[0:blk, 2 * h * hd:(2 * h + 1) * hd] = vp_ref[:, h * hd:(h + 1) * hd]
        v_scr[blk:, 2 * h * hd:(2 * h + 1) * hd] = vc_ref[:, h * hd:(h + 1) * hd]
        v_scr[:, (2 * h + 1) * hd:(2 * h + 2) * hd] = ones

    r = lax.broadcasted_iota(jnp.int32, (blk, 2 * blk), 0)
    c = lax.broadcasted_iota(jnp.int32, (blk, 2 * blk), 1)
    dist = blk + r - c
    in_band = (dist >= 0) & (dist <= ATT_SPAN)
    for h in range(ATT_HG):
        bias_scr[h] = jnp.where(in_band, (-_alibi_slope(group, h) * dil) * dist.astype(F32), NEG)
    no_prev = jnp.where((c < blk) & (j == 0), NEG, 0.0)

    scale = hd ** -0.5
    for i in range(nsub):
        rows = slice(i * blk, (i + 1) * blk)
        keys = slice(i * blk, (i + 2) * blk)
        heads = range(ATT_HG)
        ss = [_dot_nt(q_ref[rows, h * hd:(h + 1) * hd], k_scr[keys, h * hd:(h + 1) * hd]) * scale + bias_scr[h]
              for h in heads]
        if i == 0:
            ss = [s + no_prev for s in ss]
        ms = [s.max(axis=-1, keepdims=True) for s in ss]
        ps = [jnp.exp(s - m).astype(BF16) for s, m in zip(ss, ms)]
        rs = [_dot(p, v_scr[keys, 2 * h * hd:(2 * h + 2) * hd]) for h, p in zip(heads, ps)]
        lses = []
        for h, res, m in zip(heads, rs, ms):
            l = res[:, hd:]
            o_ref[rows, h * hd:(h + 1) * hd] = (res[:, :hd] / l).astype(o_ref.dtype)
            lses.append(m + jnp.log(l))
        lse_ref[rows, :] = _pack_heads(lses)


def _att_prompt(qkv, group):
    B, dil, L, _ = qkv.shape
    nsub = math.gcd(ATT_SUB, L // ATT_BLK)
    rows = nsub * ATT_BLK
    cur = lambda which: pl.BlockSpec((None, None, rows, ATT_GW), lambda b, r, j: (b, r, j, which))
    prev = lambda which: pl.BlockSpec(
        (None, None, ATT_BLK, ATT_GW), lambda b, r, j: (b, r, jnp.maximum(j * nsub - 1, 0), which))
    out_spec = lambda w: pl.BlockSpec((None, None, rows, w), lambda b, r, j: (b, r, j, 0))
    return pl.pallas_call(
        functools.partial(_att_prompt_kernel, group=group, dil=dil, nsub=nsub),
        grid=(B, dil, L // rows),
        in_specs=[cur(0), prev(1), cur(1), prev(2), cur(2)],
        out_specs=[out_spec(ATT_GW), out_spec(LANES)],
        out_shape=[
            jax.ShapeDtypeStruct((B, dil, L, ATT_GW), BF16),
            jax.ShapeDtypeStruct((B, dil, L, LANES), F32),
        ],
        scratch_shapes=[
            pltpu.VMEM((rows + ATT_BLK, ATT_GW), BF16),
            pltpu.VMEM((rows + ATT_BLK, 2 * ATT_GW), BF16),
            pltpu.VMEM((ATT_HG, ATT_BLK, 2 * ATT_BLK), F32),
        ],
        compiler_params=_params(("arbitrary", "arbitrary", "arbitrary"), 32),
        name=f"att_prompt_g{group}",
    )(qkv, qkv, qkv, qkv, qkv)


def _att_sample_kernel(q_ref, k_ref, v_ref, c_ref, o_ref, lse_ref, nc_ref, *, group, dil, wb, T, nres):
    slots = 2 * ATT_HG
    nk = c_ref.shape[1] * nres if nres else wb
    t = lax.broadcasted_iota(jnp.int32, (T, nk), 0)
    c = lax.broadcasted_iota(jnp.int32, (T, nk), 1)
    pos = (c // nres) * dil + c % nres if nres else c
    dist_c = wb + t - pos
    valid_c = ((dist_c & (dil - 1)) == 0) & (dist_c <= ATT_SPAN * dil)
    tn = lax.broadcasted_iota(jnp.int32, (T, ATT_BLK), 0)
    cn = lax.broadcasted_iota(jnp.int32, (T, ATT_BLK), 1)
    dist_n = tn - cn
    valid_n = (dist_n >= 0) & ((dist_n & (dil - 1)) == 0) & (cn < T)
    scale = ATT_HD ** -0.5
    heads = range(ATT_HG)
    sls = [slice(h * ATT_HD, (h + 1) * ATT_HD) for h in heads]
    bias_c = jnp.where(valid_c, dist_c.astype(F32), -NEG)
    bias_n = jnp.where(valid_n, dist_n.astype(F32), -NEG)
    for j in range(c_ref.shape[0]):
        rows = slice(j * T, (j + 1) * T)
        for h, sl in zip(heads, sls):
            nc_ref[j, pl.ds(h, T, stride=slots), :] = k_ref[rows, sl]
            nc_ref[j, pl.ds(ATT_HG + h, T, stride=slots), :] = v_ref[rows, sl]
        qs = [q_ref[rows, sl].astype(BF16) for sl in sls]
        if nres:
            slot_rows = lambda s: c_ref[j, :, pl.ds(s, nres, stride=slots), :].reshape(nk, ATT_HD).astype(BF16)
        else:
            slot_rows = lambda s: c_ref[j, pl.ds(s, wb, stride=slots), :].astype(BF16)
        kcs = [slot_rows(h) for h in heads]
        vcs = [slot_rows(ATT_HG + h) for h in heads]
        kns = [_pad_rows(k_ref[rows, sl], ATT_BLK).astype(BF16) for sl in sls]
        vns = [_pad_rows(v_ref[rows, sl], ATT_BLK).astype(BF16) for sl in sls]
        scs = [_dot_nt(q, kc) * scale - _alibi_slope(group, h) * bias_c for h, q, kc in zip(heads, qs, kcs)]
        sns = [_dot_nt(q, kn) * scale - _alibi_slope(group, h) * bias_n for h, q, kn in zip(heads, qs, kns)]
        ms = [jnp.maximum(sc.max(axis=-1, keepdims=True), sn.max(axis=-1, keepdims=True))
              for sc, sn in zip(scs, sns)]
        pcs = [jnp.exp(sc - m) for sc, m in zip(scs, ms)]
        pns = [jnp.exp(sn - m) for sn, m in zip(sns, ms)]
        ls = [pc.sum(axis=-1, keepdims=True) + pn.sum(axis=-1, keepdims=True) for pc, pn in zip(pcs, pns)]
        accs = [_dot(pc.astype(BF16), vc) + _dot(pn.astype(BF16), vn)
                for pc, pn, vc, vn in zip(pcs, pns, vcs, vns)]
        for sl, acc, l in zip(sls, accs, ls):
            o_ref[rows, sl] = (acc / l).astype(o_ref.dtype)
        lse_ref[rows, :] = _pack_heads([jnp.broadcast_to(m + jnp.log(l), (T, LANES)) for l, m in zip(ls, ms)])


def _cache_rows(cache):
    return cache.reshape(cache.shape[0], cache.shape[1] * 2 * ATT_HG, ATT_HD)


def _att_sample(proj_s, cache, group, T):
    nb, wb = cache.shape[0], cache.shape[1]
    assert wb > T
    _, dil = ATT_GROUPS[group]
    c0 = 3 * group
    slots = 2 * ATT_HG
    per = math.gcd(nb, max(1, 2 * SAMPLE_SEQS_PER_STEP * ATT_GROUPS[0][0] // wb))
    col = lambda k: pl.BlockSpec((per * T, ATT_GW), lambda b, k=k: (b, c0 + k))
    out_spec = pl.BlockSpec((per * T, ATT_GW), lambda b: (b, 0))
    rows = _cache_rows(cache)
    if dil > T and wb % dil == 0 and T % 8 == 0:
        nres = T
        rows = rows.reshape(nb, wb // dil, dil * slots, ATT_HD)
        cache_spec = pl.BlockSpec((per, wb // dil, nres * slots, ATT_HD), lambda b: (b, 0, 0, 0))
    else:
        nres = 0
        cache_spec = pl.BlockSpec((per, wb * slots, ATT_HD), lambda b: (b, 0, 0))
    return pl.pallas_call(
        functools.partial(_att_sample_kernel, group=group, dil=dil, wb=wb, T=T, nres=nres),
        grid=(nb // per,),
        in_specs=[col(0), col(1), col(2), cache_spec],
        out_specs=[out_spec, pl.BlockSpec((per * T, LANES), lambda b: (b, 0)),
                   pl.BlockSpec((per, T * slots, ATT_HD), lambda b: (b, 0, 0))],
        out_shape=[
            jax.ShapeDtypeStruct((nb * T, ATT_GW), F32),
            jax.ShapeDtypeStruct((nb * T, LANES), F32),
            jax.ShapeDtypeStruct((nb, T * slots, ATT_HD), F32),
        ],
        compiler_params=_params(("arbitrary",), 48),
        name=f"att_sample_g{group}",
    )(proj_s, proj_s, proj_s, rows)


def _epilogue_kernel(x_ref, gr_ref, ga_ref, yret_ref, o0_ref, o1_ref, o2_ref, l0_ref, l1_ref, l2_ref,
                     wr_ref, wa_ref, wo_ref, gf_ref, x1_ref, h2_ref, *scratch, by_residue):
    def position_order(ref, scr):
        dil = ref.shape[0]
        if dil == 1:
            return ref[0].astype(F32)
        for r in range(dil):
            slab = ref[r].astype(F32)
            for cb in range(scr.shape[0]):
                scr[cb, pl.ds(r, ref.shape[1], stride=dil), :] = slab[:, cb * LANES:(cb + 1) * LANES]
        return jnp.concatenate([scr[cb] for cb in range(scr.shape[0])], axis=-1)

    if by_residue:
        o0, o1, o2 = (position_order(ref, scr) for ref, scr in zip((o0_ref, o1_ref, o2_ref), scratch[0:3]))
        l0, l1, l2 = (position_order(ref, scr) for ref, scr in zip((l0_ref, l1_ref, l2_ref), scratch[3:6]))
    else:
        o0, o1, o2 = o0_ref[...], o1_ref[...], o2_ref[...]
        l0, l1, l2 = l0_ref[...], l1_ref[...], l2_ref[...]
    m = jnp.maximum(jnp.maximum(l0, l1), l2)
    e0, e1, e2 = jnp.exp(l0 - m), jnp.exp(l1 - m), jnp.exp(l2 - m)
    inv = 1.0 / (e0 + e1 + e2)
    o = _unpack_heads(e0 * inv) * o0 + _unpack_heads(e1 * inv) * o1 + _unpack_heads(e2 * inv) * o2
    rows = x_ref.shape[0]
    halves = [slice(0, rows // 2), slice(rows // 2, rows)] if rows % 32 == 0 else [slice(0, rows)]
    atts = [_dot(o[r].astype(BF16), wa_ref[...]) for r in halves]
    rets = [_dot(yret_ref[r, :].astype(BF16), wr_ref[...]) for r in halves]
    mergeds = [(jax.nn.sigmoid(gr_ref[r, :].astype(F32)) * ret + jax.nn.sigmoid(ga_ref[r, :].astype(F32)) * att)
               for r, att, ret in zip(halves, atts, rets)]
    x1s = [x_ref[r, :] + _dot(merged.astype(BF16), wo_ref[...]) for r, merged in zip(halves, mergeds)]
    for r, x1 in zip(halves, x1s):
        x1_ref[r, :] = x1
        h2_ref[r, :] = _rmsnorm_rows(x1, gf_ref[...]).astype(BF16)


def _epilogue(x2d, proj2, yret, os_, lses, w_ret_o, w_att_o, w_o, g_ffn, *, tm, seq_len=None):
    M = x2d.shape[0]
    row = lambda w: pl.BlockSpec((tm, w), lambda m: (m, 0))
    const = lambda a: pl.BlockSpec(a.shape, lambda m: (0, 0), pipeline_mode=pl.Buffered(1))
    gf = g_ffn.reshape(1, D_MODEL)
    by_residue = seq_len is not None
    scratch = []
    if by_residue:
        tps = seq_len // tm
        att_specs = [pl.BlockSpec((None, a.shape[1], tm // a.shape[1], a.shape[3]),
                                  lambda m: (m // tps, 0, m % tps, 0)) for a in (*os_, *lses)]
        scratch = [pltpu.VMEM((a.shape[3] // LANES, tm, LANES), F32) for a in (*os_, *lses)]
    else:
        att_specs = [row(a.shape[1]) for a in (*os_, *lses)]
    return pl.pallas_call(
        functools.partial(_epilogue_kernel, by_residue=by_residue),
        grid=(M // tm,),
        in_specs=[
            row(D_MODEL),
            pl.BlockSpec((tm, D_MODEL), lambda m: (m, 0)),
            pl.BlockSpec((tm, D_MODEL), lambda m: (m, 1)),
            row(RET_WIDTH), *att_specs,
            const(w_ret_o), const(w_att_o), const(w_o), const(gf),
        ],
        out_specs=[row(D_MODEL), row(D_MODEL)],
        out_shape=[jax.ShapeDtypeStruct((M, D_MODEL), F32), jax.ShapeDtypeStruct((M, D_MODEL), BF16)],
        scratch_shapes=scratch,
        compiler_params=_params(("arbitrary",), 56),
        name="epilogue",
    )(x2d, proj2, proj2, yret, *os_, *lses, w_ret_o, w_att_o, w_o, gf)


def _gelu_gate(a, half_b):
    return (a * (1.0 + lax.erf(a * math.sqrt(0.5)))) * half_b


def _halve_b_half(x):
    return jnp.concatenate([x[..., :D_FF], 0.5 * x[..., D_FF:]], axis=-1)


def _conv3(u, r1, r2, w_ref, b_ref):
    return b_ref[...] + (r2 * w_ref[0:1, :] + r1 * w_ref[1:2, :] + u * w_ref[2:3, :])


UP_CHUNK = 256


SHIFT_CHUNK_ROWS = 4096


def _shift_plan(cache_rows, T, steps):
    nb, n, _ = cache_rows.shape
    shift = T * 2 * ATT_HG
    body = n - shift
    per_seq = -(-body // SHIFT_CHUNK_ROWS)
    while body % per_seq or (body // per_seq) % 8:
        per_seq += 1
    plan = dict(shift=shift, chunk=body // per_seq, per_seq=per_seq, count=nb * per_seq)
    assert plan["count"] < steps, "the background copy needs one grid step per chunk plus one to drain"
    return plan


def _shift_copy_step(s, src_ref, new_ref, dst_ref, buf, sems, i, *, shift, chunk, per_seq, count):
    n = src_ref.shape[1]
    tail = pltpu.make_async_copy(new_ref, dst_ref.at[:, pl.ds(n - shift, shift)], sems.at[i, 4])

    @pl.when(s == 0)
    def _():
        tail.start()

    @pl.when(s == count)
    def _():
        tail.wait()

    def read(c, slot):
        start = pl.multiple_of(shift + (c % per_seq) * chunk, 8)
        return pltpu.make_async_copy(src_ref.at[c // per_seq, pl.ds(start, chunk)], buf.at[slot], sems.at[i, slot])

    def write(c, slot):
        start = pl.multiple_of((c % per_seq) * chunk, 8)
        return pltpu.make_async_copy(buf.at[slot], dst_ref.at[c // per_seq, pl.ds(start, chunk)],
                                     sems.at[i, 2 + slot])

    slot = s % 2

    @pl.when(s == 0)
    def _():
        read(0, 0).start()

    @pl.when(s < count)
    def _():
        read(s, slot).wait()
        write(s, slot).start()

    @pl.when((s >= 1) & (s <= count))
    def _():
        write(s - 1, 1 - slot).wait()

    @pl.when(s + 1 < count)
    def _():
        read(s + 1, 1 - slot).start()


def _up_gate_kernel(h_ref, w_ref, cwa_ref, cwb_ref, cba_ref, cbb_ref, *rest, tiles_per_seq, shift_plans):
    ng = len(shift_plans)
    caches, new_rows, rest = rest[:ng], rest[ng:2 * ng], rest[2 * ng:]
    g_ref, ta_ref, tb_ref = rest[:3]
    advanced, rest = rest[3:3 + ng], rest[3 + ng:]
    u_scr, carry_scr = rest[:2]
    bufs, (sems,) = rest[2:2 + ng], rest[2 + ng:]
    m = pl.program_id(0)
    f = pl.program_id(1)
    tm = h_ref.shape[0]

    step = m * pl.num_programs(1) + f
    for i, plan in enumerate(shift_plans):
        _shift_copy_step(step, caches[i], new_rows[i], advanced[i], bufs[i], sems, i, **plan)

    @pl.when(m % tiles_per_seq == 0)
    def _():
        carry_scr[f] = jnp.zeros(carry_scr.shape[1:], F32)

    u_scr[0:8, :] = carry_scr[f]
    u_scr[8:, :] = _dot(h_ref[...], w_ref[...])
    carry_scr[f] = u_scr[tm:, :]
    for c in range(g_ref.shape[1] // UP_CHUNK):
        cols = slice(c * UP_CHUNK, (c + 1) * UP_CHUNK)

        def conv(half, cw_ref, cb_ref, tail_ref):
            pcols = slice((2 * c + half) * UP_CHUNK, (2 * c + half + 1) * UP_CHUNK)
            tail_ref[:, cols] = u_scr[tm:, pcols]
            u, r1, r2 = u_scr[8:, pcols], u_scr[7:tm + 7, pcols], u_scr[6:tm + 6, pcols]
            return cb_ref[:, cols] + (r2 * cw_ref[0:1, cols] + r1 * cw_ref[1:2, cols] + u * cw_ref[2:3, cols])

        a = conv(0, cwa_ref, cba_ref, ta_ref)
        b = conv(1, cwb_ref, cbb_ref, tb_ref)
        g_ref[:, cols] = _gelu_gate(a, b).astype(BF16)


def _up_gate(h2, w_up_ab, conv_w, conv_b, caches, new_rows, T, *, tm, tiles_per_seq):
    M = h2.shape[0]
    tf = 512
    nf = D_FF // tf
    cb = conv_b.reshape(1, 2 * D_FF)
    grid = (M // tm, nf)
    plans = tuple(_shift_plan(c, T, grid[0] * grid[1]) for c in caches)
    any_spec = pl.BlockSpec(memory_space=pl.ANY)
    return pl.pallas_call(
        functools.partial(_up_gate_kernel, tiles_per_seq=tiles_per_seq, shift_plans=plans),
        grid=grid,
        in_specs=[
            pl.BlockSpec((tm, D_MODEL), lambda m, f: (m, 0)),
            pl.BlockSpec((D_MODEL, 2 * tf), lambda m, f: (0, f)),
            pl.BlockSpec((CONV_W, tf), lambda m, f: (0, f)),
            pl.BlockSpec((CONV_W, tf), lambda m, f: (0, f + nf)),
            pl.BlockSpec((1, tf), lambda m, f: (0, f)),
            pl.BlockSpec((1, tf), lambda m, f: (0, f + nf)),
            *[any_spec for _ in (*caches, *new_rows)],
        ],
        out_specs=[
            pl.BlockSpec((tm, tf), lambda m, f: (m, f)),
            pl.BlockSpec((None, 8, tf), lambda m, f: (m, 0, f)),
            pl.BlockSpec((None, 8, tf), lambda m, f: (m, 0, f)),
            *[any_spec for _ in caches],
        ],
        out_shape=[
            jax.ShapeDtypeStruct((M, D_FF), BF16),
            jax.ShapeDtypeStruct((M // tm, 8, D_FF), F32),
            jax.ShapeDtypeStruct((M // tm, 8, D_FF), F32),
            *[jax.ShapeDtypeStruct(c.shape, F32) for c in caches],
        ],
        scratch_shapes=[
            pltpu.VMEM((tm + 8, 2 * tf), F32),
            pltpu.VMEM((nf, 8, 2 * tf), F32),
            *[pltpu.VMEM((2, p["chunk"], ATT_HD), F32) for p in plans],
            pltpu.SemaphoreType.DMA((len(caches), 5)),
        ],
        compiler_params=_params(("arbitrary", "arbitrary"), 56),
        name="up_gate",
    )(h2, w_up_ab, conv_w, conv_w, cb, cb, *caches, *new_rows)


def _down_kernel(g_ref, wd_ref, x1_ref, gfin_ref, y_ref, *, final_norm):
    y = x1_ref[...] + _dot(g_ref[...], wd_ref[...])
    y_ref[...] = _rmsnorm_rows(y, gfin_ref[...]) if final_norm else y


def _down(g, x1, w_down, g_final, *, tm, final_norm):
    M = g.shape[0]
    return pl.pallas_call(
        functools.partial(_down_kernel, final_norm=final_norm),
        grid=(M // tm,),
        in_specs=[
            pl.BlockSpec((tm, D_FF), lambda m: (m, 0)),
            pl.BlockSpec((D_FF, D_MODEL), lambda m: (0, 0), pipeline_mode=pl.Buffered(1)),
            pl.BlockSpec((tm, D_MODEL), lambda m: (m, 0)),
            pl.BlockSpec((1, D_MODEL), lambda m: (0, 0)),
        ],
        out_specs=pl.BlockSpec((tm, D_MODEL), lambda m: (m, 0)),
        out_shape=jax.ShapeDtypeStruct((M, D_MODEL), F32),
        compiler_params=_params(("arbitrary",), 56),
        name="down",
    )(g, w_down, x1, g_final.reshape(1, D_MODEL))


def _ffn_sample_kernel(h_ref, wua_ref, wub_ref, s0a_ref, s0b_ref, s1a_ref, s1b_ref, cwa_ref, cwb_ref,
                       cba_ref, cbb_ref, wd_ref, x1_ref, gfin_ref, y_ref, ua_ref, ub_ref, wu_cast_ref, wd_cast_ref,
                       *, T, final_norm):
    f = pl.program_id(0)
    M, tf = ua_ref.shape
    t = lax.broadcasted_iota(jnp.int32, (M, tf), 0) & (T - 1)
    wua, wub, wd = wua_ref[...].astype(BF16), wub_ref[...].astype(BF16), wd_ref[...].astype(BF16)
    for c in range(tf // UP_CHUNK):
        cols = slice(c * UP_CHUNK, (c + 1) * UP_CHUNK)
        wu_cast_ref[:, 2 * c * UP_CHUNK:(2 * c + 1) * UP_CHUNK] = wua[:, cols]
        wu_cast_ref[:, (2 * c + 1) * UP_CHUNK:(2 * c + 2) * UP_CHUNK] = wub[:, cols]
    wd_cast_ref[...] = wd
    h = h_ref[...]
    per_row = lambda s_ref: jnp.broadcast_to(s_ref[...][:, None, :], (M // T, T, tf)).reshape(M, tf)

    def conv(u, s0_ref, s1_ref, w_ref, b_ref, u_out_ref):
        u_out_ref[...] = u
        s0, s1 = per_row(s0_ref), per_row(s1_ref)
        r1 = jnp.where(t >= 1, pltpu.roll(u, 1, axis=0), s1)
        r2 = jnp.where(t >= 2, pltpu.roll(u, 2, axis=0), jnp.where(t == 0, s0, s1))
        return _conv3(u, r1, r2, w_ref, b_ref)

    a = conv(_dot(h, wua), s0a_ref, s1a_ref, cwa_ref, cba_ref, ua_ref)
    b = conv(_dot(h, wub), s0b_ref, s1b_ref, cwb_ref, cbb_ref, ub_ref)
    part = _dot(_gelu_gate(a, b).astype(BF16), wd)

    @pl.when(f == 0)
    def _():
        y_ref[...] = x1_ref[...] + part

    @pl.when(f > 0)
    def _():
        y_ref[...] += part

    if final_norm:
        @pl.when(f == pl.num_programs(0) - 1)
        def _():
            y_ref[...] = _rmsnorm_rows(y_ref[...], gfin_ref[...])


def _ffn_sample(h2, x1, state_conv, w_up, conv_w, conv_b, w_down, g_final, T, final_norm):
    M = h2.shape[0]
    nb = M // T
    tf = 512
    nf = D_FF // tf
    F2 = 2 * D_FF
    assert state_conv.shape[1] == CONV_W - 1 == 2
    s0, s1 = state_conv[:, 0].astype(F32), state_conv[:, 1].astype(F32)
    cb = conv_b.reshape(1, F2)
    gfin = g_final.reshape(1, D_MODEL)
    full = lambda w: pl.BlockSpec((M, w), lambda f: (0, 0))
    ca = lambda r, w=tf: pl.BlockSpec((r, w), lambda f: (0, f))
    cbk = lambda r, w=tf: pl.BlockSpec((r, w), lambda f: (0, f + nf))
    down_rows = pl.BlockSpec((tf, D_MODEL), lambda f: (f, 0))
    y, ua, ub, w_up_ab, w_down_bf16 = pl.pallas_call(
        functools.partial(_ffn_sample_kernel, T=T, final_norm=final_norm),
        grid=(nf,),
        in_specs=[
            full(D_MODEL), ca(D_MODEL), cbk(D_MODEL), ca(nb), cbk(nb), ca(nb), cbk(nb),
            ca(CONV_W), cbk(CONV_W), ca(1), cbk(1),
            down_rows,
            full(D_MODEL), pl.BlockSpec((1, D_MODEL), lambda f: (0, 0)),
        ],
        out_specs=[full(D_MODEL), ca(M), ca(M), ca(D_MODEL, 2 * tf), down_rows],
        out_shape=[
            jax.ShapeDtypeStruct((M, D_MODEL), F32),
            jax.ShapeDtypeStruct((M, D_FF), F32),
            jax.ShapeDtypeStruct((M, D_FF), F32),
            jax.ShapeDtypeStruct((D_MODEL, F2), BF16),
            jax.ShapeDtypeStruct((D_FF, D_MODEL), BF16),
        ],
        compiler_params=_params(("arbitrary",), 56),
        name="ffn_sample",
    )(h2, w_up, w_up, s0, s0, s1, s1, conv_w, conv_w, cb, cb, w_down, x1, gfin)
    u = jnp.concatenate([ua, ub], axis=-1).reshape(nb, T, F2)
    return y, u[:, T - (CONV_W - 1):], w_up_ab, w_down_bf16


def kernel(x_prompt, x_sample, state_ret, cache_kv_w128, cache_kv_w512, cache_kv_w2048, state_conv, g_mix, w_in,
           gn_ret, w_ret_o, w_att_o, w_o, g_ffn, w_up, conv_w, conv_b, w_down, g_final):
    B, S, _ = x_prompt.shape
    NB, T, _ = x_sample.shape
    depth = w_in.shape[0]
    caches = (cache_kv_w128, cache_kv_w512, cache_kv_w2048)
    TM = 1024
    tiles_per_seq = S // TM

    xp = x_prompt.reshape(B * S, D_MODEL)
    xs = x_sample.reshape(NB * T, D_MODEL)
    ret_p, ret_s, conv_p, conv_s = [], [], [], []
    kv_p = [[] for _ in range(N_GROUPS)]
    kv_s = [[] for _ in range(N_GROUPS)]
    for l in range(depth):
        w_ret_o_l = w_ret_o[l].astype(BF16)
        w_att_o_l = w_att_o[l].astype(BF16)
        w_o_l = w_o[l].astype(BF16)
        conv_w_l, conv_b_l = _halve_b_half(conv_w[l]), _halve_b_half(conv_b[l])

        ret_tiles = 4 * RET_WIDTH // CAST_TN
        att_tiles = ATT_COLS // CAST_TN
        in_tiles = IN_COLS // CAST_TN
        proj_s, w_main_l = _inproj_sample(xs, g_mix[l], w_in[l], lambda n: (n + ret_tiles + att_tiles) % in_tiles,
                                          MAIN_COLS // CAST_TN, "inproj_sample_main")
        att_s, w_att_l = _inproj_sample(xs, g_mix[l], w_in[l], lambda n: n + ret_tiles, att_tiles,
                                        "inproj_sample_att")
        yret_s, st_s = _ret_sample(proj_s, state_ret[l], gn_ret[l], T)
        ret_s.append(st_s)
        os_, lses, new_rows = [], [], []
        for g in range(N_GROUPS):
            o, lse, rows = _att_sample(att_s, caches[g][l], g, T)
            os_.append(o)
            lses.append(lse)
            new_rows.append(rows)
        x1s, h2s = _epilogue(xs, proj_s, yret_s, os_, lses, w_ret_o_l, w_att_o_l, w_o_l, g_ffn[l], tm=NB * T)
        xs, cv, w_up_l, w_down_l = _ffn_sample(h2s, x1s, state_conv[l], w_up[l], conv_w_l, conv_b_l, w_down[l],
                                               g_final, T, final_norm=l == depth - 1)
        conv_s.append(cv)

        proj, h = _inproj_main(xp, g_mix[l], w_main_l, tm=TM)
        yret, st = _ret_prompt(proj.reshape(B, S, MAIN_COLS), gn_ret[l])
        ret_p.append(st)
        os_, lses = [], []
        for g, (win, _) in enumerate(ATT_GROUPS):
            qkv, kvf = _inproj_att(h, w_att_l, g, B=B, S=S, tm=512)
            o, lse = _att_prompt(qkv, g)
            os_.append(o)
            lses.append(lse)
            kv_p[g].append(kvf.reshape(B, min(win, S), 2, ATT_HG, ATT_HD))
        x1, h2 = _epilogue(xp, proj, yret.reshape(B * S, RET_WIDTH), os_, lses,
                           w_ret_o_l, w_att_o_l, w_o_l, g_ffn[l], tm=256, seq_len=S)
        gated, tail_a, tail_b, *advanced = _up_gate(h2, w_up_l, conv_w_l, conv_b_l,
                                                    [_cache_rows(c[l]) for c in caches], new_rows, T,
                                                    tm=TM, tiles_per_seq=tiles_per_seq)
        for g in range(N_GROUPS):
            kv_s[g].append(advanced[g].reshape(caches[g][l].shape))
        utail = jnp.concatenate([tail_a, tail_b], axis=-1)
        conv_p.append(utail[tiles_per_seq - 1::tiles_per_seq, 8 - (CONV_W - 1):])
        xp = _down(gated, x1, w_down_l, g_final, tm=512, final_norm=l == depth - 1)

    return (xp.reshape(B, S, D_MODEL), xs.reshape(NB, T, D_MODEL),
            jnp.stack(ret_p), jnp.stack(ret_s),
            jnp.stack(kv_p[0]), jnp.stack(kv_s[0]),
            jnp.stack(kv_p[1]), jnp.stack(kv_s[1]),
            jnp.stack(kv_p[2]), jnp.stack(kv_s[2]),
            jnp.stack(conv_p), jnp.stack(conv_s))
```
